```python
import math
import jax
import jax.numpy as jnp
from jax import lax
import numpy as np

D_MODEL = 2048
BATCH = 4
SEQ = 4096
DEPTH = 2

CTX_LEN = 256
GRID_W = 64
N_BRANCH = 4
BRANCH_W = 1024
ROPE_DIM = 64
ROPE_BASE = 10000.0
Q_BLOCK = 128

MLA_HEADS = 8
MLA_NOPE = 128
MLA_ROPE = ROPE_DIM
MLA_V = 128
MLA_Q_LORA = 512
MLA_KV_LORA = 256

S5_GROUP = 16
S5_GROUPS = BRANCH_W // S5_GROUP
S5_STATE = 64
S5_DT_MIN = 0.001
S5_DT_MAX = 0.1

SSD_HEAD_DIM = 64
SSD_HEADS = BRANCH_W // SSD_HEAD_DIM
SSD_GROUPS = 4
SSD_STATE = 128
SSD_CONV = 3
SSD_CHUNK = 128
SSD_CONV_CH = BRANCH_W + 2 * SSD_GROUPS * SSD_STATE
SSD_DT_MIN = 0.001
SSD_DT_MAX = 0.1

DIFF_HEAD_DIM = ROPE_DIM
DIFF_HEADS = BRANCH_W // (2 * DIFF_HEAD_DIM)

LN_EPS = 1e-5
RMS_EPS = 1e-6
DEEPNORM_ALPHA = (2 * DEPTH) ** 0.25
DEEPNORM_BETA = (8 * DEPTH) ** -0.25

IN_SPLITS = (
    MLA_Q_LORA, MLA_KV_LORA, MLA_ROPE, BRANCH_W,
    BRANCH_W, BRANCH_W,
    BRANCH_W, SSD_CONV_CH, SSD_HEADS,
    BRANCH_W, BRANCH_W, BRANCH_W, BRANCH_W,
    N_BRANCH * D_MODEL,
)
SPLIT_POINTS = tuple(sum(IN_SPLITS[: i + 1]) for i in range(len(IN_SPLITS) - 1))
N_IN = sum(IN_SPLITS)

kernel_name = 'hybrid_mla_s5_ssd_diffattn_flow_block'


def layer_norm(x):
    x32 = x.astype(jnp.float32)
    mu = jnp.mean(x32, -1, keepdims=True)
    var = jnp.mean(jnp.square(x32 - mu), -1, keepdims=True)
    return ((x32 - mu) * lax.rsqrt(var + LN_EPS)).astype(x.dtype)


def rms_norm(x, g):
    x32 = x.astype(jnp.float32)
    y = x32 * lax.rsqrt(jnp.mean(jnp.square(x32), -1, keepdims=True) + RMS_EPS)
    return y.astype(x.dtype) * g


def maybe_flip(t, flip):
    return t[:, ::-1] if flip else t


def flatten_heads(y):
    return y.reshape(y.shape[0], y.shape[1], -1)


def rope_tables(row_id, col_id, dim):
    quarter = dim // 4
    inv_freq = ROPE_BASE ** (-jnp.arange(quarter, dtype=jnp.float32) / quarter)
    ang = jnp.concatenate([row_id[:, None] * inv_freq, col_id[:, None] * inv_freq], axis=-1)
    return jnp.cos(ang), jnp.sin(ang)


def apply_rope(x, cos, sin):
    xp = x.reshape(*x.shape[:-1], -1, 2)
    xr, xi = xp[..., 0], xp[..., 1]
    c = cos[:, None, :].astype(x.dtype)
    s = sin[:, None, :].astype(x.dtype)
    return jnp.stack([xr * c - xi * s, xr * s + xi * c], axis=-1).reshape(x.shape)


def sweep_query_blocks(fn, q):
    b, t = q.shape[:2]
    nb = t // Q_BLOCK
    blocks = jnp.moveaxis(q.reshape(b, nb, Q_BLOCK, *q.shape[2:]), 1, 0)
    out = lax.map(fn, blocks)
    return jnp.moveaxis(out, 0, 1).reshape(b, t, *out.shape[3:])


def softmax_attend(q, k, v, scale):
    s = jnp.einsum('bqhd,bkhd->bhqk', q, k).astype(jnp.float32) * scale
    p = jax.nn.softmax(s, axis=-1).astype(v.dtype)
    return jnp.einsum('bhqk,bkhd->bqhd', p, v)


def mla_queries(cq, q_norm, w_uq, rope):
    b, t = cq.shape[:2]
    q = (rms_norm(cq, q_norm) @ w_uq).reshape(b, t, MLA_HEADS, MLA_NOPE + MLA_ROPE)
    if rope is None:
        return q
    return jnp.concatenate([q[..., :MLA_NOPE], apply_rope(q[..., MLA_NOPE:], *rope)], axis=-1)


def mla_keys_values(ckv, kr, kv_norm, w_ukv, rope):
    b, t = ckv.shape[:2]
    kv = (rms_norm(ckv, kv_norm) @ w_ukv).reshape(b, t, MLA_HEADS, MLA_NOPE + MLA_V)
    k_rope = kr[:, :, None, :]
    if rope is not None:
        k_rope = apply_rope(k_rope, *rope)
    k_rope = jnp.broadcast_to(k_rope, (b, t, MLA_HEADS, MLA_ROPE))
    k = jnp.concatenate([kv[..., :MLA_NOPE], k_rope], axis=-1)
    return k, kv[..., MLA_NOPE:]


def mla_mixer(lat, ctx, q_norm, w_uq, kv_norm, w_ukv, rope, ctx_out):
    cq_l, ckv_l, kr_l = lat
    cq_c, ckv_c, kr_c = ctx
    scale = (MLA_NOPE + MLA_ROPE) ** -0.5
    k_c, v_c = mla_keys_values(ckv_c, kr_c, kv_norm, w_ukv, None)
    k_l, v_l = mla_keys_values(ckv_l, kr_l, kv_norm, w_ukv, rope)
    k_all = jnp.concatenate([k_l, k_c], axis=1)
    v_all = jnp.concatenate([v_l, v_c], axis=1)
    q_l = mla_queries(cq_l, q_norm, w_uq, rope)
    y_l = sweep_query_blocks(lambda qb: softmax_attend(qb, k_all, v_all, scale), q_l)
    if not ctx_out:
        return flatten_heads(y_l), None
    y_c = softmax_attend(mla_queries(cq_c, q_norm, w_uq, None), k_c, v_c, scale)
    return flatten_heads(y_l), flatten_heads(y_c)


def s5_discretise(lam_re, lam_im, log_dt, b_re, b_im):
    lr = lam_re.astype(jnp.float32)
    li = lam_im.astype(jnp.float32)
    dt = jnp.exp(log_dt.astype(jnp.float32))[:, None]
    mag = jnp.exp(lr * dt)
    a_re = mag * jnp.cos(li * dt)
    a_im = mag * jnp.sin(li * dt)
    den = lr * lr + li * li
    f_re = ((a_re - 1.0) * lr + a_im * li) / den
    f_im = (a_im * lr - (a_re - 1.0) * li) / den
    br = b_re.astype(jnp.float32)
    bi = b_im.astype(jnp.float32)
    bb_re = f_re[..., None] * br - f_im[..., None] * bi
    bb_im = f_re[..., None] * bi + f_im[..., None] * br
    return a_re, a_im, bb_re, bb_im


def complex_affine_combine(e1, e2):
    a1r, a1i, b1r, b1i = e1
    a2r, a2i, b2r, b2i = e2
    return (a2r * a1r - a2i * a1i, a2r * a1i + a2i * a1r,
            a2r * b1r - a2i * b1i + b2r, a2r * b1i + a2i * b1r + b2i)


def s5_states(u, h0_re, h0_im, a_re, a_im, bb_re, bb_im):
    u32 = u.astype(jnp.float32)
    bu_re = jnp.einsum('blgs,gps->blgp', u32, bb_re)
    bu_im = jnp.einsum('blgs,gps->blgp', u32, bb_im)
    bu_re = bu_re.at[:, 0].add(a_re * h0_re - a_im * h0_im)
    bu_im = bu_im.at[:, 0].add(a_re * h0_im + a_im * h0_re)
    n = u.shape[1]
    a_re_seq = jnp.broadcast_to(a_re, (1, n) + a_re.shape)
    a_im_seq = jnp.broadcast_to(a_im, (1, n) + a_im.shape)
    _, _, h_re, h_im = lax.associative_scan(
        complex_affine_combine, (a_re_seq, a_im_seq, bu_re, bu_im), axis=1)
    return h_re, h_im


def s5_readout(h_re, h_im, c_re, c_im):
    return (jnp.einsum('blgp,gsp->blgs', h_re, c_re.astype(jnp.float32))
            - jnp.einsum('blgp,gsp->blgs', h_im, c_im.astype(jnp.float32)))


def s5_glu(y, w_glu, b_glu):
    g = jax.nn.gelu(y)
    return g * jax.nn.sigmoid(g @ w_glu + b_glu)


def s5_mixer(u_l, u_c, lam_re, lam_im, log_dt, b_re, b_im, c_re, c_im, d, w_glu, b_glu, ctx_out):
    b = u_l.shape[0]
    ul = u_l.reshape(b, u_l.shape[1], S5_GROUPS, S5_GROUP)
    uc = u_c.reshape(b, u_c.shape[1], S5_GROUPS, S5_GROUP)
    d_g = d.astype(jnp.float32).reshape(S5_GROUPS, S5_GROUP)
    zero = jnp.zeros((b, S5_GROUPS, S5_STATE), jnp.float32)
    y_l = ul.astype(jnp.float32) * d_g
    y_c = uc.astype(jnp.float32) * d_g if ctx_out else None
    for direction in range(2):
        flip = direction == 1
        disc = s5_discretise(lam_re[direction], lam_im[direction], log_dt[direction],
                             b_re[direction], b_im[direction])
        cr, ci = c_re[direction], c_im[direction]
        hc_re, hc_im = s5_states(maybe_flip(uc, flip), zero, zero, *disc)
        hl_re, hl_im = s5_states(maybe_flip(ul, flip), hc_re[:, -1], hc_im[:, -1], *disc)
        y_l = y_l + maybe_flip(s5_readout(hl_re, hl_im, cr, ci), flip)
        if ctx_out:
            y_c = y_c + maybe_flip(s5_readout(hc_re, hc_im, cr, ci), flip)

    def finish(y, u):
        return s5_glu(y.reshape(u.shape).astype(u.dtype), w_glu, b_glu)

    return finish(y_l, u_l), (finish(y_c, u_c) if ctx_out else None)


def depthwise_conv_centred(x, w, bias):
    k, ch = w.shape
    y = lax.conv_general_dilated(
        x, w[:, None, :], window_strides=(1,), padding=[((k - 1) // 2, k // 2)],
        dimension_numbers=('NWC', 'WIO', 'NWC'), feature_group_count=ch)
    return y + bias


def ssd_chunked(x, dt, a, bm, cm, h0, with_output):
    b, n, nh, hp = x.shape
    ng, ns = bm.shape[2:]
    r = nh // ng
    q = SSD_CHUNK
    nc = n // q
    da = jnp.transpose((dt * a).reshape(b, nc, q, ng, r), (0, 3, 4, 1, 2))
    cum = jnp.cumsum(da, axis=-1)
    xdt = (x * dt[..., None]).reshape(b, nc, q, ng, r, hp)
    bc = bm.reshape(b, nc, q, ng, ns)
    cc = cm.reshape(b, nc, q, ng, ns)
    to_end = jnp.exp(cum[..., -1:] - cum)
    states = jnp.einsum('bcjgn,bgrcj,bcjgrp->bcgrpn', bc, to_end, xdt)
    chunk_decay = jnp.exp(cum[..., -1])

    def step(h, inp):
        dec, st = inp
        return dec[..., None, None] * h + st, h

    h_last, h_in = lax.scan(step, h0.reshape(b, ng, r, hp, ns),
                            (jnp.moveaxis(chunk_decay, -1, 0), jnp.moveaxis(states, 1, 0)))
    h_last = h_last.reshape(b, nh, hp, ns)
    if not with_output:
        return None, h_last
    h_in = jnp.moveaxis(h_in, 0, 1)
    lower = jnp.tril(jnp.ones((q, q), dtype=bool))
    decay = jnp.exp(jnp.where(lower, cum[..., :, None] - cum[..., None, :], -jnp.inf))
    cb = jnp.einsum('bcign,bcjgn->bgcij', cc, bc)
    y_diag = jnp.einsum('bgcij,bgrcij,bcjgrp->bcigrp', cb, decay, xdt)
    y_off = jnp.einsum('bcign,bcgrpn,bgrci->bcigrp', cc, h_in, jnp.exp(cum))
    return (y_diag + y_off).reshape(b, n, nh, hp), h_last


def ssd_prepare(xbc, dt_raw, conv_w, conv_b, dt_bias):
    b, n = xbc.shape[:2]
    xbc = jax.nn.silu(depthwise_conv_centred(xbc, conv_w, conv_b)).astype(jnp.float32)
    gn = SSD_GROUPS * SSD_STATE
    xs = xbc[..., :BRANCH_W].reshape(b, n, SSD_HEADS, SSD_HEAD_DIM)
    bm = xbc[..., BRANCH_W:BRANCH_W + gn].reshape(b, n, SSD_GROUPS, SSD_STATE)
    cm = xbc[..., BRANCH_W + gn:].reshape(b, n, SSD_GROUPS, SSD_STATE)
    dts = jax.nn.softplus(dt_raw.astype(jnp.float32)[None]
                          + dt_bias.astype(jnp.float32)[:, None, None, :])
    return xs, bm, cm, dts


def ssd_mixer(lat, ctx, conv_w, conv_b, dt_bias, a_log, d, norm_g, ctx_out):
    z_l, xbc_l, dtr_l = lat
    z_c, xbc_c, dtr_c = ctx
    a = -jnp.exp(a_log.astype(jnp.float32))
    xl, bl, cl, dtl = ssd_prepare(xbc_l, dtr_l, conv_w, conv_b, dt_bias)
    xc, bc, cc, dtc = ssd_prepare(xbc_c, dtr_c, conv_w, conv_b, dt_bias)
    b = xl.shape[0]
    zero = jnp.zeros((b, SSD_HEADS, SSD_HEAD_DIM, SSD_STATE), jnp.float32)
    d_h = d.astype(jnp.float32)[:, None]
    y_l = xl * d_h
    y_c = xc * d_h if ctx_out else None
    for direction in range(2):
        flip = direction == 1
        yc_dir, hc = ssd_chunked(maybe_flip(xc, flip), maybe_flip(dtc[direction], flip), a[direction],
                                 maybe_flip(bc, flip), maybe_flip(cc, flip), zero, ctx_out)
        yl_dir, _ = ssd_chunked(maybe_flip(xl, flip), maybe_flip(dtl[direction], flip), a[direction],
                                maybe_flip(bl, flip), maybe_flip(cl, flip), hc, True)
        y_l = y_l + maybe_flip(yl_dir, flip)
        if ctx_out:
            y_c = y_c + maybe_flip(yc_dir, flip)

    def finish(y, z):
        return rms_norm(y.reshape(z.shape).astype(z.dtype) * jax.nn.silu(z), norm_g)

    return finish(y_l, z_l), (finish(y_c, z_c) if ctx_out else None)


def diff_heads(t, rope):
    b, n = t.shape[:2]
    t = t.reshape(b, n, DIFF_HEADS * 2, DIFF_HEAD_DIM)
    if rope is not None:
        t = apply_rope(t, *rope)
    return t.reshape(b, n, DIFF_HEADS, 2, DIFF_HEAD_DIM)


def diff_values(v):
    return v.reshape(v.shape[0], v.shape[1], DIFF_HEADS, 2 * DIFF_HEAD_DIM)


def diff_attend(q, k, v, lam, scale):
    s = jnp.einsum('bqhcd,bkhcd->bhcqk', q, k).astype(jnp.float32) * scale
    p = jax.nn.softmax(s, axis=-1)
    w = (p[:, :, 0] - lam * p[:, :, 1]).astype(v.dtype)
    return jnp.einsum('bhqk,bkhd->bqhd', w, v)


def diff_mixer(lat, ctx, lam_q, lam_k, norm_g, lam_init, rope, ctx_out):
    q_l, k_l, v_l = lat
    q_c, k_c, v_c = ctx
    scale = DIFF_HEAD_DIM ** -0.5
    lq = lam_q.astype(jnp.float32)
    lk = lam_k.astype(jnp.float32)
    lam = jnp.exp(jnp.sum(lq[0] * lk[0])) - jnp.exp(jnp.sum(lq[1] * lk[1])) + lam_init
    kc = diff_heads(k_c, None)
    vc = diff_values(v_c)
    k_all = jnp.concatenate([diff_heads(k_l, rope), kc], axis=1)
    v_all = jnp.concatenate([diff_values(v_l), vc], axis=1)
    y_l = sweep_query_blocks(lambda qb: diff_attend(qb, k_all, v_all, lam, scale),
                             diff_heads(q_l, rope))

    def finish(y):
        return flatten_heads(rms_norm(y, norm_g) * (1.0 - lam_init))

    if not ctx_out:
        return finish(y_l), None
    y_c = diff_attend(diff_heads(q_c, None), kc, vc, lam, scale)
    return finish(y_l), finish(y_c)


def merge_branches(branches, gate_pre, b_merge, w_branch, w_out):
    merged = None
    for n, y in enumerate(branches):
        g = jax.nn.sigmoid(gate_pre[..., n * D_MODEL:(n + 1) * D_MODEL] + b_merge[n])
        term = g * (y @ w_branch[n])
        merged = term if merged is None else merged + term
    return merged @ w_out


def hybrid_layer(xl, xc, c, c_ctx, p, rope, layer_idx, ctx_out):
    mod_l = jax.nn.silu(c) @ p['w_ada'] + p['b_ada']
    mod_c = jax.nn.silu(c_ctx) @ p['w_ada'] + p['b_ada']
    shift_l, scale_l, gate_l = jnp.split(mod_l[:, None, :], 3, axis=-1)
    shift_c, scale_c, gate_c = jnp.split(mod_c, 3, axis=-1)
    hl = layer_norm(xl) * (1.0 + scale_l) + shift_l
    hc = layer_norm(xc) * (1.0 + scale_c) + shift_c
    (cq_l, ckv_l, kr_l, ga_l, u_l, gb_l, z_l, xbc_l, dt_l,
     qd_l, kd_l, vd_l, gd_l, gm_l) = jnp.split(hl @ p['w_in'], SPLIT_POINTS, axis=-1)
    (cq_c, ckv_c, kr_c, ga_c, u_c, gb_c, z_c, xbc_c, dt_c,
     qd_c, kd_c, vd_c, gd_c, gm_c) = jnp.split(hc @ p['w_in'], SPLIT_POINTS, axis=-1)

    ya_l, ya_c = mla_mixer((cq_l, ckv_l, kr_l), (cq_c, ckv_c, kr_c), p['mla_q_norm'],
                           p['mla_w_uq'], p['mla_kv_norm'], p['mla_w_ukv'], rope, ctx_out)
    yb_l, yb_c = s5_mixer(u_l, u_c, p['s5_lambda_re'], p['s5_lambda_im'], p['s5_log_dt'],
                          p['s5_b_re'], p['s5_b_im'], p['s5_c_re'], p['s5_c_im'], p['s5_d'],
                          p['s5_w_glu'], p['s5_b_glu'], ctx_out)
    yc_l, yc_c = ssd_mixer((z_l, xbc_l, dt_l), (z_c, xbc_c, dt_c), p['ssd_conv_w'], p['ssd_conv_b'],
                           p['ssd_dt_bias'], p['ssd_a_log'], p['ssd_d'], p['ssd_norm'], ctx_out)
    lam_init = 0.8 - 0.6 * math.exp(-0.3 * layer_idx)
    yd_l, yd_c = diff_mixer((qd_l, kd_l, vd_l), (qd_c, kd_c, vd_c), p['diff_lambda_q'],
                            p['diff_lambda_k'], p['diff_norm'], lam_init, rope, ctx_out)

    def merge(ya, yb, yc, yd, ga, gb, gd, gm):
        branches = (ya * jax.nn.silu(ga), yb * jax.nn.silu(gb), yc, yd * jax.nn.silu(gd))
        return merge_branches(branches, gm, p['b_merge'], p['w_branch'], p['w_out'])

    out_l = merge(ya_l, yb_l, yc_l, yd_l, ga_l, gb_l, gd_l, gm_l)
    xl_new = layer_norm(DEEPNORM_ALPHA * xl + gate_l * out_l) * p['ln_g'] + p['ln_b']
    if not ctx_out:
        return xl_new, None
    out_c = merge(ya_c, yb_c, yc_c, yd_c, ga_c, gb_c, gd_c, gm_c)
    xc_new = layer_norm(DEEPNORM_ALPHA * xc + gate_c * out_c) * p['ln_g'] + p['ln_b']
    return xl_new, xc_new


def setup_inputs(seed: int = 0) -> dict:
    key = jax.random.key(seed)
    keys = iter(jax.random.split(key, 48))
    f32 = jnp.float32

    def normal(shape, scale):
        return scale * jax.random.normal(next(keys), shape, f32)

    def uniform(shape, lo, hi):
        return jax.random.uniform(next(keys), shape, f32, lo, hi)

    def gain(shape):
        return 1.0 + normal(shape, 0.01)

    L, D = DEPTH, D_MODEL
    G, P, S = S5_GROUPS, S5_STATE, S5_GROUP
    ssd_dt = jnp.exp(uniform((L, 2, SSD_HEADS), math.log(SSD_DT_MIN), math.log(SSD_DT_MAX)))
    return {
        'x': normal((BATCH, SEQ, D), 1.0),
        'c': normal((BATCH, D), 1.0),
        'ctx': normal((BATCH, CTX_LEN, D), 1.0),
        'c_ctx': normal((D,), 1.0),
        'w_ada': normal((L, D, 3 * D), D ** -0.5),
        'b_ada': normal((L, 3 * D), 0.01),
        'w_in': normal((L, D, N_IN), D ** -0.5),
        'mla_q_norm': gain((L, MLA_Q_LORA)),
        'mla_w_uq': normal((L, MLA_Q_LORA, MLA_HEADS * (MLA_NOPE + MLA_ROPE)), MLA_Q_LORA ** -0.5),
        'mla_kv_norm': gain((L, MLA_KV_LORA)),
        'mla_w_ukv': normal((L, MLA_KV_LORA, MLA_HEADS * (MLA_NOPE + MLA_V)), MLA_KV_LORA ** -0.5),
        's5_lambda_re': -0.5 + normal((L, 2, G, P), 0.01),
        's5_lambda_im': math.pi * jnp.arange(P, dtype=f32) + normal((L, 2, G, P), 0.01),
        's5_log_dt': uniform((L, 2, G), math.log(S5_DT_MIN), math.log(S5_DT_MAX)),
        's5_b_re': normal((L, 2, G, P, S), (2 * S) ** -0.5),
        's5_b_im': normal((L, 2, G, P, S), (2 * S) ** -0.5),
        's5_c_re': normal((L, 2, G, S, P), (2 * P) ** -0.5),
        's5_c_im': normal((L, 2, G, S, P), (2 * P) ** -0.5),
        's5_d': normal((L, BRANCH_W), 1.0),
        's5_w_glu': normal((L, BRANCH_W, BRANCH_W), BRANCH_W ** -0.5),
        's5_b_glu': normal((L, BRANCH_W), 0.01),
        'ssd_conv_w': normal((L, SSD_CONV, SSD_CONV_CH), SSD_CONV ** -0.5),
        'ssd_conv_b': normal((L, SSD_CONV_CH), 0.01),
        'ssd_dt_bias': ssd_dt + jnp.log(-jnp.expm1(-ssd_dt)),
        'ssd_a_log': jnp.log(uniform((L, 2, SSD_HEADS), 1.0, 16.0)),
        'ssd_d': gain((L, SSD_HEADS)),
        'ssd_norm': gain((L, BRANCH_W)),
        'diff_lambda_q': normal((L, 2, DIFF_HEAD_DIM), 0.1),
        'diff_lambda_k': normal((L, 2, DIFF_HEAD_DIM), 0.1),
        'diff_norm': gain((L, 2 * DIFF_HEAD_DIM)),
        'b_merge': normal((L, N_BRANCH, D), 0.01),
        'w_branch': normal((L, N_BRANCH, BRANCH_W, D), DEEPNORM_BETA * BRANCH_W ** -0.5),
        'w_out': normal((L, D, D), DEEPNORM_BETA * D ** -0.5),
        'ln_g': gain((L, D)),
        'ln_b': normal((L, D), 0.01),
    }


def reference(x, c, ctx, c_ctx, w_ada, b_ada, w_in, mla_q_norm, mla_w_uq, mla_kv_norm, mla_w_ukv,
              s5_lambda_re, s5_lambda_im, s5_log_dt, s5_b_re, s5_b_im, s5_c_re, s5_c_im, s5_d,
              s5_w_glu, s5_b_glu, ssd_conv_w, ssd_conv_b, ssd_dt_bias, ssd_a_log, ssd_d, ssd_norm,
              diff_lambda_q, diff_lambda_k, diff_norm, b_merge, w_branch, w_out, ln_g, ln_b):
    rows = x.shape[1] // GRID_W
    row_id = jnp.repeat(jnp.arange(rows, dtype=jnp.float32), GRID_W)
    col_id = jnp.tile(jnp.arange(GRID_W, dtype=jnp.float32), rows)
    rope = rope_tables(row_id, col_id, ROPE_DIM)
    xl, xc = x, ctx
    for i in range(DEPTH):
        p = dict(
            w_ada=w_ada[i], b_ada=b_ada[i], w_in=w_in[i],
            mla_q_norm=mla_q_norm[i], mla_w_uq=mla_w_uq[i],
            mla_kv_norm=mla_kv_norm[i], mla_w_ukv=mla_w_ukv[i],
            s5_lambda_re=s5_lambda_re[i], s5_lambda_im=s5_lambda_im[i], s5_log_dt=s5_log_dt[i],
            s5_b_re=s5_b_re[i], s5_b_im=s5_b_im[i], s5_c_re=s5_c_re[i], s5_c_im=s5_c_im[i],
            s5_d=s5_d[i], s5_w_glu=s5_w_glu[i], s5_b_glu=s5_b_glu[i],
            ssd_conv_w=ssd_conv_w[i], ssd_conv_b=ssd_conv_b[i], ssd_dt_bias=ssd_dt_bias[i],
            ssd_a_log=ssd_a_log[i], ssd_d=ssd_d[i], ssd_norm=ssd_norm[i],
            diff_lambda_q=diff_lambda_q[i], diff_lambda_k=diff_lambda_k[i], diff_norm=diff_norm[i],
            b_merge=b_merge[i], w_branch=w_branch[i], w_out=w_out[i], ln_g=ln_g[i], ln_b=ln_b[i],
        )
        xl, xc = hybrid_layer(xl, xc, c, c_ctx, p, rope, i, i < DEPTH - 1)
    return xl
```

```python
import functools
import math

import numpy as np
import jax
import jax.numpy as jnp
from jax import lax
from jax.experimental import pallas as pl
from jax.experimental.pallas import tpu as pltpu

F32 = jnp.float32
BF16 = jnp.bfloat16

D_MODEL = 2048
DEPTH = 2
GRID_W = 64
N_BRANCH = 4
BRANCH_W = 1024
ROPE_DIM = 64
ROPE_BASE = 10000.0

MLA_HEADS = 8
MLA_NOPE = 128
MLA_ROPE = ROPE_DIM
MLA_V = 128
MLA_Q_LORA = 512
MLA_KV_LORA = 256

S5_GROUP = 16
S5_GROUPS = BRANCH_W // S5_GROUP
S5_STATE = 64
S5_CHUNK = 16

SSD_HEAD_DIM = 64
SSD_HEADS = BRANCH_W // SSD_HEAD_DIM
SSD_GROUPS = 4
SSD_STATE = 128
SSD_CHUNK = 128
SSD_CONV_CH = BRANCH_W + 2 * SSD_GROUPS * SSD_STATE

DIFF_HEAD_DIM = ROPE_DIM
DIFF_HEADS = BRANCH_W // (2 * DIFF_HEAD_DIM)

LN_EPS = 1e-5
RMS_EPS = 1e-6
DEEPNORM_ALPHA = (2 * DEPTH) ** 0.25

V7X_VMEM_LIMIT_BYTES = 56 * 1024 * 1024
Q_BLOCK = 256

OFF_GM = 0
OFF_XBC = 8192
OFF_GA = 10240
OFF_U = 11264
OFF_GB = 12288
OFF_Z = 13312
OFF_QD = 14336
OFF_KD = 15360
OFF_VD = 16384
OFF_GD = 17408
OFF_CQ = 18432
OFF_CKV = 18944
OFF_KR = 19200
OFF_DT = 19328
N_PACK = 19456
IN_TILE_N = 1024


def _sigmoid(x):
    return 1.0 / (1.0 + jnp.exp(-x))


def _silu(x):
    return x * _sigmoid(x)


def _gelu_tanh(x):
    return 0.5 * x * (1.0 + jnp.tanh(math.sqrt(2.0 / math.pi) * (x + 0.044715 * (x * x * x))))


def _softplus(x):
    return jnp.maximum(x, 0.0) + jnp.log(1.0 + jnp.exp(-jnp.abs(x)))


def _params(*sem):
    return pltpu.CompilerParams(dimension_semantics=sem, vmem_limit_bytes=V7X_VMEM_LIMIT_BYTES)


def _ada_kernel(c_ref, w_ref, b_ref, o_ref):
    s = _silu(c_ref[...])
    o_ref[...] = jnp.dot(s.astype(BF16), w_ref[...].astype(BF16),
                         preferred_element_type=F32) + b_ref[...]


def _ada(cs, w_ada, b_ada):
    n = w_ada.shape[1]
    tn = 512
    return pl.pallas_call(
        _ada_kernel,
        grid=(n // tn,),
        in_specs=[pl.BlockSpec((8, D_MODEL), lambda j: (0, 0)),
                  pl.BlockSpec((D_MODEL, tn), lambda j: (0, j)),
                  pl.BlockSpec((1, tn), lambda j: (0, j))],
        out_specs=pl.BlockSpec((8, tn), lambda j: (0, j)),
        out_shape=jax.ShapeDtypeStruct((8, n), F32),
        compiler_params=_params("arbitrary"),
        name="ada",
    )(cs, w_ada, b_ada.reshape(1, n))


def _inproj_kernel(x_ref, mod_ref, w_ref, o_ref, h_ref, *, tc, tm):
    i = pl.program_id(1)

    @pl.when(pl.program_id(2) == 0)
    def _():
        x = x_ref[0]
        mu = jnp.mean(x, axis=-1, keepdims=True)
        xc = x - mu
        var = jnp.mean(xc * xc, axis=-1, keepdims=True)
        xn = xc * lax.rsqrt(var + LN_EPS)
        row = i * tm + lax.broadcasted_iota(jnp.int32, (tm, 1), 0)
        is_ctx = row < tc
        m = mod_ref[0]
        scale = jnp.where(is_ctx, m[0:1], m[2:3])
        shift = jnp.where(is_ctx, m[1:2], m[3:4])
        h_ref[...] = (xn * (1.0 + scale) + shift).astype(BF16)

    o_ref[0] = jnp.dot(h_ref[...], w_ref[...], preferred_element_type=F32).astype(o_ref.dtype)


def _inproj(xa, mod, w_pack, tc):
    b, tt, d = xa.shape
    tm = tt // 4
    tn = IN_TILE_N
    return pl.pallas_call(
        functools.partial(_inproj_kernel, tc=tc, tm=tm),
        grid=(b, tt // tm, N_PACK // tn),
        in_specs=[pl.BlockSpec((1, tm, d), lambda bi, i, j: (bi, i, 0)),
                  pl.BlockSpec((1, 8, d), lambda bi, i, j: (bi, 0, 0)),
                  pl.BlockSpec((d, tn), lambda bi, i, j: (0, j))],
        out_specs=pl.BlockSpec((1, tm, tn), lambda bi, i, j: (bi, i, j)),
        out_shape=jax.ShapeDtypeStruct((b, tt, N_PACK), BF16),
        scratch_shapes=[pltpu.VMEM((tm, d), BF16)],
        compiler_params=_params("arbitrary", "arbitrary", "arbitrary"),
        name="inproj",
    )(xa, mod, w_pack)


def _mla_kernel(cq_ref, ckv_ref, kr_ref, wq_ref, wk_ref, wv_ref, qn_ref, kvn_ref,
                ct_ref, st_ref, ctq_ref, stq_ref, o_ref, k_s, v_s, *, tc, tt, scale):
    i = pl.program_id(2)

    @pl.when(i == 0)
    def _():
        ckv = ckv_ref[0].astype(F32)
        r = lax.rsqrt(jnp.mean(ckv * ckv, axis=-1, keepdims=True) + RMS_EPS)
        ckvn = (ckv * r * kvn_ref[...]).astype(BF16)
        k_s[:, 0:MLA_NOPE] = jnp.dot(ckvn, wk_ref[0], preferred_element_type=F32).astype(BF16)
        v_s[...] = jnp.dot(ckvn, wv_ref[0], preferred_element_type=F32).astype(BF16)
        kr = kr_ref[0].astype(F32)
        k_s[:, MLA_NOPE:] = (kr * ct_ref[...] + pltpu.roll(kr * st_ref[...], 64, 1)).astype(BF16)

    cq = cq_ref[0].astype(F32)
    r = lax.rsqrt(jnp.mean(cq * cq, axis=-1, keepdims=True) + RMS_EPS)
    cqn = (cq * r * qn_ref[...]).astype(BF16)
    q = jnp.dot(cqn, wq_ref[0], preferred_element_type=F32)
    qh = q[:, MLA_NOPE:]
    qr = qh * ctq_ref[...] + pltpu.roll(qh * stq_ref[...], 64, 1)
    qf = (jnp.concatenate([q[:, :MLA_NOPE], qr], axis=1) * scale).astype(BF16)

    def attend(nk):
        s = lax.dot_general(qf, k_s[0:nk, :], (((1,), (1,)), ((), ())), preferred_element_type=F32)
        m = jnp.max(s, axis=1, keepdims=True)
        p = jnp.exp(s - m)
        l = jnp.sum(p, axis=1, keepdims=True)
        o = jnp.dot(p.astype(BF16), v_s[0:nk, :], preferred_element_type=F32)
        o_ref[0] = (o * (1.0 / l)).astype(o_ref.dtype)

    @pl.when(i == 0)
    def _():
        attend(tc)

    @pl.when(i > 0)
    def _():
        attend(tt)


def _mla(proj, wq, wk, wv, q_norm, kv_norm, ct, st, tc):
    b, tt, _ = proj.shape
    tq = Q_BLOCK
    assert tc == tq
    scale = (MLA_NOPE + MLA_ROPE) ** -0.5
    return pl.pallas_call(
        functools.partial(_mla_kernel, tc=tc, tt=tt, scale=scale),
        grid=(b, MLA_HEADS, tt // tq),
        in_specs=[
            pl.BlockSpec((1, tq, MLA_Q_LORA), lambda bi, h, i: (bi, i, OFF_CQ // MLA_Q_LORA)),
            pl.BlockSpec((1, tt, MLA_KV_LORA), lambda bi, h, i: (bi, 0, OFF_CKV // MLA_KV_LORA)),
            pl.BlockSpec((1, tt, 128), lambda bi, h, i: (bi, 0, OFF_KR // 128)),
            pl.BlockSpec((1, MLA_Q_LORA, 256), lambda bi, h, i: (h, 0, 0)),
            pl.BlockSpec((1, MLA_KV_LORA, MLA_NOPE), lambda bi, h, i: (h, 0, 0)),
            pl.BlockSpec((1, MLA_KV_LORA, MLA_V), lambda bi, h, i: (h, 0, 0)),
            pl.BlockSpec((1, MLA_Q_LORA), lambda bi, h, i: (0, 0)),
            pl.BlockSpec((1, MLA_KV_LORA), lambda bi, h, i: (0, 0)),
            pl.BlockSpec((tt, 128), lambda bi, h, i: (0, 0)),
            pl.BlockSpec((tt, 128), lambda bi, h, i: (0, 0)),
            pl.BlockSpec((tq, 128), lambda bi, h, i: (i, 0)),
            pl.BlockSpec((tq, 128), lambda bi, h, i: (i, 0)),
        ],
        out_specs=pl.BlockSpec((1, tq, MLA_V), lambda bi, h, i: (bi, i, h)),
        out_shape=jax.ShapeDtypeStruct((b, tt, BRANCH_W), BF16),
        scratch_shapes=[pltpu.VMEM((tt, 256), BF16), pltpu.VMEM((tt, MLA_V), BF16)],
        compiler_params=_params("arbitrary", "arbitrary", "arbitrary"),
        name="mla_attn",
    )(proj, proj, proj, wq, wk, wv, q_norm.reshape(1, -1), kv_norm.reshape(1, -1), ct, st, ct, st)


def _diff_kernel(q_ref, k_ref, v_ref, lq_ref, lk_ref, g_ref, ct_ref, st_ref, ctq_ref, stq_ref,
                 o_ref, k_s, *, tc, tt, lam_init):
    i = pl.program_id(2)
    lane = lax.broadcasted_iota(jnp.int32, (1, 128), 1)
    first_half = (lane & 32) == 0

    def rope(x, c, s):
        xs = jnp.where(first_half, pltpu.roll(x, 96, 1), pltpu.roll(x, 32, 1))
        return x * c + xs * s

    @pl.when(i == 0)
    def _():
        k_s[...] = rope(k_ref[0].astype(F32), ct_ref[...], st_ref[...]).astype(BF16)

    q = rope(q_ref[0].astype(F32), ctq_ref[...], stq_ref[...]) * (DIFF_HEAD_DIM ** -0.5)
    q1 = jnp.where(lane < 64, q, 0.0).astype(BF16)
    q2 = jnp.where(lane >= 64, q, 0.0).astype(BF16)
    lqk = lq_ref[...] * lk_ref[...]
    lam = (jnp.exp(jnp.sum(lqk[0:1], axis=1, keepdims=True))
           - jnp.exp(jnp.sum(lqk[1:2], axis=1, keepdims=True)) + lam_init)

    def attend(nk):
        kk = k_s[0:nk, :]
        nt = (((1,), (1,)), ((), ()))
        s1 = lax.dot_general(q1, kk, nt, preferred_element_type=F32)
        s2 = lax.dot_general(q2, kk, nt, preferred_element_type=F32)
        e1 = jnp.exp(s1 - jnp.max(s1, axis=1, keepdims=True))
        e2 = jnp.exp(s2 - jnp.max(s2, axis=1, keepdims=True))
        l1 = jnp.sum(e1, axis=1, keepdims=True)
        l2 = jnp.sum(e2, axis=1, keepdims=True)
        w = e1 - e2 * (lam * l1 / l2)
        o = jnp.dot(w.astype(BF16), v_ref[0, 0:nk, :], preferred_element_type=F32) * (1.0 / l1)
        y = o * lax.rsqrt(jnp.mean(o * o, axis=-1, keepdims=True) + RMS_EPS)
        o_ref[0] = (y * g_ref[...] * (1.0 - lam_init)).astype(o_ref.dtype)

    @pl.when(i == 0)
    def _():
        attend(tc)

    @pl.when(i > 0)
    def _():
        attend(tt)


def _diff(proj, lam_q, lam_k, norm_g, ct, st, lam_init, tc):
    b, tt, _ = proj.shape
    tq = Q_BLOCK
    assert tc == tq
    return pl.pallas_call(
        functools.partial(_diff_kernel, tc=tc, tt=tt, lam_init=lam_init),
        grid=(b, DIFF_HEADS, tt // tq),
        in_specs=[
            pl.BlockSpec((1, tq, 128), lambda bi, h, i: (bi, i, OFF_QD // 128 + h)),
            pl.BlockSpec((1, tt, 128), lambda bi, h, i: (bi, 0, OFF_KD // 128 + h)),
            pl.BlockSpec((1, tt, 128), lambda bi, h, i: (bi, 0, OFF_VD // 128 + h)),
            pl.BlockSpec((2, DIFF_HEAD_DIM), lambda bi, h, i: (0, 0)),
            pl.BlockSpec((2, DIFF_HEAD_DIM), lambda bi, h, i: (0, 0)),
            pl.BlockSpec((1, 128), lambda bi, h, i: (0, 0)),
            pl.BlockSpec((tt, 128), lambda bi, h, i: (0, 0)),
            pl.BlockSpec((tt, 128), lambda bi, h, i: (0, 0)),
            pl.BlockSpec((tq, 128), lambda bi, h, i: (i, 0)),
            pl.BlockSpec((tq, 128), lambda bi, h, i: (i, 0)),
        ],
        out_specs=pl.BlockSpec((1, tq, 128), lambda bi, h, i: (bi, i, h)),
        out_shape=jax.ShapeDtypeStruct((b, tt, BRANCH_W), BF16),
        scratch_shapes=[pltpu.VMEM((tt, 128), BF16)],
        compiler_params=_params("arbitrary", "arbitrary", "arbitrary"),
        name="diff_attn",
    )(proj, proj, proj, lam_q, lam_k, norm_g.reshape(1, -1), ct, st, ct, st)


def _s5_kernel(x_ref, t_ref, uh_ref, hy_ref, p_ref, o_ref, v_s, h_s, *, nb, nch, nctx):
    x = x_ref[0]
    yloc = jnp.dot(x, t_ref[0], preferred_element_type=F32)
    v_s[...] = jnp.dot(x, uh_ref[0], preferred_element_type=F32)
    pm = p_ref[0]
    zero = jnp.zeros((nb, 128), F32)

    def scan(order, col, p1, p2):
        h, hs = zero, zero
        for c in order:
            rows = slice(c * nb, (c + 1) * nb)
            h_s[rows, col:col + 128] = h
            v = v_s[rows, 2 * col:2 * col + 128]
            vs = v_s[rows, 2 * col + 128:2 * col + 256]
            h, hs = h * p1 + hs * p2 + v, hs * p1 - h * p2 + vs

    scan(range(nch), 0, pm[0:1], pm[1:2])
    scan(list(range(nctx - 1, -1, -1)) + list(range(nch - 1, nctx - 1, -1)), 128, pm[2:3], pm[3:4])
    y = yloc + jnp.dot(h_s[...].astype(BF16), hy_ref[0], preferred_element_type=F32)
    o_ref[0] = y.astype(o_ref.dtype)


def _s5_weights(lam_re, lam_im, log_dt, b_re, b_im, c_re, c_im, d):
    q = S5_CHUNK
    g, p, s = S5_GROUPS, S5_STATE, S5_GROUP
    lam = lam_re.astype(F32) + 1j * lam_im.astype(F32)
    dt = jnp.exp(log_dt.astype(F32))[..., None]
    ldt = lam * dt
    a = jnp.exp(ldt)
    bbar = ((a - 1.0) / lam)[..., None] * (b_re.astype(F32) + 1j * b_im.astype(F32))
    cc = c_re.astype(F32) + 1j * c_im.astype(F32)
    tau = jnp.arange(q + 1, dtype=F32)
    apow = jnp.exp(ldt[:, None] * tau[None, :, None, None])
    kern = jnp.real(jnp.einsum('dgop,dtgp,dgpi->dtgoi', cc, apow[:, :q], bbar))
    ii = np.arange(q)
    lag = ii[None, :] - ii[:, None]
    tf = jnp.where((lag >= 0)[:, :, None, None, None],
                   kern[0][np.clip(lag, 0, q - 1)], 0.0)
    tb = jnp.where((lag <= 0)[:, :, None, None, None],
                   kern[1][np.clip(-lag, 0, q - 1)], 0.0)
    tmat = jnp.transpose(tf + tb, (2, 0, 4, 1, 3))
    eye = jnp.eye(q, dtype=F32)[:, None, :, None] * jnp.eye(s, dtype=F32)[None, :, None, :]
    tmat = tmat + eye[None] * d.astype(F32).reshape(g, 1, s, 1, 1)
    tmat = tmat.reshape(g, q * s, q * s)
    uf = jnp.einsum('jgp,gpi->gjip', apow[0, :q][::-1], bbar[0])
    ub = jnp.einsum('jgp,gpi->gjip', apow[1, :q], bbar[1])

    def ri(z):
        return jnp.concatenate([jnp.real(z), jnp.imag(z), jnp.imag(z), jnp.real(z)], axis=-1)

    uh = jnp.concatenate([ri(uf), ri(ub)], axis=-1).reshape(g, q * s, 8 * p)
    hf = jnp.einsum('gop,igp->gpio', cc[0], apow[0, 1:])
    hb = jnp.einsum('gop,igp->gpio', cc[1], apow[1, 1:][::-1])

    def rows(z):
        return jnp.concatenate([jnp.real(z), -jnp.imag(z)], axis=1)

    hy = jnp.concatenate([rows(hf), rows(hb)], axis=1).reshape(g, 4 * p, q * s)
    aq = apow[:, q]
    pm = jnp.stack([jnp.concatenate([jnp.real(aq[0]), jnp.real(aq[0])], -1),
                    jnp.concatenate([-jnp.imag(aq[0]), jnp.imag(aq[0])], -1),
                    jnp.concatenate([jnp.real(aq[1]), jnp.real(aq[1])], -1),
                    jnp.concatenate([-jnp.imag(aq[1]), jnp.imag(aq[1])], -1)], axis=1)
    pm = jnp.concatenate([pm, jnp.zeros_like(pm)], axis=1)
    return tmat.astype(BF16), uh.astype(BF16), hy.astype(BF16), pm


def _s5(proj, weights, tc):
    b, tt, _ = proj.shape
    q, g, s = S5_CHUNK, S5_GROUPS, S5_GROUP
    nch = tt // q
    tmat, uh, hy, pm = weights
    u = proj[:, :, OFF_U:OFF_U + BRANCH_W]
    x = u.reshape(b, nch, q, g, s).transpose(3, 1, 0, 2, 4).reshape(g, nch * b, q * s)
    r = nch * b
    y = pl.pallas_call(
        functools.partial(_s5_kernel, nb=b, nch=nch, nctx=tc // q),
        grid=(g,),
        in_specs=[pl.BlockSpec((1, r, q * s), lambda gi: (gi, 0, 0)),
                  pl.BlockSpec((1, q * s, q * s), lambda gi: (gi, 0, 0)),
                  pl.BlockSpec((1, q * s, 512), lambda gi: (gi, 0, 0)),
                  pl.BlockSpec((1, 256, q * s), lambda gi: (gi, 0, 0)),
                  pl.BlockSpec((1, 8, 128), lambda gi: (gi, 0, 0))],
        out_specs=pl.BlockSpec((1, r, q * s), lambda gi: (gi, 0, 0)),
        out_shape=jax.ShapeDtypeStruct((g, r, q * s), BF16),
        scratch_shapes=[pltpu.VMEM((r, 512), F32), pltpu.VMEM((r, 256), F32)],
        compiler_params=_params("arbitrary"),
        name="s5",
    )(x, tmat, uh, hy, pm)
    return y.reshape(g, nch, b, q, s).transpose(2, 1, 3, 0, 4).reshape(b, tt, BRANCH_W)


def _ssd_kernel(*refs, backward, first, nctx_chunks, nchunks):
    if first:
        (xbc_ref, prev_ref, next_ref, z_ref, dt_ref, cw_ref, cb_ref, dtb_ref, alog_ref, dsk_ref,
         sel_ref, o_ref, st_s) = refs
        yin_ref = ng_ref = None
    else:
        (xbc_ref, prev_ref, next_ref, z_ref, dt_ref, cw_ref, cb_ref, dtb_ref, alog_ref, dsk_ref,
         sel_ref, yin_ref, ng_ref, o_ref, st_s) = refs
    qn = SSD_CHUNK
    step = pl.program_id(1)
    if backward:
        c = jnp.where(step < nctx_chunks, nctx_chunks - 1 - step, nchunks - 1 - (step - nctx_chunks))
    else:
        c = step

    @pl.when(step == 0)
    def _():
        st_s[...] = jnp.zeros_like(st_s)

    x = xbc_ref[0].astype(F32)
    has_prev = jnp.logical_and(c != 0, c != nctx_chunks)
    has_next = jnp.logical_and(c != nctx_chunks - 1, c != nchunks - 1)
    prow = jnp.where(has_prev, prev_ref[0, 7:8, :].astype(F32), 0.0)
    nrow = jnp.where(has_next, next_ref[0, 0:1, :].astype(F32), 0.0)
    rid = lax.broadcasted_iota(jnp.int32, (qn, 1), 0)
    xm = jnp.where(rid == 0, prow, pltpu.roll(x, 1, 0))
    xp = jnp.where(rid == qn - 1, nrow, pltpu.roll(x, qn - 1, 0))
    cw = cw_ref[...]
    conv = xm * cw[0:1] + x * cw[1:2] + xp * cw[2:3] + cb_ref[...]
    act = _silu(conv)
    xs = act[:, :BRANCH_W]
    gn = SSD_GROUPS * SSD_STATE
    bm = act[:, BRANCH_W:BRANCH_W + gn].astype(BF16)
    cm = act[:, BRANCH_W + gn:].astype(BF16)

    dt = _softplus(dt_ref[0].astype(F32) + dtb_ref[...])
    a = -jnp.exp(alog_ref[...])
    da = dt * a
    ri = lax.broadcasted_iota(jnp.int32, (qn, qn), 0)
    ci = lax.broadcasted_iota(jnp.int32, (qn, qn), 1)
    causal = (ci >= ri) if backward else (ci <= ri)
    ones_tri = jnp.where(causal, 1.0, 0.0).astype(BF16)
    d1 = da.astype(BF16)
    r1 = da - d1.astype(F32)
    d2 = r1.astype(BF16)
    d3 = (r1 - d2.astype(F32)).astype(BF16)
    cum = (jnp.dot(ones_tri, d1, preferred_element_type=F32)
           + jnp.dot(ones_tri, d2, preferred_element_type=F32)
           + jnp.dot(ones_tri, d3, preferred_element_type=F32))
    edge = cum[0:1] if backward else cum[qn - 1:qn]
    cum_t = cum.T
    dt_t = dt.T
    w_edge = dt * jnp.exp(edge - cum)
    e_in = jnp.exp(cum)

    def expand(v):
        v1 = v.astype(BF16)
        v2 = (v - v1.astype(F32)).astype(BF16)
        return (jnp.dot(v1, sel_ref[...], preferred_element_type=F32)
                + jnp.dot(v2, sel_ref[...], preferred_element_type=F32))

    xw = (xs * expand(w_edge)).astype(BF16)
    e_in_x = expand(e_in)
    xs_b = xs.astype(BF16)
    rpg = SSD_HEADS // SSD_GROUPS
    gw = rpg * SSD_HEAD_DIM
    ys = []
    for g in range(SSD_GROUPS):
        bg = bm[:, g * SSD_STATE:(g + 1) * SSD_STATE]
        cg = cm[:, g * SSD_STATE:(g + 1) * SSD_STATE]
        cb = lax.dot_general(cg, bg, (((1,), (1,)), ((), ())), preferred_element_type=F32)
        st_g = st_s[g * gw:(g + 1) * gw, :]
        y_off = lax.dot_general(cg, st_g.astype(BF16), (((1,), (1,)), ((), ())),
                                preferred_element_type=F32)
        yg = y_off * e_in_x[:, g * gw:(g + 1) * gw]
        parts = []
        for r in range(rpg):
            h = g * rpg + r
            dec = jnp.exp(jnp.where(causal, cum[:, h:h + 1] - cum_t[h:h + 1, :], -jnp.inf))
            wmat = (cb * dec * dt_t[h:h + 1, :]).astype(BF16)
            parts.append(jnp.dot(wmat, xs_b[:, h * SSD_HEAD_DIM:(h + 1) * SSD_HEAD_DIM],
                                 preferred_element_type=F32))
        ys.append(yg + jnp.concatenate(parts, axis=1))
        new = lax.dot_general(xw[:, g * gw:(g + 1) * gw], bg, (((0,), (0,)), ((), ())),
                              preferred_element_type=F32)
        for r in range(rpg):
            h = g * rpg + r
            rows = slice(g * gw + r * SSD_HEAD_DIM, g * gw + (r + 1) * SSD_HEAD_DIM)
            cd = jnp.exp(edge[:, h:h + 1])
            st_s[rows, :] = st_s[rows, :] * cd + new[r * SSD_HEAD_DIM:(r + 1) * SSD_HEAD_DIM, :]
    y = jnp.concatenate(ys, axis=1)
    if first:
        o_ref[0] = y + xs * dsk_ref[...]
    else:
        y = y + yin_ref[0]
        zz = z_ref[0].astype(F32)
        y = y * _silu(zz)
        y = y * lax.rsqrt(jnp.mean(y * y, axis=-1, keepdims=True) + RMS_EPS) * ng_ref[...]
        o_ref[0] = y.astype(o_ref.dtype)


def _ssd_pass(proj, conv_w, conv_b, dt_bias, a_log, d_skip, sel, yin, norm_g, tc, backward):
    b, tt, _ = proj.shape
    qn = SSD_CHUNK
    nchunks = tt // qn
    nctx = tc // qn
    first = yin is None
    last_blk8 = tt // 8 - 1

    def chunk_of(s):
        if backward:
            return jnp.where(s < nctx, nctx - 1 - s, nchunks - 1 - (s - nctx))
        return s

    xbc_blk = OFF_XBC // SSD_CONV_CH
    in_specs = [
        pl.BlockSpec((1, qn, SSD_CONV_CH), lambda bi, s: (bi, chunk_of(s), xbc_blk)),
        pl.BlockSpec((1, 8, SSD_CONV_CH),
                     lambda bi, s: (bi, jnp.maximum(chunk_of(s) * (qn // 8) - 1, 0), xbc_blk)),
        pl.BlockSpec((1, 8, SSD_CONV_CH),
                     lambda bi, s: (bi, jnp.minimum((chunk_of(s) + 1) * (qn // 8), last_blk8), xbc_blk)),
        pl.BlockSpec((1, qn, BRANCH_W), lambda bi, s: (bi, chunk_of(s), OFF_Z // BRANCH_W)),
        pl.BlockSpec((1, qn, 128), lambda bi, s: (bi, chunk_of(s), OFF_DT // 128)),
        pl.BlockSpec((8, SSD_CONV_CH), lambda bi, s: (0, 0)),
        pl.BlockSpec((1, SSD_CONV_CH), lambda bi, s: (0, 0)),
        pl.BlockSpec((1, 128), lambda bi, s: (0, 0)),
        pl.BlockSpec((1, 128), lambda bi, s: (0, 0)),
        pl.BlockSpec((1, BRANCH_W), lambda bi, s: (0, 0)),
        pl.BlockSpec((128, BRANCH_W), lambda bi, s: (0, 0)),
    ]
    args = [proj, proj, proj, proj, proj, conv_w, conv_b, dt_bias, a_log, d_skip, sel]
    if not first:
        in_specs += [pl.BlockSpec((1, qn, BRANCH_W), lambda bi, s: (bi, chunk_of(s), 0)),
                     pl.BlockSpec((1, BRANCH_W), lambda bi, s: (0, 0))]
        args += [yin, norm_g]
    return pl.pallas_call(
        functools.partial(_ssd_kernel, backward=backward, first=first, nctx_chunks=nctx, nchunks=nchunks),
        grid=(b, nchunks),
        in_specs=in_specs,
        out_specs=pl.BlockSpec((1, qn, BRANCH_W), lambda bi, s: (bi, chunk_of(s), 0)),
        out_shape=jax.ShapeDtypeStruct((b, tt, BRANCH_W), F32 if first else BF16),
        scratch_shapes=[pltpu.VMEM((SSD_HEADS * SSD_HEAD_DIM, SSD_STATE), F32)],
        compiler_params=_params("arbitrary", "arbitrary"),
        name="ssd_bwd" if backward else "ssd_fwd",
    )(*args)


def _ssd(proj, conv_w, conv_b, dt_bias, a_log, d_skip, norm_g, tc):
    def lanes(v):
        return jnp.pad(v.astype(F32), (0, 128 - SSD_HEADS)).reshape(1, 128)

    cw = jnp.pad(conv_w.astype(F32), ((0, 5), (0, 0)))
    cb = conv_b.astype(F32).reshape(1, -1)
    dsk = jnp.repeat(d_skip.astype(F32), SSD_HEAD_DIM).reshape(1, -1)
    sel = (jnp.arange(128)[:, None] == (jnp.arange(BRANCH_W)[None, :] // SSD_HEAD_DIM)).astype(BF16)
    y1 = _ssd_pass(proj, cw, cb, lanes(dt_bias[0]), lanes(a_log[0]), dsk, sel, None, None, tc, False)
    return _ssd_pass(proj, cw, cb, lanes(dt_bias[1]), lanes(a_log[1]), jnp.zeros_like(dsk), sel,
                     y1, norm_g.astype(F32).reshape(1, -1), tc, True)


def _merge_kernel(ya_ref, ga_ref, yb_ref, gb_ref, yc_ref, yd_ref, gd_ref, gm_ref, bm_ref,
                  wb_ref, wg_ref, bg_ref, o_ref):
    def f(ref):
        return ref[0].astype(F32)

    g = _gelu_tanh(f(yb_ref))
    glu = g * _sigmoid(jnp.dot(g.astype(BF16), wg_ref[...], preferred_element_type=F32) + bg_ref[...])
    branches = (f(ya_ref) * _silu(f(ga_ref)), glu * _silu(f(gb_ref)), f(yc_ref), f(yd_ref) * _silu(f(gd_ref)))
    acc = None
    for n, br in enumerate(branches):
        gate = _sigmoid(gm_ref[0, :, n * D_MODEL:(n + 1) * D_MODEL].astype(F32) + bm_ref[n:n + 1, :])
        term = gate * jnp.dot(br.astype(BF16), wb_ref[n], preferred_element_type=F32)
        acc = term if acc is None else acc + term
    o_ref[0] = acc.astype(o_ref.dtype)


def _merge(proj, ya, yb, yc, yd, b_merge, w_branch, w_glu, b_glu):
    b, tt, _ = proj.shape
    tm = Q_BLOCK
    bw = BRANCH_W

    def pspec(off):
        return pl.BlockSpec((1, tm, bw), lambda bi, i: (bi, i, off // bw))

    yspec = pl.BlockSpec((1, tm, bw), lambda bi, i: (bi, i, 0))
    once = dict(pipeline_mode=pl.Buffered(1))
    return pl.pallas_call(
        _merge_kernel,
        grid=(b, tt // tm),
        in_specs=[yspec, pspec(OFF_GA), yspec, pspec(OFF_GB), yspec, yspec, pspec(OFF_GD),
                  pl.BlockSpec((1, tm, N_BRANCH * D_MODEL), lambda bi, i: (bi, i, 0)),
                  pl.BlockSpec((N_BRANCH, D_MODEL), lambda bi, i: (0, 0)),
                  pl.BlockSpec((N_BRANCH, bw, D_MODEL), lambda bi, i: (0, 0, 0), **once),
                  pl.BlockSpec((bw, bw), lambda bi, i: (0, 0), **once),
                  pl.BlockSpec((1, bw), lambda bi, i: (0, 0))],
        out_specs=pl.BlockSpec((1, tm, D_MODEL), lambda bi, i: (bi, i, 0)),
        out_shape=jax.ShapeDtypeStruct((b, tt, D_MODEL), BF16),
        compiler_params=_params("arbitrary", "arbitrary"),
        name="merge",
    )(ya, proj, yb, proj, yc, yd, proj, proj, b_merge, w_branch, w_glu, b_glu.reshape(1, -1))


def _out_kernel(m_ref, x_ref, mod_ref, w_ref, lg_ref, lb_ref, o_ref, *, tc, tm, row0):
    i = pl.program_id(1)
    out = jnp.dot(m_ref[0], w_ref[...], preferred_element_type=F32)
    row = row0 + i * tm + lax.broadcasted_iota(jnp.int32, (tm, 1), 0)
    m = mod_ref[0]
    gate = jnp.where(row < tc, m[4:5], m[5:6])
    v = DEEPNORM_ALPHA * x_ref[0] + gate * out
    mu = jnp.mean(v, axis=-1, keepdims=True)
    vc = v - mu
    var = jnp.mean(vc * vc, axis=-1, keepdims=True)
    o_ref[0] = vc * lax.rsqrt(var + LN_EPS) * lg_ref[...] + lb_ref[...]


def _out(merged, xa, mod, w_out, ln_g, ln_b, tc, latent_only):
    b, tt, d = xa.shape
    tm = Q_BLOCK
    skip = tc // tm if latent_only else 0
    nblk = tt // tm - skip
    spec = pl.BlockSpec((1, tm, d), lambda bi, i: (bi, i + skip, 0))
    return pl.pallas_call(
        functools.partial(_out_kernel, tc=tc, tm=tm, row0=skip * tm),
        grid=(b, nblk),
        in_specs=[spec, spec,
                  pl.BlockSpec((1, 8, d), lambda bi, i: (bi, 0, 0)),
                  pl.BlockSpec((d, d), lambda bi, i: (0, 0)),
                  pl.BlockSpec((1, d), lambda bi, i: (0, 0)),
                  pl.BlockSpec((1, d), lambda bi, i: (0, 0))],
        out_specs=pl.BlockSpec((1, tm, d), lambda bi, i: (bi, i, 0)),
        out_shape=jax.ShapeDtypeStruct((b, nblk * tm, d), F32),
        compiler_params=_params("arbitrary", "arbitrary"),
        name="out_proj",
    )(merged, xa, mod, w_out, ln_g.reshape(1, -1), ln_b.reshape(1, -1))


def _half_split(w, heads):
    k = w.shape[0]
    return w.reshape(k, heads, ROPE_DIM // 2, 2).transpose(0, 1, 3, 2).reshape(k, heads * ROPE_DIM)


def _pack_w_in(w):
    sp = np.cumsum([0, MLA_Q_LORA, MLA_KV_LORA, MLA_ROPE, BRANCH_W, BRANCH_W, BRANCH_W, BRANCH_W,
                    SSD_CONV_CH, SSD_HEADS, BRANCH_W, BRANCH_W, BRANCH_W, BRANCH_W, N_BRANCH * D_MODEL])
    seg = [w[:, sp[k]:sp[k + 1]] for k in range(14)]
    cq, ckv, kr, ga, u, gb, z, xbc, dt, qd, kd, vd, gd, gm = seg
    kr_hs = _half_split(kr, 1)
    kr_sw = jnp.concatenate([kr_hs[:, 32:], kr_hs[:, :32]], axis=1)
    pad = jnp.zeros((w.shape[0], 128 - SSD_HEADS), w.dtype)
    packed = jnp.concatenate([gm, xbc, ga, u, gb, z, _half_split(qd, 2 * DIFF_HEADS),
                              _half_split(kd, 2 * DIFF_HEADS), vd, gd, cq, ckv, kr_hs, kr_sw, dt, pad],
                             axis=1)
    return packed.astype(BF16)


def _pack_mla(w_uq, w_ukv):
    k = w_uq.shape[0]
    wq = w_uq.reshape(k, MLA_HEADS, MLA_NOPE + MLA_ROPE)
    rope = wq[:, :, MLA_NOPE:].reshape(k, MLA_HEADS, ROPE_DIM // 2, 2)
    ev, od = rope[..., 0], rope[..., 1]
    wq = jnp.concatenate([wq[:, :, :MLA_NOPE], ev, od, od, ev], axis=-1)
    wkv = w_ukv.reshape(w_ukv.shape[0], MLA_HEADS, MLA_NOPE + MLA_V)
    return (wq.transpose(1, 0, 2).astype(BF16),
            wkv[:, :, :MLA_NOPE].transpose(1, 0, 2).astype(BF16),
            wkv[:, :, MLA_NOPE:].transpose(1, 0, 2).astype(BF16))


def _rope_tables(tl, tc):
    rows = tl // GRID_W
    row_id = jnp.repeat(jnp.arange(rows, dtype=F32), GRID_W)
    col_id = jnp.tile(jnp.arange(GRID_W, dtype=F32), rows)
    quarter = ROPE_DIM // 4
    inv_freq = ROPE_BASE ** (-jnp.arange(quarter, dtype=F32) / quarter)
    ang = jnp.concatenate([row_id[:, None] * inv_freq, col_id[:, None] * inv_freq], axis=-1)
    cos = jnp.concatenate([jnp.ones((tc, ROPE_DIM // 2), F32), jnp.cos(ang)], axis=0)
    sin = jnp.concatenate([jnp.zeros((tc, ROPE_DIM // 2), F32), jnp.sin(ang)], axis=0)
    zero = jnp.zeros_like(cos)
    mla = (jnp.concatenate([cos, cos, zero, zero], axis=1), jnp.concatenate([zero, zero, -sin, sin], axis=1))
    diff = (jnp.concatenate([cos, cos, cos, cos], axis=1), jnp.concatenate([-sin, sin, -sin, sin], axis=1))
    return mla, diff


def _layer(xa, cs, p, tables, layer_idx, tc, last):
    b = xa.shape[0]
    mla_tab, diff_tab = tables
    mod = _ada(cs, p['w_ada'], p['b_ada'])
    d = D_MODEL
    shift, scale, gate = mod[:, :d], mod[:, d:2 * d], mod[:, 2 * d:]
    zeros = jnp.zeros((b, d), F32)
    modb = jnp.stack([jnp.broadcast_to(scale[0], (b, d)), jnp.broadcast_to(shift[0], (b, d)),
                      scale[1:1 + b], shift[1:1 + b],
                      jnp.broadcast_to(gate[0], (b, d)), gate[1:1 + b], zeros, zeros], axis=1)
    proj = _inproj(xa, modb, _pack_w_in(p['w_in']), tc)
    wq, wk, wv = _pack_mla(p['mla_w_uq'], p['mla_w_ukv'])
    ya = _mla(proj, wq, wk, wv, p['mla_q_norm'], p['mla_kv_norm'], mla_tab[0], mla_tab[1], tc)
    lam_init = 0.8 - 0.6 * math.exp(-0.3 * layer_idx)
    yd = _diff(proj, p['diff_lambda_q'], p['diff_lambda_k'], p['diff_norm'], diff_tab[0], diff_tab[1],
               lam_init, tc)
    yb = _s5(proj, _s5_weights(p['s5_lambda_re'], p['s5_lambda_im'], p['s5_log_dt'], p['s5_b_re'],
                               p['s5_b_im'], p['s5_c_re'], p['s5_c_im'], p['s5_d']), tc)
    yc = _ssd(proj, p['ssd_conv_w'], p['ssd_conv_b'], p['ssd_dt_bias'], p['ssd_a_log'], p['ssd_d'],
              p['ssd_norm'], tc)
    merged = _merge(proj, ya, yb, yc, yd, p['b_merge'], p['w_branch'].astype(BF16),
                    p['s5_w_glu'].astype(BF16), p['s5_b_glu'])
    return _out(merged, xa, modb, p['w_out'].astype(BF16), p['ln_g'], p['ln_b'], tc, last)


def kernel(x, c, ctx, c_ctx, w_ada, b_ada, w_in, mla_q_norm, mla_w_uq, mla_kv_norm, mla_w_ukv,
           s5_lambda_re, s5_lambda_im, s5_log_dt, s5_b_re, s5_b_im, s5_c_re, s5_c_im, s5_d,
           s5_w_glu, s5_b_glu, ssd_conv_w, ssd_conv_b, ssd_dt_bias, ssd_a_log, ssd_d, ssd_norm,
           diff_lambda_q, diff_lambda_k, diff_norm, b_merge, w_branch, w_out, ln_g, ln_b):
    b, tl, d = x.shape
    tc = ctx.shape[1]
    stacked = dict(
        w_ada=w_ada, b_ada=b_ada, w_in=w_in, mla_q_norm=mla_q_norm, mla_w_uq=mla_w_uq,
        mla_kv_norm=mla_kv_norm, mla_w_ukv=mla_w_ukv, s5_lambda_re=s5_lambda_re, s5_lambda_im=s5_lambda_im,
        s5_log_dt=s5_log_dt, s5_b_re=s5_b_re, s5_b_im=s5_b_im, s5_c_re=s5_c_re, s5_c_im=s5_c_im, s5_d=s5_d,
        s5_w_glu=s5_w_glu, s5_b_glu=s5_b_glu, ssd_conv_w=ssd_conv_w, ssd_conv_b=ssd_conv_b,
        ssd_dt_bias=ssd_dt_bias, ssd_a_log=ssd_a_log, ssd_d=ssd_d, ssd_norm=ssd_norm,
        diff_lambda_q=diff_lambda_q, diff_lambda_k=diff_lambda_k, diff_norm=diff_norm, b_merge=b_merge,
        w_branch=w_branch, w_out=w_out, ln_g=ln_g, ln_b=ln_b)
    tables = _rope_tables(tl, tc)
    cs = jnp.concatenate([c_ctx[None], c, jnp.zeros((8 - 1 - b, d), c.dtype)], axis=0)
    xa = jnp.concatenate([ctx, x], axis=1)
    depth = w_in.shape[0]
    for i in range(depth):
        p = {k: v[i] for k, v in stacked.items()}
        xa = _layer(xa, cs, p, tables, i, tc, i == depth - 1)
    return xa
```

```python
import functools
import math

import numpy as np
import jax
import jax.numpy as jnp
from jax import lax
from jax.experimental import pallas as pl
from jax.experimental.pallas import tpu as pltpu

F32 = jnp.float32
BF16 = jnp.bfloat16

D_MODEL = 2048
DEPTH = 2
GRID_W = 64
N_BRANCH = 4
BRANCH_W = 1024
ROPE_DIM = 64
ROPE_BASE = 10000.0

MLA_HEADS = 8
MLA_NOPE = 128
MLA_ROPE = ROPE_DIM
MLA_V = 128
MLA_Q_LORA = 512
MLA_KV_LORA = 256

S5_GROUP = 16
S5_GROUPS = BRANCH_W // S5_GROUP
S5_STATE = 64
S5_CHUNK = 16

SSD_HEAD_DIM = 64
SSD_HEADS = BRANCH_W // SSD_HEAD_DIM
SSD_GROUPS = 4
SSD_STATE = 128
SSD_CHUNK = 128
SSD_CONV_CH = BRANCH_W + 2 * SSD_GROUPS * SSD_STATE

DIFF_HEAD_DIM = ROPE_DIM
DIFF_HEADS = BRANCH_W // (2 * DIFF_HEAD_DIM)

LN_EPS = 1e-5
RMS_EPS = 1e-6
DEEPNORM_ALPHA = (2 * DEPTH) ** 0.25

V7X_VMEM_LIMIT_BYTES = 56 * 1024 * 1024
Q_BLOCK = 256

OFF_GM = 0
OFF_XBC = 8192
OFF_GA = 10240
OFF_U = 11264
OFF_GB = 12288
OFF_Z = 13312
OFF_QD = 14336
OFF_KD = 15360
OFF_VD = 16384
OFF_GD = 17408
OFF_CQ = 18432
OFF_CKV = 18944
OFF_KR = 19200
OFF_DT = 19328
N_PACK = 19456
IN_TILE_N = 1024


def _sigmoid(x):
    return 1.0 / (1.0 + jnp.exp(-x))


def _silu(x):
    return x * _sigmoid(x)


def _gelu_tanh(x):
    return 0.5 * x * (1.0 + jnp.tanh(math.sqrt(2.0 / math.pi) * (x + 0.044715 * (x * x * x))))


def _softplus(x):
    return jnp.maximum(x, 0.0) + jnp.log(1.0 + jnp.exp(-jnp.abs(x)))


def _params(*sem):
    return pltpu.CompilerParams(dimension_semantics=sem, vmem_limit_bytes=V7X_VMEM_LIMIT_BYTES)


def _ada_kernel(c_ref, w_ref, b_ref, o_ref):
    s = _silu(c_ref[...])
    o_ref[...] = jnp.dot(s.astype(BF16), w_ref[...].astype(BF16),
                         preferred_element_type=F32) + b_ref[...]


def _ada(cs, w_ada, b_ada):
    n = w_ada.shape[1]
    tn = 512
    return pl.pallas_call(
        _ada_kernel,
        grid=(n // tn,),
        in_specs=[pl.BlockSpec((8, D_MODEL), lambda j: (0, 0)),
                  pl.BlockSpec((D_MODEL, tn), lambda j: (0, j)),
                  pl.BlockSpec((1, tn), lambda j: (0, j))],
        out_specs=pl.BlockSpec((8, tn), lambda j: (0, j)),
        out_shape=jax.ShapeDtypeStruct((8, n), F32),
        compiler_params=_params("arbitrary"),
        name="ada",
    )(cs, w_ada, b_ada.reshape(1, n))


def _inproj_kernel(x_ref, mod_ref, w_ref, o_ref, h_ref, *, tc, tm):
    i = pl.program_id(1)

    @pl.when(pl.program_id(2) == 0)
    def _():
        x = x_ref[0]
        mu = jnp.mean(x, axis=-1, keepdims=True)
        xc = x - mu
        var = jnp.mean(xc * xc, axis=-1, keepdims=True)
        xn = xc * lax.rsqrt(var + LN_EPS)
        row = i * tm + lax.broadcasted_iota(jnp.int32, (tm, 1), 0)
        is_ctx = row < tc
        m = mod_ref[0]
        scale = jnp.where(is_ctx, m[0:1], m[2:3])
        shift = jnp.where(is_ctx, m[1:2], m[3:4])
        h_ref[...] = (xn * (1.0 + scale) + shift).astype(BF16)

    o_ref[0] = jnp.dot(h_ref[...], w_ref[...], preferred_element_type=F32).astype(o_ref.dtype)


def _inproj(xa, mod, w_pack, tc):
    b, tt, d = xa.shape
    tm = tt // 4
    tn = IN_TILE_N
    return pl.pallas_call(
        functools.partial(_inproj_kernel, tc=tc, tm=tm),
        grid=(b, tt // tm, N_PACK // tn),
        in_specs=[pl.BlockSpec((1, tm, d), lambda bi, i, j: (bi, i, 0)),
                  pl.BlockSpec((1, 8, d), lambda bi, i, j: (bi, 0, 0)),
                  pl.BlockSpec((d, tn), lambda bi, i, j: (0, j))],
        out_specs=pl.BlockSpec((1, tm, tn), lambda bi, i, j: (bi, i, j)),
        out_shape=jax.ShapeDtypeStruct((b, tt, N_PACK), BF16),
        scratch_shapes=[pltpu.VMEM((tm, d), BF16)],
        compiler_params=_params("arbitrary", "arbitrary", "arbitrary"),
        name="inproj",
    )(xa, mod, w_pack)


def _mla_kernel(cq_ref, ckv_ref, kr_ref, wq_ref, wk_ref, wv_ref, qn_ref, kvn_ref,
                ct_ref, st_ref, ctq_ref, stq_ref, o_ref, k_s, v_s, *, tc, tt, scale):
    i = pl.program_id(2)

    @pl.when(i == 0)
    def _():
        ckv = ckv_ref[0].astype(F32)
        r = lax.rsqrt(jnp.mean(ckv * ckv, axis=-1, keepdims=True) + RMS_EPS)
        ckvn = (ckv * r * kvn_ref[...]).astype(BF16)
        k_s[:, 0:MLA_NOPE] = jnp.dot(ckvn, wk_ref[0], preferred_element_type=F32).astype(BF16)
        v_s[...] = lax.dot_general(wv_ref[0], ckvn, (((1,), (1,)), ((), ())),
                                   preferred_element_type=F32).astype(BF16)
        kr = kr_ref[0].astype(F32)
        k_s[:, MLA_NOPE:] = (kr * ct_ref[...] + pltpu.roll(kr * st_ref[...], 64, 1)).astype(BF16)

    cq = cq_ref[0].astype(F32)
    r = lax.rsqrt(jnp.mean(cq * cq, axis=-1, keepdims=True) + RMS_EPS)
    cqn = (cq * r * qn_ref[...]).astype(BF16)
    q = jnp.dot(cqn, wq_ref[0], preferred_element_type=F32)
    qh = q[:, MLA_NOPE:]
    qr = qh * ctq_ref[...] + pltpu.roll(qh * stq_ref[...], 64, 1)
    qf = (jnp.concatenate([q[:, :MLA_NOPE], qr], axis=1) * scale).astype(BF16)

    def attend(nk):
        s = lax.dot_general(k_s[0:nk, :], qf, (((1,), (1,)), ((), ())), preferred_element_type=F32)
        m = jnp.max(s, axis=0, keepdims=True)
        p = jnp.exp(s - m)
        l = jnp.sum(p, axis=0, keepdims=True)
        o = jnp.dot(v_s[:, 0:nk], p.astype(BF16), preferred_element_type=F32)
        o_ref[0] = (o * (1.0 / l)).T.astype(o_ref.dtype)

    @pl.when(i == 0)
    def _():
        attend(tc)

    @pl.when(i > 0)
    def _():
        attend(tt)


def _mla(proj, wq, wk, wv, q_norm, kv_norm, ct, st, tc):
    b, tt, _ = proj.shape
    tq = Q_BLOCK
    assert tc == tq
    scale = (MLA_NOPE + MLA_ROPE) ** -0.5
    return pl.pallas_call(
        functools.partial(_mla_kernel, tc=tc, tt=tt, scale=scale),
        grid=(b, MLA_HEADS, tt // tq),
        in_specs=[
            pl.BlockSpec((1, tq, MLA_Q_LORA), lambda bi, h, i: (bi, i, OFF_CQ // MLA_Q_LORA)),
            pl.BlockSpec((1, tt, MLA_KV_LORA), lambda bi, h, i: (bi, 0, OFF_CKV // MLA_KV_LORA)),
            pl.BlockSpec((1, tt, 128), lambda bi, h, i: (bi, 0, OFF_KR // 128)),
            pl.BlockSpec((1, MLA_Q_LORA, 256), lambda bi, h, i: (h, 0, 0)),
            pl.BlockSpec((1, MLA_KV_LORA, MLA_NOPE), lambda bi, h, i: (h, 0, 0)),
            pl.BlockSpec((1, MLA_V, MLA_KV_LORA), lambda bi, h, i: (h, 0, 0)),
            pl.BlockSpec((1, MLA_Q_LORA), lambda bi, h, i: (0, 0)),
            pl.BlockSpec((1, MLA_KV_LORA), lambda bi, h, i: (0, 0)),
            pl.BlockSpec((tt, 128), lambda bi, h, i: (0, 0)),
            pl.BlockSpec((tt, 128), lambda bi, h, i: (0, 0)),
            pl.BlockSpec((tq, 128), lambda bi, h, i: (i, 0)),
            pl.BlockSpec((tq, 128), lambda bi, h, i: (i, 0)),
        ],
        out_specs=pl.BlockSpec((1, tq, MLA_V), lambda bi, h, i: (bi, i, h)),
        out_shape=jax.ShapeDtypeStruct((b, tt, BRANCH_W), BF16),
        scratch_shapes=[pltpu.VMEM((tt, 256), BF16), pltpu.VMEM((MLA_V, tt), BF16)],
        compiler_params=_params("arbitrary", "arbitrary", "arbitrary"),
        name="mla_attn",
    )(proj, proj, proj, wq, wk, wv, q_norm.reshape(1, -1), kv_norm.reshape(1, -1), ct, st, ct, st)


def _diff_kernel(q_ref, k_ref, v_ref, lq_ref, lk_ref, g_ref, ct_ref, st_ref, ctq_ref, stq_ref,
                 o_ref, k_s, v_s, *, tc, tt, lam_init):
    i = pl.program_id(2)
    lane = lax.broadcasted_iota(jnp.int32, (1, 128), 1)
    first_half = (lane & 32) == 0

    def rope(x, c, s):
        xs = jnp.where(first_half, pltpu.roll(x, 96, 1), pltpu.roll(x, 32, 1))
        return x * c + xs * s

    @pl.when(i == 0)
    def _():
        k_s[...] = rope(k_ref[0].astype(F32), ct_ref[...], st_ref[...]).astype(BF16)
        v_s[...] = v_ref[0].astype(F32).T.astype(BF16)

    q = rope(q_ref[0].astype(F32), ctq_ref[...], stq_ref[...]) * (DIFF_HEAD_DIM ** -0.5)
    q1 = jnp.where(lane < 64, q, 0.0).astype(BF16)
    q2 = jnp.where(lane >= 64, q, 0.0).astype(BF16)
    lqk = lq_ref[...] * lk_ref[...]
    lam = (jnp.exp(jnp.sum(lqk[0:1], axis=1, keepdims=True))
           - jnp.exp(jnp.sum(lqk[1:2], axis=1, keepdims=True)) + lam_init)

    def attend(nk):
        kk = k_s[0:nk, :]
        nt = (((1,), (1,)), ((), ()))
        s1 = lax.dot_general(kk, q1, nt, preferred_element_type=F32)
        s2 = lax.dot_general(kk, q2, nt, preferred_element_type=F32)
        e1 = jnp.exp(s1 - jnp.max(s1, axis=0, keepdims=True))
        e2 = jnp.exp(s2 - jnp.max(s2, axis=0, keepdims=True))
        l1 = jnp.sum(e1, axis=0, keepdims=True)
        l2 = jnp.sum(e2, axis=0, keepdims=True)
        w = e1 - e2 * (lam * l1 / l2)
        ot = jnp.dot(v_s[:, 0:nk], w.astype(BF16), preferred_element_type=F32) * (1.0 / l1)
        o = ot.T
        y = o * lax.rsqrt(jnp.mean(o * o, axis=-1, keepdims=True) + RMS_EPS)
        o_ref[0] = (y * g_ref[...] * (1.0 - lam_init)).astype(o_ref.dtype)

    @pl.when(i == 0)
    def _():
        attend(tc)

    @pl.when(i > 0)
    def _():
        attend(tt)


def _diff(proj, lam_q, lam_k, norm_g, ct, st, lam_init, tc):
    b, tt, _ = proj.shape
    tq = Q_BLOCK
    assert tc == tq
    return pl.pallas_call(
        functools.partial(_diff_kernel, tc=tc, tt=tt, lam_init=lam_init),
        grid=(b, DIFF_HEADS, tt // tq),
        in_specs=[
            pl.BlockSpec((1, tq, 128), lambda bi, h, i: (bi, i, OFF_QD // 128 + h)),
            pl.BlockSpec((1, tt, 128), lambda bi, h, i: (bi, 0, OFF_KD // 128 + h)),
            pl.BlockSpec((1, tt, 128), lambda bi, h, i: (bi, 0, OFF_VD // 128 + h)),
            pl.BlockSpec((2, DIFF_HEAD_DIM), lambda bi, h, i: (0, 0)),
            pl.BlockSpec((2, DIFF_HEAD_DIM), lambda bi, h, i: (0, 0)),
            pl.BlockSpec((1, 128), lambda bi, h, i: (0, 0)),
            pl.BlockSpec((tt, 128), lambda bi, h, i: (0, 0)),
            pl.BlockSpec((tt, 128), lambda bi, h, i: (0, 0)),
            pl.BlockSpec((tq, 128), lambda bi, h, i: (i, 0)),
            pl.BlockSpec((tq, 128), lambda bi, h, i: (i, 0)),
        ],
        out_specs=pl.BlockSpec((1, tq, 128), lambda bi, h, i: (bi, i, h)),
        out_shape=jax.ShapeDtypeStruct((b, tt, BRANCH_W), BF16),
        scratch_shapes=[pltpu.VMEM((tt, 128), BF16), pltpu.VMEM((128, tt), BF16)],
        compiler_params=_params("arbitrary", "arbitrary", "arbitrary"),
        name="diff_attn",
    )(proj, proj, proj, lam_q, lam_k, norm_g.reshape(1, -1), ct, st, ct, st)


def _s5_kernel(*refs, nb, nch, nctx):
    q = S5_CHUNK
    x_refs, (t_ref, uh_ref, hy_ref, p_ref), o_refs = refs[:q], refs[q:q + 4], refs[q + 4:2 * q + 4]
    xf_s, of_s, vf_s, vfs_s, vb_s, vbs_s, hf_s, hb_s = refs[2 * q + 4:]
    r = nb * nch
    g8 = pl.program_id(0) % 8
    lane_blk = lax.broadcasted_iota(jnp.int32, (1, 128), 1) // S5_GROUP

    @pl.when(g8 == 0)
    def _():
        for i in range(q):
            xf_s[i] = x_refs[i][...].astype(F32)
        of_s[...] = jnp.zeros_like(of_s)

    cols = []
    for k in range(q // 8):
        acc = jnp.zeros((r, 128), F32)
        for i8 in range(8):
            rot = pltpu.roll(xf_s[k * 8 + i8], lax.rem((i8 - g8) * S5_GROUP + 128, 128), 1)
            acc = jnp.where(lane_blk == i8, rot, acc)
        cols.append(acc)
    x = jnp.concatenate(cols, axis=1).astype(BF16)

    yloc = jnp.dot(x, t_ref[0], preferred_element_type=F32)
    v = jnp.dot(x, uh_ref[0], preferred_element_type=F32)
    vf_s[...] = v[:, 0:128]
    vfs_s[...] = v[:, 128:256]
    vb_s[...] = v[:, 256:384]
    vbs_s[...] = v[:, 384:512]
    pm = p_ref[0]
    zero = jnp.zeros((nb, 128), F32)

    def scan(order, v_ref, vs_ref, h_ref, p1, p2):
        h, hs = zero, zero
        for c in order:
            rows = pl.ds(c, nb, stride=nch)
            h_ref[rows, :] = h
            h, hs = h * p1 + hs * p2 + v_ref[rows, :], hs * p1 - h * p2 + vs_ref[rows, :]

    scan(range(nch), vf_s, vfs_s, hf_s, pm[0:1], pm[1:2])
    scan(list(range(nctx - 1, -1, -1)) + list(range(nch - 1, nctx - 1, -1)), vb_s, vbs_s, hb_s,
         pm[2:3], pm[3:4])
    hcat = jnp.concatenate([hf_s[...], hb_s[...]], axis=1).astype(BF16)
    y = yloc + jnp.dot(hcat, hy_ref[0], preferred_element_type=F32)

    for k in range(q // 8):
        yk = y[:, k * 128:(k + 1) * 128]
        for i8 in range(8):
            rot = pltpu.roll(yk, lax.rem((g8 - i8) * S5_GROUP + 128, 128), 1)
            i = k * 8 + i8
            of_s[i] = jnp.where(lane_blk == g8, rot, of_s[i])

    @pl.when(g8 == 7)
    def _():
        for i in range(q):
            o_refs[i][...] = of_s[i].astype(o_refs[i].dtype)


def _s5_weights(lam_re, lam_im, log_dt, b_re, b_im, c_re, c_im, d):
    q = S5_CHUNK
    g, p, s = S5_GROUPS, S5_STATE, S5_GROUP
    hi = lax.Precision.HIGHEST
    lr, li = lam_re.astype(F32), lam_im.astype(F32)
    dt = jnp.exp(log_dt.astype(F32))[..., None]
    tau = jnp.arange(q + 1, dtype=F32)[None, :, None, None]
    mag = jnp.exp((lr * dt)[:, None] * tau)
    ang = (li * dt)[:, None] * tau
    ar, ai = mag * jnp.cos(ang), mag * jnp.sin(ang)
    den = lr * lr + li * li
    fr = ((ar[:, 1] - 1.0) * lr + ai[:, 1] * li) / den
    fi = (ai[:, 1] * lr - (ar[:, 1] - 1.0) * li) / den
    br, bi = b_re.astype(F32), b_im.astype(F32)
    bbr = fr[..., None] * br - fi[..., None] * bi
    bbi = fr[..., None] * bi + fi[..., None] * br
    cr, ci = c_re.astype(F32)[:, None], c_im.astype(F32)[:, None]
    car = cr * ar[:, :, :, None, :] - ci * ai[:, :, :, None, :]
    cai = cr * ai[:, :, :, None, :] + ci * ar[:, :, :, None, :]
    kern = (jnp.einsum('dtgop,dgpi->dtgoi', car[:, :q], bbr, precision=hi)
            - jnp.einsum('dtgop,dgpi->dtgoi', cai[:, :q], bbi, precision=hi))
    ii = np.arange(q)
    lag = ii[None, :, None] - ii[:, None, None]
    sel_f = jnp.asarray(lag == ii[None, None, :], F32)
    sel_b = jnp.asarray(-lag == ii[None, None, :], F32)
    tmat = (jnp.einsum('jit,tgos->gjsio', sel_f, kern[0], precision=hi)
            + jnp.einsum('jit,tgos->gjsio', sel_b, kern[1], precision=hi))
    eye = jnp.eye(q, dtype=F32)[:, None, :, None] * jnp.eye(s, dtype=F32)[None, :, None, :]
    tmat = tmat + eye[None] * d.astype(F32).reshape(g, 1, s, 1, 1)
    tmat = tmat.reshape(g, q * s, q * s)

    def chunk_in(are, aim, bre, bim):
        re = jnp.einsum('jgp,gpi->gjip', are, bre) - jnp.einsum('jgp,gpi->gjip', aim, bim)
        im = jnp.einsum('jgp,gpi->gjip', are, bim) + jnp.einsum('jgp,gpi->gjip', aim, bre)
        return jnp.concatenate([re, im, im, re], axis=-1)

    uh = jnp.concatenate([chunk_in(ar[0, :q][::-1], ai[0, :q][::-1], bbr[0], bbi[0]),
                          chunk_in(ar[1, :q], ai[1, :q], bbr[1], bbi[1])], axis=-1).reshape(g, q * s, 8 * p)

    def state_out(re, im):
        return jnp.concatenate([jnp.transpose(re, (1, 3, 0, 2)), -jnp.transpose(im, (1, 3, 0, 2))], axis=1)

    hy = jnp.concatenate([state_out(car[0, 1:], cai[0, 1:]),
                          state_out(car[1, 1:][::-1], cai[1, 1:][::-1])], axis=1).reshape(g, 4 * p, q * s)
    pm = jnp.stack([jnp.concatenate([ar[0, q], ar[0, q]], -1), jnp.concatenate([-ai[0, q], ai[0, q]], -1),
                    jnp.concatenate([ar[1, q], ar[1, q]], -1), jnp.concatenate([-ai[1, q], ai[1, q]], -1)],
                   axis=1)
    pm = jnp.concatenate([pm, jnp.zeros_like(pm)], axis=1)
    return tmat.astype(BF16), uh.astype(BF16), hy.astype(BF16), pm


def _s5(proj, weights, tc):
    b, tt, _ = proj.shape
    q, g, s = S5_CHUNK, S5_GROUPS, S5_GROUP
    nch = tt // q
    tmat, uh, hy, pm = weights
    r = nch * b
    pw = N_PACK // 128
    x_specs = [pl.BlockSpec((r, 128), functools.partial(lambda gi, i: (0, i * pw + OFF_U // 128 + gi // 8), i=i))
               for i in range(q)]
    o_specs = [pl.BlockSpec((r, 128), lambda gi: (0, gi // 8))] * q
    ys = pl.pallas_call(
        functools.partial(_s5_kernel, nb=b, nch=nch, nctx=tc // q),
        grid=(g,),
        in_specs=x_specs + [pl.BlockSpec((1, q * s, q * s), lambda gi: (gi, 0, 0)),
                            pl.BlockSpec((1, q * s, 512), lambda gi: (gi, 0, 0)),
                            pl.BlockSpec((1, 256, q * s), lambda gi: (gi, 0, 0)),
                            pl.BlockSpec((1, 8, 128), lambda gi: (gi, 0, 0))],
        out_specs=o_specs,
        out_shape=[jax.ShapeDtypeStruct((r, BRANCH_W), BF16)] * q,
        scratch_shapes=[pltpu.VMEM((q, r, 128), F32), pltpu.VMEM((q, r, 128), F32)]
        + [pltpu.VMEM((r, 128), F32)] * 6,
        compiler_params=_params("arbitrary"),
        name="s5",
    )(*([proj.reshape(r, q * N_PACK)] * q), tmat, uh, hy, pm)
    return jnp.stack(ys, axis=1).reshape(b, tt, BRANCH_W)


def _ssd_kernel(*refs, backward, first, nctx_chunks, nchunks):
    if first:
        (xbc_ref, prev_ref, next_ref, z_ref, dt_ref, cw_ref, cb_ref, dtb_ref, alog_ref, dsk_ref,
         sel_ref, o_ref, st_s) = refs
        yin_ref = ng_ref = None
    else:
        (xbc_ref, prev_ref, next_ref, z_ref, dt_ref, cw_ref, cb_ref, dtb_ref, alog_ref, dsk_ref,
         sel_ref, yin_ref, ng_ref, o_ref, st_s) = refs
    qn = SSD_CHUNK
    step = pl.program_id(1)
    if backward:
        c = jnp.where(step < nctx_chunks, nctx_chunks - 1 - step, nchunks - 1 - (step - nctx_chunks))
    else:
        c = step

    @pl.when(step == 0)
    def _():
        st_s[...] = jnp.zeros_like(st_s)

    x = xbc_ref[0].astype(F32)
    has_prev = jnp.logical_and(c != 0, c != nctx_chunks)
    has_next = jnp.logical_and(c != nctx_chunks - 1, c != nchunks - 1)
    prow = jnp.where(has_prev, prev_ref[0, 7:8, :].astype(F32), 0.0)
    nrow = jnp.where(has_next, next_ref[0, 0:1, :].astype(F32), 0.0)
    rid = lax.broadcasted_iota(jnp.int32, (qn, 1), 0)
    xm = jnp.where(rid == 0, prow, pltpu.roll(x, 1, 0))
    xp = jnp.where(rid == qn - 1, nrow, pltpu.roll(x, qn - 1, 0))
    cw = cw_ref[...]
    conv = xm * cw[0:1] + x * cw[1:2] + xp * cw[2:3] + cb_ref[...]
    act = _silu(conv)
    xs = act[:, :BRANCH_W]
    gn = SSD_GROUPS * SSD_STATE
    bm = act[:, BRANCH_W:BRANCH_W + gn].astype(BF16)
    cm = act[:, BRANCH_W + gn:].astype(BF16)

    dt = _softplus(dt_ref[0].astype(F32) + dtb_ref[...])
    a = -jnp.exp(alog_ref[...])
    da = dt * a
    ri = lax.broadcasted_iota(jnp.int32, (qn, qn), 0)
    ci = lax.broadcasted_iota(jnp.int32, (qn, qn), 1)
    causal = (ci >= ri) if backward else (ci <= ri)
    ones_tri = jnp.where(causal, 1.0, 0.0).astype(BF16)
    d1 = da.astype(BF16)
    r1 = da - d1.astype(F32)
    d2 = r1.astype(BF16)
    d3 = (r1 - d2.astype(F32)).astype(BF16)
    cum = (jnp.dot(ones_tri, d1, preferred_element_type=F32)
           + jnp.dot(ones_tri, d2, preferred_element_type=F32)
           + jnp.dot(ones_tri, d3, preferred_element_type=F32))
    edge = cum[0:1] if backward else cum[qn - 1:qn]
    cum_t = cum.T
    dt_t = dt.T
    w_edge = dt * jnp.exp(edge - cum)
    e_in = jnp.exp(cum)

    def expand(v):
        v1 = v.astype(BF16)
        v2 = (v - v1.astype(F32)).astype(BF16)
        return (jnp.dot(v1, sel_ref[...], preferred_element_type=F32)
                + jnp.dot(v2, sel_ref[...], preferred_element_type=F32))

    xw = (xs * expand(w_edge)).astype(BF16)
    e_in_x = expand(e_in)
    xs_b = xs.astype(BF16)
    rpg = SSD_HEADS // SSD_GROUPS
    gw = rpg * SSD_HEAD_DIM
    ys = []
    for g in range(SSD_GROUPS):
        bg = bm[:, g * SSD_STATE:(g + 1) * SSD_STATE]
        cg = cm[:, g * SSD_STATE:(g + 1) * SSD_STATE]
        cb = lax.dot_general(cg, bg, (((1,), (1,)), ((), ())), preferred_element_type=F32)
        st_g = st_s[g * gw:(g + 1) * gw, :]
        y_off = lax.dot_general(cg, st_g.astype(BF16), (((1,), (1,)), ((), ())),
                                preferred_element_type=F32)
        yg = y_off * e_in_x[:, g * gw:(g + 1) * gw]
        parts = []
        for r in range(rpg):
            h = g * rpg + r
            dec = jnp.exp(jnp.where(causal, cum[:, h:h + 1] - cum_t[h:h + 1, :], -jnp.inf))
            wmat = (cb * dec * dt_t[h:h + 1, :]).astype(BF16)
            parts.append(jnp.dot(wmat, xs_b[:, h * SSD_HEAD_DIM:(h + 1) * SSD_HEAD_DIM],
                                 preferred_element_type=F32))
        ys.append(yg + jnp.concatenate(parts, axis=1))
        new = lax.dot_general(xw[:, g * gw:(g + 1) * gw], bg, (((0,), (0,)), ((), ())),
                              preferred_element_type=F32)
        for r in range(rpg):
            h = g * rpg + r
            rows = slice(g * gw + r * SSD_HEAD_DIM, g * gw + (r + 1) * SSD_HEAD_DIM)
            cd = jnp.exp(edge[:, h:h + 1])
            st_s[rows, :] = st_s[rows, :] * cd + new[r * SSD_HEAD_DIM:(r + 1) * SSD_HEAD_DIM, :]
    y = jnp.concatenate(ys, axis=1)
    if first:
        o_ref[0] = y + xs * dsk_ref[...]
    else:
        y = y + yin_ref[0]
        zz = z_ref[0].astype(F32)
        y = y * _silu(zz)
        y = y * lax.rsqrt(jnp.mean(y * y, axis=-1, keepdims=True) + RMS_EPS) * ng_ref[...]
        o_ref[0] = y.astype(o_ref.dtype)


def _ssd_pass(proj, conv_w, conv_b, dt_bias, a_log, d_skip, sel, yin, norm_g, tc, backward):
    b, tt, _ = proj.shape
    qn = SSD_CHUNK
    nchunks = tt // qn
    nctx = tc // qn
    first = yin is None
    last_blk8 = tt // 8 - 1

    def chunk_of(s):
        if backward:
            return jnp.where(s < nctx, nctx - 1 - s, nchunks - 1 - (s - nctx))
        return s

    xbc_blk = OFF_XBC // SSD_CONV_CH
    in_specs = [
        pl.BlockSpec((1, qn, SSD_CONV_CH), lambda bi, s: (bi, chunk_of(s), xbc_blk)),
        pl.BlockSpec((1, 8, SSD_CONV_CH),
                     lambda bi, s: (bi, jnp.maximum(chunk_of(s) * (qn // 8) - 1, 0), xbc_blk)),
        pl.BlockSpec((1, 8, SSD_CONV_CH),
                     lambda bi, s: (bi, jnp.minimum((chunk_of(s) + 1) * (qn // 8), last_blk8), xbc_blk)),
        pl.BlockSpec((1, qn, BRANCH_W), lambda bi, s: (bi, chunk_of(s), OFF_Z // BRANCH_W)),
        pl.BlockSpec((1, qn, 128), lambda bi, s: (bi, chunk_of(s), OFF_DT // 128)),
        pl.BlockSpec((8, SSD_CONV_CH), lambda bi, s: (0, 0)),
        pl.BlockSpec((1, SSD_CONV_CH), lambda bi, s: (0, 0)),
        pl.BlockSpec((1, 128), lambda bi, s: (0, 0)),
        pl.BlockSpec((1, 128), lambda bi, s: (0, 0)),
        pl.BlockSpec((1, BRANCH_W), lambda bi, s: (0, 0)),
        pl.BlockSpec((128, BRANCH_W), lambda bi, s: (0, 0)),
    ]
    args = [proj, proj, proj, proj, proj, conv_w, conv_b, dt_bias, a_log, d_skip, sel]
    if not first:
        in_specs += [pl.BlockSpec((1, qn, BRANCH_W), lambda bi, s: (bi, chunk_of(s), 0)),
                     pl.BlockSpec((1, BRANCH_W), lambda bi, s: (0, 0))]
        args += [yin, norm_g]
    return pl.pallas_call(
        functools.partial(_ssd_kernel, backward=backward, first=first, nctx_chunks=nctx, nchunks=nchunks),
        grid=(b, nchunks),
        in_specs=in_specs,
        out_specs=pl.BlockSpec((1, qn, BRANCH_W), lambda bi, s: (bi, chunk_of(s), 0)),
        out_shape=jax.ShapeDtypeStruct((b, tt, BRANCH_W), F32 if first else BF16),
        scratch_shapes=[pltpu.VMEM((SSD_HEADS * SSD_HEAD_DIM, SSD_STATE), F32)],
        compiler_params=_params("arbitrary", "arbitrary"),
        name="ssd_bwd" if backward else "ssd_fwd",
    )(*args)


def _ssd(proj, conv_w, conv_b, dt_bias, a_log, d_skip, norm_g, tc):
    def lanes(v):
        return jnp.pad(v.astype(F32), (0, 128 - SSD_HEADS)).reshape(1, 128)

    cw = jnp.pad(conv_w.astype(F32), ((0, 5), (0, 0)))
    cb = conv_b.astype(F32).reshape(1, -1)
    dsk = jnp.repeat(d_skip.astype(F32), SSD_HEAD_DIM).reshape(1, -1)
    sel = (jnp.arange(128)[:, None] == (jnp.arange(BRANCH_W)[None, :] // SSD_HEAD_DIM)).astype(BF16)
    y1 = _ssd_pass(proj, cw, cb, lanes(dt_bias[0]), lanes(a_log[0]), dsk, sel, None, None, tc, False)
    return _ssd_pass(proj, cw, cb, lanes(dt_bias[1]), lanes(a_log[1]), jnp.zeros_like(dsk), sel,
                     y1, norm_g.astype(F32).reshape(1, -1), tc, True)


def _merge_kernel(ya_ref, ga_ref, yb_ref, gb_ref, yc_ref, yd_ref, gd_ref, gm_ref, bm_ref,
                  wb_ref, wg_ref, bg_ref, o_ref):
    def f(ref):
        return ref[0].astype(F32)

    g = _gelu_tanh(f(yb_ref))
    glu = g * _sigmoid(jnp.dot(g.astype(BF16), wg_ref[...], preferred_element_type=F32) + bg_ref[...])
    branches = (f(ya_ref) * _silu(f(ga_ref)), glu * _silu(f(gb_ref)), f(yc_ref), f(yd_ref) * _silu(f(gd_ref)))
    acc = None
    for n, br in enumerate(branches):
        gate = _sigmoid(gm_ref[0, :, n * D_MODEL:(n + 1) * D_MODEL].astype(F32) + bm_ref[n:n + 1, :])
        term = gate * jnp.dot(br.astype(BF16), wb_ref[n], preferred_element_type=F32)
        acc = term if acc is None else acc + term
    o_ref[0] = acc.astype(o_ref.dtype)


def _merge(proj, ya, yb, yc, yd, b_merge, w_branch, w_glu, b_glu):
    b, tt, _ = proj.shape
    tm = Q_BLOCK
    bw = BRANCH_W

    def pspec(off):
        return pl.BlockSpec((1, tm, bw), lambda bi, i: (bi, i, off // bw))

    yspec = pl.BlockSpec((1, tm, bw), lambda bi, i: (bi, i, 0))
    once = dict(pipeline_mode=pl.Buffered(1))
    return pl.pallas_call(
        _merge_kernel,
        grid=(b, tt // tm),
        in_specs=[yspec, pspec(OFF_GA), yspec, pspec(OFF_GB), yspec, yspec, pspec(OFF_GD),
                  pl.BlockSpec((1, tm, N_BRANCH * D_MODEL), lambda bi, i: (bi, i, 0)),
                  pl.BlockSpec((N_BRANCH, D_MODEL), lambda bi, i: (0, 0)),
                  pl.BlockSpec((N_BRANCH, bw, D_MODEL), lambda bi, i: (0, 0, 0), **once),
                  pl.BlockSpec((bw, bw), lambda bi, i: (0, 0), **once),
                  pl.BlockSpec((1, bw), lambda bi, i: (0, 0))],
        out_specs=pl.BlockSpec((1, tm, D_MODEL), lambda bi, i: (bi, i, 0)),
        out_shape=jax.ShapeDtypeStruct((b, tt, D_MODEL), BF16),
        compiler_params=_params("arbitrary", "arbitrary"),
        name="merge",
    )(ya, proj, yb, proj, yc, yd, proj, proj, b_merge, w_branch, w_glu, b_glu.reshape(1, -1))


def _out_kernel(m_ref, x_ref, mod_ref, w_ref, lg_ref, lb_ref, o_ref, *, tc, tm, row0):
    i = pl.program_id(1)
    out = jnp.dot(m_ref[0], w_ref[...], preferred_element_type=F32)
    row = row0 + i * tm + lax.broadcasted_iota(jnp.int32, (tm, 1), 0)
    m = mod_ref[0]
    gate = jnp.where(row < tc, m[4:5], m[5:6])
    v = DEEPNORM_ALPHA * x_ref[0] + gate * out
    mu = jnp.mean(v, axis=-1, keepdims=True)
    vc = v - mu
    var = jnp.mean(vc * vc, axis=-1, keepdims=True)
    o_ref[0] = vc * lax.rsqrt(var + LN_EPS) * lg_ref[...] + lb_ref[...]


def _out(merged, xa, mod, w_out, ln_g, ln_b, tc, latent_only):
    b, tt, d = xa.shape
    tm = Q_BLOCK
    skip = tc // tm if latent_only else 0
    nblk = tt // tm - skip
    spec = pl.BlockSpec((1, tm, d), lambda bi, i: (bi, i + skip, 0))
    return pl.pallas_call(
        functools.partial(_out_kernel, tc=tc, tm=tm, row0=skip * tm),
        grid=(b, nblk),
        in_specs=[spec, spec,
                  pl.BlockSpec((1, 8, d), lambda bi, i: (bi, 0, 0)),
                  pl.BlockSpec((d, d), lambda bi, i: (0, 0)),
                  pl.BlockSpec((1, d), lambda bi, i: (0, 0)),
                  pl.BlockSpec((1, d), lambda bi, i: (0, 0))],
        out_specs=pl.BlockSpec((1, tm, d), lambda bi, i: (bi, i, 0)),
        out_shape=jax.ShapeDtypeStruct((b, nblk * tm, d), F32),
        compiler_params=_params("arbitrary", "arbitrary"),
        name="out_proj",
    )(merged, xa, mod, w_out, ln_g.reshape(1, -1), ln_b.reshape(1, -1))


def _half_split(w, heads):
    k = w.shape[0]
    return w.reshape(k, heads, ROPE_DIM // 2, 2).transpose(0, 1, 3, 2).reshape(k, heads * ROPE_DIM)


def _pack_w_in(w):
    sp = np.cumsum([0, MLA_Q_LORA, MLA_KV_LORA, MLA_ROPE, BRANCH_W, BRANCH_W, BRANCH_W, BRANCH_W,
                    SSD_CONV_CH, SSD_HEADS, BRANCH_W, BRANCH_W, BRANCH_W, BRANCH_W, N_BRANCH * D_MODEL])
    seg = [w[:, sp[k]:sp[k + 1]] for k in range(14)]
    cq, ckv, kr, ga, u, gb, z, xbc, dt, qd, kd, vd, gd, gm = seg
    kr_hs = _half_split(kr, 1)
    kr_sw = jnp.concatenate([kr_hs[:, 32:], kr_hs[:, :32]], axis=1)
    pad = jnp.zeros((w.shape[0], 128 - SSD_HEADS), w.dtype)
    packed = jnp.concatenate([gm, xbc, ga, u, gb, z, _half_split(qd, 2 * DIFF_HEADS),
                              _half_split(kd, 2 * DIFF_HEADS), vd, gd, cq, ckv, kr_hs, kr_sw, dt, pad],
                             axis=1)
    return packed.astype(BF16)


def _pack_mla(w_uq, w_ukv):
    k = w_uq.shape[0]
    wq = w_uq.reshape(k, MLA_HEADS, MLA_NOPE + MLA_ROPE)
    rope = wq[:, :, MLA_NOPE:].reshape(k, MLA_HEADS, ROPE_DIM // 2, 2)
    ev, od = rope[..., 0], rope[..., 1]
    wq = jnp.concatenate([wq[:, :, :MLA_NOPE], ev, od, od, ev], axis=-1)
    wkv = w_ukv.reshape(w_ukv.shape[0], MLA_HEADS, MLA_NOPE + MLA_V)
    return (wq.transpose(1, 0, 2).astype(BF16),
            wkv[:, :, :MLA_NOPE].transpose(1, 0, 2).astype(BF16),
            wkv[:, :, MLA_NOPE:].transpose(1, 2, 0).astype(BF16))


def _rope_tables(tl, tc):
    rows = tl // GRID_W
    row_id = jnp.repeat(jnp.arange(rows, dtype=F32), GRID_W)
    col_id = jnp.tile(jnp.arange(GRID_W, dtype=F32), rows)
    quarter = ROPE_DIM // 4
    inv_freq = ROPE_BASE ** (-jnp.arange(quarter, dtype=F32) / quarter)
    ang = jnp.concatenate([row_id[:, None] * inv_freq, col_id[:, None] * inv_freq], axis=-1)
    cos = jnp.concatenate([jnp.ones((tc, ROPE_DIM // 2), F32), jnp.cos(ang)], axis=0)
    sin = jnp.concatenate([jnp.zeros((tc, ROPE_DIM // 2), F32), jnp.sin(ang)], axis=0)
    zero = jnp.zeros_like(cos)
    mla = (jnp.concatenate([cos, cos, zero, zero], axis=1), jnp.concatenate([zero, zero, -sin, sin], axis=1))
    diff = (jnp.concatenate([cos, cos, cos, cos], axis=1), jnp.concatenate([-sin, sin, -sin, sin], axis=1))
    return mla, diff


def _layer(xa, cs, p, tables, layer_idx, tc, last):
    b = xa.shape[0]
    mla_tab, diff_tab = tables
    mod = _ada(cs, p['w_ada'], p['b_ada'])
    d = D_MODEL
    shift, scale, gate = mod[:, :d], mod[:, d:2 * d], mod[:, 2 * d:]
    zeros = jnp.zeros((b, d), F32)
    modb = jnp.stack([jnp.broadcast_to(scale[0], (b, d)), jnp.broadcast_to(shift[0], (b, d)),
                      scale[1:1 + b], shift[1:1 + b],
                      jnp.broadcast_to(gate[0], (b, d)), gate[1:1 + b], zeros, zeros], axis=1)
    proj = _inproj(xa, modb, _pack_w_in(p['w_in']), tc)
    wq, wk, wv = _pack_mla(p['mla_w_uq'], p['mla_w_ukv'])
    ya = _mla(proj, wq, wk, wv, p['mla_q_norm'], p['mla_kv_norm'], mla_tab[0], mla_tab[1], tc)
    lam_init = 0.8 - 0.6 * math.exp(-0.3 * layer_idx)
    yd = _diff(proj, p['diff_lambda_q'], p['diff_lambda_k'], p['diff_norm'], diff_tab[0], diff_tab[1],
               lam_init, tc)
    yb = _s5(proj, _s5_weights(p['s5_lambda_re'], p['s5_lambda_im'], p['s5_log_dt'], p['s5_b_re'],
                               p['s5_b_im'], p['s5_c_re'], p['s5_c_im'], p['s5_d']), tc)
    yc = _ssd(proj, p['ssd_conv_w'], p['ssd_conv_b'], p['ssd_dt_bias'], p['ssd_a_log'], p['ssd_d'],
              p['ssd_norm'], tc)
    merged = _merge(proj, ya, yb, yc, yd, p['b_merge'], p['w_branch'].astype(BF16),
                    p['s5_w_glu'].astype(BF16), p['s5_b_glu'])
    return _out(merged, xa, modb, p['w_out'].astype(BF16), p['ln_g'], p['ln_b'], tc, last)


def kernel(x, c, ctx, c_ctx, w_ada, b_ada, w_in, mla_q_norm, mla_w_uq, mla_kv_norm, mla_w_ukv,
           s5_lambda_re, s5_lambda_im, s5_log_dt, s5_b_re, s5_b_im, s5_c_re, s5_c_im, s5_d,
           s5_w_glu, s5_b_glu, ssd_conv_w, ssd_conv_b, ssd_dt_bias, ssd_a_log, ssd_d, ssd_norm,
           diff_lambda_q, diff_lambda_k, diff_norm, b_merge, w_branch, w_out, ln_g, ln_b):
    b, tl, d = x.shape
    tc = ctx.shape[1]
    stacked = dict(
        w_ada=w_ada, b_ada=b_ada, w_in=w_in, mla_q_norm=mla_q_norm, mla_w_uq=mla_w_uq,
        mla_kv_norm=mla_kv_norm, mla_w_ukv=mla_w_ukv, s5_lambda_re=s5_lambda_re, s5_lambda_im=s5_lambda_im,
        s5_log_dt=s5_log_dt, s5_b_re=s5_b_re, s5_b_im=s5_b_im, s5_c_re=s5_c_re, s5_c_im=s5_c_im, s5_d=s5_d,
        s5_w_glu=s5_w_glu, s5_b_glu=s5_b_glu, ssd_conv_w=ssd_conv_w, ssd_conv_b=ssd_conv_b,
        ssd_dt_bias=ssd_dt_bias, ssd_a_log=ssd_a_log, ssd_d=ssd_d, ssd_norm=ssd_norm,
        diff_lambda_q=diff_lambda_q, diff_lambda_k=diff_lambda_k, diff_norm=diff_norm, b_merge=b_merge,
        w_branch=w_branch, w_out=w_out, ln_g=ln_g, ln_b=ln_b)
    tables = _rope_tables(tl, tc)
    cs = jnp.concatenate([c_ctx[None], c, jnp.zeros((8 - 1 - b, d), c.dtype)], axis=0)
    xa = jnp.concatenate([ctx, x], axis=1)
    depth = w_in.shape[0]
    for i in range(depth):
        p = {k: v[i] for k, v in stacked.items()}
        xa = _layer(xa, cs, p, tables, i, tc, i == depth - 1)
    return xa
```

```python
import functools
import math

import numpy as np
import jax
import jax.numpy as jnp
from jax import lax
from jax.experimental import pallas as pl
from jax.experimental.pallas import tpu as pltpu

F32 = jnp.float32
BF16 = jnp.bfloat16

D_MODEL = 2048
DEPTH = 2
GRID_W = 64
N_BRANCH = 4
BRANCH_W = 1024
ROPE_DIM = 64
ROPE_BASE = 10000.0

MLA_HEADS = 8
MLA_NOPE = 128
MLA_ROPE = ROPE_DIM
MLA_V = 128
MLA_Q_LORA = 512
MLA_KV_LORA = 256

S5_GROUP = 16
S5_GROUPS = BRANCH_W // S5_GROUP
S5_STATE = 64
S5_CHUNK = 16

SSD_HEAD_DIM = 64
SSD_HEADS = BRANCH_W // SSD_HEAD_DIM
SSD_GROUPS = 4
SSD_STATE = 128
SSD_CHUNK = 128
SSD_CONV_CH = BRANCH_W + 2 * SSD_GROUPS * SSD_STATE

DIFF_HEAD_DIM = ROPE_DIM
DIFF_HEADS = BRANCH_W // (2 * DIFF_HEAD_DIM)

LN_EPS = 1e-5
RMS_EPS = 1e-6
DEEPNORM_ALPHA = (2 * DEPTH) ** 0.25

V7X_VMEM_LIMIT_BYTES = 56 * 1024 * 1024
Q_BLOCK = 256
MLA_HEADS_PER_STEP = 2
LOG2E = math.log2(math.e)

OFF_GM = 0
OFF_XBC = 8192
OFF_GA = 10240
OFF_U = 11264
OFF_GB = 12288
OFF_Z = 13312
OFF_QD = 14336
OFF_KD = 15360
OFF_VD = 16384
OFF_GD = 17408
OFF_CQ = 18432
OFF_CKV = 18944
OFF_KR = 19200
OFF_DT = 19328
N_PACK = 19456
IN_TILE_N = 1024


def _sigmoid(x):
    return 0.5 * (1.0 + jnp.tanh(0.5 * x))


def _silu(x):
    return x * _sigmoid(x)


def _gelu_tanh(x):
    return 0.5 * x * (1.0 + jnp.tanh(math.sqrt(2.0 / math.pi) * (x + 0.044715 * (x * x * x))))


def _softplus(x):
    return jnp.maximum(x, 0.0) + jnp.log(1.0 + jnp.exp(-jnp.abs(x)))


def _params(*sem):
    return pltpu.CompilerParams(dimension_semantics=sem, vmem_limit_bytes=V7X_VMEM_LIMIT_BYTES)


def _ada_kernel(c_ref, w_ref, b_ref, o_ref):
    s = _silu(c_ref[...])
    o_ref[...] = jnp.dot(s.astype(BF16), w_ref[0].astype(BF16),
                         preferred_element_type=F32) + b_ref[...]


def _ada(cs, w_ada, b_ada, layer):
    n = w_ada.shape[2]
    tn = 512
    return pl.pallas_call(
        _ada_kernel,
        grid=(n // tn,),
        in_specs=[pl.BlockSpec((8, D_MODEL), lambda j: (0, 0)),
                  pl.BlockSpec((1, D_MODEL, tn), lambda j: (layer, 0, j)),
                  pl.BlockSpec((1, tn), lambda j: (0, j))],
        out_specs=pl.BlockSpec((8, tn), lambda j: (0, j)),
        out_shape=jax.ShapeDtypeStruct((8, n), F32),
        compiler_params=_params("arbitrary"),
        name="ada",
    )(cs, w_ada, b_ada.reshape(1, n))


def _inproj_kernel(x_ref, mod_ref, w_ref, o_ref, h_ref, *, tc, tm):
    i = pl.program_id(1)

    @pl.when(pl.program_id(2) == 0)
    def _():
        x = x_ref[0]
        mu = jnp.mean(x, axis=-1, keepdims=True)
        xc = x - mu
        var = jnp.mean(xc * xc, axis=-1, keepdims=True)
        xn = xc * lax.rsqrt(var + LN_EPS)
        row = i * tm + lax.broadcasted_iota(jnp.int32, (tm, 1), 0)
        is_ctx = row < tc
        m = mod_ref[0]
        scale = jnp.where(is_ctx, m[0:1], m[2:3])
        shift = jnp.where(is_ctx, m[1:2], m[3:4])
        h_ref[...] = (xn * (1.0 + scale) + shift).astype(BF16)

    o_ref[0] = jnp.dot(h_ref[...], w_ref[...], preferred_element_type=F32).astype(o_ref.dtype)


def _inproj(xa, mod, w_pack, tc):
    b, tt, d = xa.shape
    tm = tt // 4
    tn = IN_TILE_N
    return pl.pallas_call(
        functools.partial(_inproj_kernel, tc=tc, tm=tm),
        grid=(b, tt // tm, N_PACK // tn),
        in_specs=[pl.BlockSpec((1, tm, d), lambda bi, i, j: (bi, i, 0)),
                  pl.BlockSpec((1, 8, d), lambda bi, i, j: (bi, 0, 0)),
                  pl.BlockSpec((d, tn), lambda bi, i, j: (0, j))],
        out_specs=pl.BlockSpec((1, tm, tn), lambda bi, i, j: (bi, i, j)),
        out_shape=jax.ShapeDtypeStruct((b, tt, N_PACK), BF16),
        scratch_shapes=[pltpu.VMEM((tm, d), BF16)],
        compiler_params=_params("arbitrary", "arbitrary", "arbitrary"),
        name="inproj",
    )(xa, mod, w_pack)


def _mla_kernel(cq_ref, ckv_ref, kr_ref, wq_ref, wk_ref, wv_ref, qn_ref, kvn_ref,
                ct_ref, st_ref, ctq_ref, stq_ref, o_ref, k_s, v_s, *, tc, tt, scale):
    i = pl.program_id(2)

    @pl.when(i == 0)
    def _():
        ckv = ckv_ref[0].astype(F32)
        r = lax.rsqrt(jnp.mean(ckv * ckv, axis=-1, keepdims=True) + RMS_EPS)
        ckvn = (ckv * r * kvn_ref[...]).astype(BF16)
        kr = kr_ref[0].astype(F32)
        kroped = (kr * ct_ref[...] + pltpu.roll(kr * st_ref[...], 64, 1)).astype(BF16)
        for hh in range(MLA_HEADS_PER_STEP):
            k_s[hh, :, 0:MLA_NOPE] = jnp.dot(ckvn, wk_ref[hh], preferred_element_type=F32).astype(BF16)
            k_s[hh, :, MLA_NOPE:] = kroped
            v_s[hh] = jnp.dot(ckvn, wv_ref[hh], preferred_element_type=F32).astype(BF16)

    cq = cq_ref[0].astype(F32)
    r = lax.rsqrt(jnp.mean(cq * cq, axis=-1, keepdims=True) + RMS_EPS)
    cqn = (cq * r * qn_ref[...]).astype(BF16)
    qfs = []
    for hh in range(MLA_HEADS_PER_STEP):
        q = jnp.dot(cqn, wq_ref[hh], preferred_element_type=F32)
        qh = q[:, MLA_NOPE:]
        qr = qh * ctq_ref[...] + pltpu.roll(qh * stq_ref[...], 64, 1)
        qfs.append((jnp.concatenate([q[:, :MLA_NOPE], qr], axis=1) * (scale * LOG2E)).astype(BF16))

    def attend(nk):
        outs = []
        for hh in range(MLA_HEADS_PER_STEP):
            s = lax.dot_general(qfs[hh], k_s[hh, 0:nk, :], (((1,), (1,)), ((), ())),
                                preferred_element_type=F32)
            p = jnp.exp2(s - jnp.max(s, axis=1, keepdims=True))
            l = jnp.sum(p, axis=1, keepdims=True)
            o = jnp.dot(p.astype(BF16), v_s[hh, 0:nk, :], preferred_element_type=F32)
            outs.append(o * (1.0 / l))
        o_ref[0] = jnp.concatenate(outs, axis=1).astype(o_ref.dtype)

    @pl.when(i == 0)
    def _():
        attend(tc)

    @pl.when(i > 0)
    def _():
        attend(tt)


def _mla(proj, wq, wk, wv, q_norm, kv_norm, ct, st, tc):
    b, tt, _ = proj.shape
    tq = Q_BLOCK
    assert tc == tq
    scale = (MLA_NOPE + MLA_ROPE) ** -0.5
    hps = MLA_HEADS_PER_STEP
    return pl.pallas_call(
        functools.partial(_mla_kernel, tc=tc, tt=tt, scale=scale),
        grid=(b, MLA_HEADS // hps, tt // tq),
        in_specs=[
            pl.BlockSpec((1, tq, MLA_Q_LORA), lambda bi, h, i: (bi, i, OFF_CQ // MLA_Q_LORA)),
            pl.BlockSpec((1, tt, MLA_KV_LORA), lambda bi, h, i: (bi, 0, OFF_CKV // MLA_KV_LORA)),
            pl.BlockSpec((1, tt, 128), lambda bi, h, i: (bi, 0, OFF_KR // 128)),
            pl.BlockSpec((hps, MLA_Q_LORA, 256), lambda bi, h, i: (h, 0, 0)),
            pl.BlockSpec((hps, MLA_KV_LORA, MLA_NOPE), lambda bi, h, i: (h, 0, 0)),
            pl.BlockSpec((hps, MLA_KV_LORA, MLA_V), lambda bi, h, i: (h, 0, 0)),
            pl.BlockSpec((1, MLA_Q_LORA), lambda bi, h, i: (0, 0)),
            pl.BlockSpec((1, MLA_KV_LORA), lambda bi, h, i: (0, 0)),
            pl.BlockSpec((tt, 128), lambda bi, h, i: (0, 0)),
            pl.BlockSpec((tt, 128), lambda bi, h, i: (0, 0)),
            pl.BlockSpec((tq, 128), lambda bi, h, i: (i, 0)),
            pl.BlockSpec((tq, 128), lambda bi, h, i: (i, 0)),
        ],
        out_specs=pl.BlockSpec((1, tq, hps * MLA_V), lambda bi, h, i: (bi, i, h)),
        out_shape=jax.ShapeDtypeStruct((b, tt, BRANCH_W), BF16),
        scratch_shapes=[pltpu.VMEM((hps, tt, 256), BF16), pltpu.VMEM((hps, tt, MLA_V), BF16)],
        compiler_params=_params("arbitrary", "arbitrary", "arbitrary"),
        name="mla_attn",
    )(proj, proj, proj, wq, wk, wv, q_norm.reshape(1, -1), kv_norm.reshape(1, -1), ct, st, ct, st)


def _diff_kernel(q_ref, k_ref, v_ref, lq_ref, lk_ref, g_ref, ct_ref, st_ref, ctq_ref, stq_ref,
                 o_ref, k_s, *, tc, tt, lam_init):
    i = pl.program_id(2)
    lane = lax.broadcasted_iota(jnp.int32, (1, 128), 1)
    first_half = (lane & 32) == 0

    def rope(x, c, s):
        xs = jnp.where(first_half, pltpu.roll(x, 96, 1), pltpu.roll(x, 32, 1))
        return x * c + xs * s

    @pl.when(i == 0)
    def _():
        k_s[...] = rope(k_ref[0].astype(F32), ct_ref[...], st_ref[...]).astype(BF16)

    q = rope(q_ref[0].astype(F32), ctq_ref[...], stq_ref[...]) * (DIFF_HEAD_DIM ** -0.5 * LOG2E)
    q1 = jnp.where(lane < 64, q, 0.0).astype(BF16)
    q2 = jnp.where(lane >= 64, q, 0.0).astype(BF16)
    lqk = lq_ref[...] * lk_ref[...]
    lam = (jnp.exp(jnp.sum(lqk[0:1], axis=1, keepdims=True))
           - jnp.exp(jnp.sum(lqk[1:2], axis=1, keepdims=True)) + lam_init)

    def attend(nk):
        kk = k_s[0:nk, :]
        nt = (((1,), (1,)), ((), ()))
        vv = v_ref[0, 0:nk, :]

        def one_map(qm):
            s = lax.dot_general(qm, kk, nt, preferred_element_type=F32)
            e = jnp.exp2(s - jnp.max(s, axis=1, keepdims=True))
            l = jnp.sum(e, axis=1, keepdims=True)
            return jnp.dot(e.astype(BF16), vv, preferred_element_type=F32), l

        o1, l1 = one_map(q1)
        o2, l2 = one_map(q2)
        o = o1 * (1.0 / l1) - o2 * (lam / l2)
        y = o * lax.rsqrt(jnp.mean(o * o, axis=-1, keepdims=True) + RMS_EPS)
        o_ref[0] = (y * g_ref[...] * (1.0 - lam_init)).astype(o_ref.dtype)

    @pl.when(i == 0)
    def _():
        attend(tc)

    @pl.when(i > 0)
    def _():
        attend(tt)


def _diff(proj, lam_q, lam_k, norm_g, ct, st, lam_init, tc):
    b, tt, _ = proj.shape
    tq = Q_BLOCK
    assert tc == tq
    return pl.pallas_call(
        functools.partial(_diff_kernel, tc=tc, tt=tt, lam_init=lam_init),
        grid=(b, DIFF_HEADS, tt // tq),
        in_specs=[
            pl.BlockSpec((1, tq, 128), lambda bi, h, i: (bi, i, OFF_QD // 128 + h)),
            pl.BlockSpec((1, tt, 128), lambda bi, h, i: (bi, 0, OFF_KD // 128 + h)),
            pl.BlockSpec((1, tt, 128), lambda bi, h, i: (bi, 0, OFF_VD // 128 + h)),
            pl.BlockSpec((2, DIFF_HEAD_DIM), lambda bi, h, i: (0, 0)),
            pl.BlockSpec((2, DIFF_HEAD_DIM), lambda bi, h, i: (0, 0)),
            pl.BlockSpec((1, 128), lambda bi, h, i: (0, 0)),
            pl.BlockSpec((tt, 128), lambda bi, h, i: (0, 0)),
            pl.BlockSpec((tt, 128), lambda bi, h, i: (0, 0)),
            pl.BlockSpec((tq, 128), lambda bi, h, i: (i, 0)),
            pl.BlockSpec((tq, 128), lambda bi, h, i: (i, 0)),
        ],
        out_specs=pl.BlockSpec((1, tq, 128), lambda bi, h, i: (bi, i, h)),
        out_shape=jax.ShapeDtypeStruct((b, tt, BRANCH_W), BF16),
        scratch_shapes=[pltpu.VMEM((tt, 128), BF16)],
        compiler_params=_params("arbitrary", "arbitrary", "arbitrary"),
        name="diff_attn",
    )(proj, proj, proj, lam_q, lam_k, norm_g.reshape(1, -1), ct, st, ct, st)


def _s5_kernel(*refs, nb, nch, nctx):
    q = S5_CHUNK
    x_refs, (t_ref, uh_ref, hy_ref, p_ref), o_ref = refs[:q], refs[q:q + 4], refs[q + 4]
    xf_s, of_s, vf_s, vfs_s, vb_s, vbs_s, hf_s, hb_s = refs[q + 5:]
    r = nb * nch
    g8 = pl.program_id(0) % 8
    lane_blk = lax.broadcasted_iota(jnp.int32, (1, 128), 1) // S5_GROUP

    @pl.when(g8 == 0)
    def _():
        for i in range(q):
            xf_s[i] = x_refs[i][...].astype(F32)
        of_s[...] = jnp.zeros_like(of_s)

    cols = []
    for k in range(q // 8):
        acc = jnp.zeros((r, 128), F32)
        for i8 in range(8):
            rot = pltpu.roll(xf_s[k * 8 + i8], lax.rem((i8 - g8) * S5_GROUP + 128, 128), 1)
            acc = jnp.where(lane_blk == i8, rot, acc)
        cols.append(acc)
    x = jnp.concatenate(cols, axis=1).astype(BF16)

    yloc = jnp.dot(x, t_ref[0], preferred_element_type=F32)
    v = jnp.dot(x, uh_ref[0], preferred_element_type=F32)
    vf_s[...] = v[:, 0:128]
    vfs_s[...] = v[:, 128:256]
    vb_s[...] = v[:, 256:384]
    vbs_s[...] = v[:, 384:512]
    pm = p_ref[0]
    zero = jnp.zeros((nb, 128), F32)

    def scan(order, v_ref, vs_ref, h_ref, p1, p2):
        h, hs = zero, zero
        for c in order:
            rows = pl.ds(c, nb, stride=nch)
            h_ref[rows, :] = h
            h, hs = h * p1 + hs * p2 + v_ref[rows, :], hs * p1 - h * p2 + vs_ref[rows, :]

    scan(range(nch), vf_s, vfs_s, hf_s, pm[0:1], pm[1:2])
    scan(list(range(nctx - 1, -1, -1)) + list(range(nch - 1, nctx - 1, -1)), vb_s, vbs_s, hb_s,
         pm[2:3], pm[3:4])
    hcat = jnp.concatenate([hf_s[...], hb_s[...]], axis=1).astype(BF16)
    y = yloc + jnp.dot(hcat, hy_ref[0], preferred_element_type=F32)

    for k in range(q // 8):
        yk = y[:, k * 128:(k + 1) * 128]
        for i8 in range(8):
            rot = pltpu.roll(yk, lax.rem((g8 - i8) * S5_GROUP + 128, 128), 1)
            rows = pl.ds(k * 8 + i8, r, stride=q)
            of_s[rows, :] = jnp.where(lane_blk == g8, rot, of_s[rows, :])

    @pl.when(g8 == 7)
    def _():
        o_ref[...] = of_s[...].astype(o_ref.dtype)


def _s5_weights(lam_re, lam_im, log_dt, b_re, b_im, c_re, c_im, d):
    q = S5_CHUNK
    g, p, s = S5_GROUPS, S5_STATE, S5_GROUP
    hi = lax.Precision.HIGHEST
    lr, li = lam_re.astype(F32), lam_im.astype(F32)
    dt = jnp.exp(log_dt.astype(F32))[..., None]
    tau = jnp.arange(q + 1, dtype=F32)[None, :, None, None]
    mag = jnp.exp((lr * dt)[:, None] * tau)
    ang = (li * dt)[:, None] * tau
    ar, ai = mag * jnp.cos(ang), mag * jnp.sin(ang)
    den = lr * lr + li * li
    fr = ((ar[:, 1] - 1.0) * lr + ai[:, 1] * li) / den
    fi = (ai[:, 1] * lr - (ar[:, 1] - 1.0) * li) / den
    br, bi = b_re.astype(F32), b_im.astype(F32)
    bbr = fr[..., None] * br - fi[..., None] * bi
    bbi = fr[..., None] * bi + fi[..., None] * br
    cr, ci = c_re.astype(F32)[:, None], c_im.astype(F32)[:, None]
    car = cr * ar[:, :, :, None, :] - ci * ai[:, :, :, None, :]
    cai = cr * ai[:, :, :, None, :] + ci * ar[:, :, :, None, :]
    bbr_t = jnp.swapaxes(bbr, 2, 3)[:, None, :, None]
    bbi_t = jnp.swapaxes(bbi, 2, 3)[:, None, :, None]
    kern = jnp.sum(car[:, :q, :, :, None, :] * bbr_t - cai[:, :q, :, :, None, :] * bbi_t,
                   axis=-1)
    ii = np.arange(q)
    lag = ii[None, :, None] - ii[:, None, None]
    sel_f = jnp.asarray(lag == ii[None, None, :], F32)
    sel_b = jnp.asarray(-lag == ii[None, None, :], F32)
    tmat = (jnp.einsum('jit,tgos->gjsio', sel_f, kern[0], precision=hi)
            + jnp.einsum('jit,tgos->gjsio', sel_b, kern[1], precision=hi))
    eye = jnp.eye(q, dtype=F32)[:, None, :, None] * jnp.eye(s, dtype=F32)[None, :, None, :]
    tmat = tmat + eye[None] * d.astype(F32).reshape(g, 1, s, 1, 1)
    tmat = tmat.reshape(g, q * s, q * s)

    def chunk_in(are, aim, bre, bim):
        re = jnp.einsum('jgp,gpi->gjip', are, bre) - jnp.einsum('jgp,gpi->gjip', aim, bim)
        im = jnp.einsum('jgp,gpi->gjip', are, bim) + jnp.einsum('jgp,gpi->gjip', aim, bre)
        return jnp.concatenate([re, im, im, re], axis=-1)

    uh = jnp.concatenate([chunk_in(ar[0, :q][::-1], ai[0, :q][::-1], bbr[0], bbi[0]),
                          chunk_in(ar[1, :q], ai[1, :q], bbr[1], bbi[1])], axis=-1).reshape(g, q * s, 8 * p)

    def state_out(re, im):
        return jnp.concatenate([jnp.transpose(re, (1, 3, 0, 2)), -jnp.transpose(im, (1, 3, 0, 2))], axis=1)

    hy = jnp.concatenate([state_out(car[0, 1:], cai[0, 1:]),
                          state_out(car[1, 1:][::-1], cai[1, 1:][::-1])], axis=1).reshape(g, 4 * p, q * s)
    pm = jnp.stack([jnp.concatenate([ar[0, q], ar[0, q]], -1), jnp.concatenate([-ai[0, q], ai[0, q]], -1),
                    jnp.concatenate([ar[1, q], ar[1, q]], -1), jnp.concatenate([-ai[1, q], ai[1, q]], -1)],
                   axis=1)
    pm = jnp.concatenate([pm, jnp.zeros_like(pm)], axis=1)
    return tmat.astype(BF16), uh.astype(BF16), hy.astype(BF16), pm


def _s5(proj, weights, tc):
    b, tt, _ = proj.shape
    q, g, s = S5_CHUNK, S5_GROUPS, S5_GROUP
    nch = tt // q
    tmat, uh, hy, pm = weights
    r = nch * b
    pw = N_PACK // 128
    x_specs = [pl.BlockSpec((r, 128), functools.partial(lambda gi, i: (0, i * pw + OFF_U // 128 + gi // 8), i=i))
               for i in range(q)]
    y = pl.pallas_call(
        functools.partial(_s5_kernel, nb=b, nch=nch, nctx=tc // q),
        grid=(g,),
        in_specs=x_specs + [pl.BlockSpec((1, q * s, q * s), lambda gi: (gi, 0, 0)),
                            pl.BlockSpec((1, q * s, 512), lambda gi: (gi, 0, 0)),
                            pl.BlockSpec((1, 256, q * s), lambda gi: (gi, 0, 0)),
                            pl.BlockSpec((1, 8, 128), lambda gi: (gi, 0, 0))],
        out_specs=pl.BlockSpec((r * q, 128), lambda gi: (0, gi // 8)),
        out_shape=jax.ShapeDtypeStruct((r * q, BRANCH_W), BF16),
        scratch_shapes=[pltpu.VMEM((q, r, 128), F32), pltpu.VMEM((r * q, 128), F32)]
        + [pltpu.VMEM((r, 128), F32)] * 6,
        compiler_params=_params("arbitrary"),
        name="s5",
    )(*([proj.reshape(r, q * N_PACK)] * q), tmat, uh, hy, pm)
    return y.reshape(b, tt, BRANCH_W)


def _ssd_kernel(*refs, backward, first, nctx_chunks, nchunks):
    if first:
        (xbc_ref, prev_ref, next_ref, z_ref, dt_ref, cw_ref, cb_ref, dtb_ref, alog_ref, dsk_ref,
         sel_ref, o_ref, st_s) = refs
        yin_ref = ng_ref = None
    else:
        (xbc_ref, prev_ref, next_ref, z_ref, dt_ref, cw_ref, cb_ref, dtb_ref, alog_ref, dsk_ref,
         sel_ref, yin_ref, ng_ref, o_ref, st_s) = refs
    qn = SSD_CHUNK
    step = pl.program_id(1)
    if backward:
        c = jnp.where(step < nctx_chunks, nctx_chunks - 1 - step, nchunks - 1 - (step - nctx_chunks))
    else:
        c = step

    @pl.when(step == 0)
    def _():
        st_s[...] = jnp.zeros_like(st_s)

    x = xbc_ref[0].astype(F32)
    has_prev = jnp.logical_and(c != 0, c != nctx_chunks)
    has_next = jnp.logical_and(c != nctx_chunks - 1, c != nchunks - 1)
    prow = jnp.where(has_prev, prev_ref[0, 7:8, :].astype(F32), 0.0)
    nrow = jnp.where(has_next, next_ref[0, 0:1, :].astype(F32), 0.0)
    rid = lax.broadcasted_iota(jnp.int32, (qn, 1), 0)
    xm = jnp.where(rid == 0, prow, pltpu.roll(x, 1, 0))
    xp = jnp.where(rid == qn - 1, nrow, pltpu.roll(x, qn - 1, 0))
    cw = cw_ref[...]
    conv = xm * cw[0:1] + x * cw[1:2] + xp * cw[2:3] + cb_ref[...]
    act = _silu(conv)
    xs = act[:, :BRANCH_W]
    gn = SSD_GROUPS * SSD_STATE
    bm = act[:, BRANCH_W:BRANCH_W + gn].astype(BF16)
    cm = act[:, BRANCH_W + gn:].astype(BF16)

    dt = _softplus(dt_ref[0].astype(F32) + dtb_ref[...])
    a = -jnp.exp(alog_ref[...])
    da = dt * a
    ri = lax.broadcasted_iota(jnp.int32, (qn, qn), 0)
    ci = lax.broadcasted_iota(jnp.int32, (qn, qn), 1)
    causal = (ci >= ri) if backward else (ci <= ri)
    ones_tri = jnp.where(causal, 1.0, 0.0).astype(BF16)
    d1 = da.astype(BF16)
    r1 = da - d1.astype(F32)
    d2 = r1.astype(BF16)
    d3 = (r1 - d2.astype(F32)).astype(BF16)
    cum = (jnp.dot(ones_tri, d1, preferred_element_type=F32)
           + jnp.dot(ones_tri, d2, preferred_element_type=F32)
           + jnp.dot(ones_tri, d3, preferred_element_type=F32))
    edge = cum[0:1] if backward else cum[qn - 1:qn]
    cum_t = cum.T
    dt_t = dt.T
    w_edge = dt * jnp.exp(edge - cum)
    e_in = jnp.exp(cum)

    def expand(v):
        v1 = v.astype(BF16)
        v2 = (v - v1.astype(F32)).astype(BF16)
        return (jnp.dot(v1, sel_ref[...], preferred_element_type=F32)
                + jnp.dot(v2, sel_ref[...], preferred_element_type=F32))

    xw = (xs * expand(w_edge)).astype(BF16)
    e_in_x = expand(e_in)
    xs_b = xs.astype(BF16)
    rpg = SSD_HEADS // SSD_GROUPS
    gw = rpg * SSD_HEAD_DIM
    ys = []
    for g in range(SSD_GROUPS):
        bg = bm[:, g * SSD_STATE:(g + 1) * SSD_STATE]
        cg = cm[:, g * SSD_STATE:(g + 1) * SSD_STATE]
        cb = lax.dot_general(cg, bg, (((1,), (1,)), ((), ())), preferred_element_type=F32)
        st_g = st_s[g * gw:(g + 1) * gw, :]
        y_off = lax.dot_general(cg, st_g.astype(BF16), (((1,), (1,)), ((), ())),
                                preferred_element_type=F32)
        yg = y_off * e_in_x[:, g * gw:(g + 1) * gw]
        parts = []
        for r in range(rpg):
            h = g * rpg + r
            dec = jnp.exp(jnp.where(causal, cum[:, h:h + 1] - cum_t[h:h + 1, :], -jnp.inf))
            wmat = (cb * dec * dt_t[h:h + 1, :]).astype(BF16)
            parts.append(jnp.dot(wmat, xs_b[:, h * SSD_HEAD_DIM:(h + 1) * SSD_HEAD_DIM],
                                 preferred_element_type=F32))
        ys.append(yg + jnp.concatenate(parts, axis=1))
        new = lax.dot_general(xw[:, g * gw:(g + 1) * gw], bg, (((0,), (0,)), ((), ())),
                              preferred_element_type=F32)
        for r in range(rpg):
            h = g * rpg + r
            rows = slice(g * gw + r * SSD_HEAD_DIM, g * gw + (r + 1) * SSD_HEAD_DIM)
            cd = jnp.exp(edge[:, h:h + 1])
            st_s[rows, :] = st_s[rows, :] * cd + new[r * SSD_HEAD_DIM:(r + 1) * SSD_HEAD_DIM, :]
    y = jnp.concatenate(ys, axis=1)
    if first:
        o_ref[0] = y + xs * dsk_ref[...]
    else:
        y = y + yin_ref[0]
        zz = z_ref[0].astype(F32)
        y = y * _silu(zz)
        y = y * lax.rsqrt(jnp.mean(y * y, axis=-1, keepdims=True) + RMS_EPS) * ng_ref[...]
        o_ref[0] = y.astype(o_ref.dtype)


def _ssd_pass(proj, conv_w, conv_b, dt_bias, a_log, d_skip, sel, yin, norm_g, tc, backward):
    b, tt, _ = proj.shape
    qn = SSD_CHUNK
    nchunks = tt // qn
    nctx = tc // qn
    first = yin is None
    last_blk8 = tt // 8 - 1

    def chunk_of(s):
        if backward:
            return jnp.where(s < nctx, nctx - 1 - s, nchunks - 1 - (s - nctx))
        return s

    xbc_blk = OFF_XBC // SSD_CONV_CH
    in_specs = [
        pl.BlockSpec((1, qn, SSD_CONV_CH), lambda bi, s: (bi, chunk_of(s), xbc_blk)),
        pl.BlockSpec((1, 8, SSD_CONV_CH),
                     lambda bi, s: (bi, jnp.maximum(chunk_of(s) * (qn // 8) - 1, 0), xbc_blk)),
        pl.BlockSpec((1, 8, SSD_CONV_CH),
                     lambda bi, s: (bi, jnp.minimum((chunk_of(s) + 1) * (qn // 8), last_blk8), xbc_blk)),
        pl.BlockSpec((1, qn, BRANCH_W), lambda bi, s: (bi, chunk_of(s), OFF_Z // BRANCH_W)),
        pl.BlockSpec((1, qn, 128), lambda bi, s: (bi, chunk_of(s), OFF_DT // 128)),
        pl.BlockSpec((8, SSD_CONV_CH), lambda bi, s: (0, 0)),
        pl.BlockSpec((1, SSD_CONV_CH), lambda bi, s: (0, 0)),
        pl.BlockSpec((1, 128), lambda bi, s: (0, 0)),
        pl.BlockSpec((1, 128), lambda bi, s: (0, 0)),
        pl.BlockSpec((1, BRANCH_W), lambda bi, s: (0, 0)),
        pl.BlockSpec((128, BRANCH_W), lambda bi, s: (0, 0)),
    ]
    args = [proj, proj, proj, proj, proj, conv_w, conv_b, dt_bias, a_log, d_skip, sel]
    if not first:
        in_specs += [pl.BlockSpec((1, qn, BRANCH_W), lambda bi, s: (bi, chunk_of(s), 0)),
                     pl.BlockSpec((1, BRANCH_W), lambda bi, s: (0, 0))]
        args += [yin, norm_g]
    return pl.pallas_call(
        functools.partial(_ssd_kernel, backward=backward, first=first, nctx_chunks=nctx, nchunks=nchunks),
        grid=(b, nchunks),
        in_specs=in_specs,
        out_specs=pl.BlockSpec((1, qn, BRANCH_W), lambda bi, s: (bi, chunk_of(s), 0)),
        out_shape=jax.ShapeDtypeStruct((b, tt, BRANCH_W), F32 if first else BF16),
        scratch_shapes=[pltpu.VMEM((SSD_HEADS * SSD_HEAD_DIM, SSD_STATE), F32)],
        compiler_params=_params("arbitrary", "arbitrary"),
        name="ssd_bwd" if backward else "ssd_fwd",
    )(*args)


def _ssd(proj, conv_w, conv_b, dt_bias, a_log, d_skip, norm_g, tc):
    def lanes(v):
        return jnp.pad(v.astype(F32), (0, 128 - SSD_HEADS)).reshape(1, 128)

    cw = jnp.pad(conv_w.astype(F32), ((0, 5), (0, 0)))
    cb = conv_b.astype(F32).reshape(1, -1)
    dsk = jnp.repeat(d_skip.astype(F32), SSD_HEAD_DIM).reshape(1, -1)
    sel = (jnp.arange(128)[:, None] == (jnp.arange(BRANCH_W)[None, :] // SSD_HEAD_DIM)).astype(BF16)
    y1 = _ssd_pass(proj, cw, cb, lanes(dt_bias[0]), lanes(a_log[0]), dsk, sel, None, None, tc, False)
    return _ssd_pass(proj, cw, cb, lanes(dt_bias[1]), lanes(a_log[1]), jnp.zeros_like(dsk), sel,
                     y1, norm_g.astype(F32).reshape(1, -1), tc, True)


def _merge_kernel(ya_ref, ga_ref, yb_ref, gb_ref, yc_ref, yd_ref, gd_ref, gm_ref, bm_ref,
                  wb_ref, wg_ref, bg_ref, o_ref):
    def f(ref):
        return ref[0].astype(F32)

    g = _gelu_tanh(f(yb_ref))
    glu = g * _sigmoid(jnp.dot(g.astype(BF16), wg_ref[...], preferred_element_type=F32) + bg_ref[...])
    branches = (f(ya_ref) * _silu(f(ga_ref)), glu * _silu(f(gb_ref)), f(yc_ref), f(yd_ref) * _silu(f(gd_ref)))
    acc = None
    for n, br in enumerate(branches):
        gate = _sigmoid(gm_ref[0, :, n * D_MODEL:(n + 1) * D_MODEL].astype(F32) + bm_ref[n:n + 1, :])
        term = gate * jnp.dot(br.astype(BF16), wb_ref[n], preferred_element_type=F32)
        acc = term if acc is None else acc + term
    o_ref[0] = acc.astype(o_ref.dtype)


def _merge(proj, ya, yb, yc, yd, b_merge, w_branch, w_glu, b_glu):
    b, tt, _ = proj.shape
    tm = Q_BLOCK
    bw = BRANCH_W

    def pspec(off):
        return pl.BlockSpec((1, tm, bw), lambda bi, i: (bi, i, off // bw))

    yspec = pl.BlockSpec((1, tm, bw), lambda bi, i: (bi, i, 0))
    once = dict(pipeline_mode=pl.Buffered(1))
    return pl.pallas_call(
        _merge_kernel,
        grid=(b, tt // tm),
        in_specs=[yspec, pspec(OFF_GA), yspec, pspec(OFF_GB), yspec, yspec, pspec(OFF_GD),
                  pl.BlockSpec((1, tm, N_BRANCH * D_MODEL), lambda bi, i: (bi, i, 0)),
                  pl.BlockSpec((N_BRANCH, D_MODEL), lambda bi, i: (0, 0)),
                  pl.BlockSpec((N_BRANCH, bw, D_MODEL), lambda bi, i: (0, 0, 0), **once),
                  pl.BlockSpec((bw, bw), lambda bi, i: (0, 0), **once),
                  pl.BlockSpec((1, bw), lambda bi, i: (0, 0))],
        out_specs=pl.BlockSpec((1, tm, D_MODEL), lambda bi, i: (bi, i, 0)),
        out_shape=jax.ShapeDtypeStruct((b, tt, D_MODEL), BF16),
        compiler_params=_params("arbitrary", "arbitrary"),
        name="merge",
    )(ya, proj, yb, proj, yc, yd, proj, proj, b_merge, w_branch, w_glu, b_glu.reshape(1, -1))


def _out_kernel(m_ref, x_ref, mod_ref, w_ref, lg_ref, lb_ref, o_ref, *, tc, tm, row0):
    i = pl.program_id(1)
    out = jnp.dot(m_ref[0], w_ref[...], preferred_element_type=F32)
    row = row0 + i * tm + lax.broadcasted_iota(jnp.int32, (tm, 1), 0)
    m = mod_ref[0]
    gate = jnp.where(row < tc, m[4:5], m[5:6])
    v = DEEPNORM_ALPHA * x_ref[0] + gate * out
    mu = jnp.mean(v, axis=-1, keepdims=True)
    vc = v - mu
    var = jnp.mean(vc * vc, axis=-1, keepdims=True)
    o_ref[0] = vc * lax.rsqrt(var + LN_EPS) * lg_ref[...] + lb_ref[...]


def _out(merged, xa, mod, w_out, ln_g, ln_b, tc, latent_only):
    b, tt, d = xa.shape
    tm = Q_BLOCK
    skip = tc // tm if latent_only else 0
    nblk = tt // tm - skip
    spec = pl.BlockSpec((1, tm, d), lambda bi, i: (bi, i + skip, 0))
    return pl.pallas_call(
        functools.partial(_out_kernel, tc=tc, tm=tm, row0=skip * tm),
        grid=(b, nblk),
        in_specs=[spec, spec,
                  pl.BlockSpec((1, 8, d), lambda bi, i: (bi, 0, 0)),
                  pl.BlockSpec((d, d), lambda bi, i: (0, 0)),
                  pl.BlockSpec((1, d), lambda bi, i: (0, 0)),
                  pl.BlockSpec((1, d), lambda bi, i: (0, 0))],
        out_specs=pl.BlockSpec((1, tm, d), lambda bi, i: (bi, i, 0)),
        out_shape=jax.ShapeDtypeStruct((b, nblk * tm, d), F32),
        compiler_params=_params("arbitrary", "arbitrary"),
        name="out_proj",
    )(merged, xa, mod, w_out, ln_g.reshape(1, -1), ln_b.reshape(1, -1))


def _half_split(w, heads):
    k = w.shape[0]
    return w.reshape(k, heads, ROPE_DIM // 2, 2).transpose(0, 1, 3, 2).reshape(k, heads * ROPE_DIM)


def _pack_w_in(w):
    sp = np.cumsum([0, MLA_Q_LORA, MLA_KV_LORA, MLA_ROPE, BRANCH_W, BRANCH_W, BRANCH_W, BRANCH_W,
                    SSD_CONV_CH, SSD_HEADS, BRANCH_W, BRANCH_W, BRANCH_W, BRANCH_W, N_BRANCH * D_MODEL])
    seg = [w[:, sp[k]:sp[k + 1]] for k in range(14)]
    cq, ckv, kr, ga, u, gb, z, xbc, dt, qd, kd, vd, gd, gm = seg
    kr_hs = _half_split(kr, 1)
    kr_sw = jnp.concatenate([kr_hs[:, 32:], kr_hs[:, :32]], axis=1)
    pad = jnp.zeros((w.shape[0], 128 - SSD_HEADS), w.dtype)
    packed = jnp.concatenate([gm, xbc, ga, u, gb, z, _half_split(qd, 2 * DIFF_HEADS),
                              _half_split(kd, 2 * DIFF_HEADS), vd, gd, cq, ckv, kr_hs, kr_sw, dt, pad],
                             axis=1)
    return packed.astype(BF16)


def _pack_mla(w_uq, w_ukv):
    k = w_uq.shape[0]
    wq = w_uq.reshape(k, MLA_HEADS, MLA_NOPE + MLA_ROPE)
    rope = wq[:, :, MLA_NOPE:].reshape(k, MLA_HEADS, ROPE_DIM // 2, 2)
    ev, od = rope[..., 0], rope[..., 1]
    wq = jnp.concatenate([wq[:, :, :MLA_NOPE], ev, od, od, ev], axis=-1)
    wkv = w_ukv.reshape(w_ukv.shape[0], MLA_HEADS, MLA_NOPE + MLA_V)
    return (wq.transpose(1, 0, 2).astype(BF16),
            wkv[:, :, :MLA_NOPE].transpose(1, 0, 2).astype(BF16),
            wkv[:, :, MLA_NOPE:].transpose(1, 0, 2).astype(BF16))


def _rope_tables(tl, tc):
    rows = tl // GRID_W
    row_id = jnp.repeat(jnp.arange(rows, dtype=F32), GRID_W)
    col_id = jnp.tile(jnp.arange(GRID_W, dtype=F32), rows)
    quarter = ROPE_DIM // 4
    inv_freq = ROPE_BASE ** (-jnp.arange(quarter, dtype=F32) / quarter)
    ang = jnp.concatenate([row_id[:, None] * inv_freq, col_id[:, None] * inv_freq], axis=-1)
    cos = jnp.concatenate([jnp.ones((tc, ROPE_DIM // 2), F32), jnp.cos(ang)], axis=0)
    sin = jnp.concatenate([jnp.zeros((tc, ROPE_DIM // 2), F32), jnp.sin(ang)], axis=0)
    zero = jnp.zeros_like(cos)
    mla = (jnp.concatenate([cos, cos, zero, zero], axis=1), jnp.concatenate([zero, zero, -sin, sin], axis=1))
    diff = (jnp.concatenate([cos, cos, cos, cos], axis=1), jnp.concatenate([-sin, sin, -sin, sin], axis=1))
    return mla, diff


def _layer(xa, cs, p, tables, layer_idx, tc, last):
    b = xa.shape[0]
    mla_tab, diff_tab = tables
    mod = _ada(cs, p['w_ada_stacked'], p['b_ada'], layer_idx)
    d = D_MODEL
    shift, scale, gate = mod[:, :d], mod[:, d:2 * d], mod[:, 2 * d:]
    zeros = jnp.zeros((b, d), F32)
    modb = jnp.stack([jnp.broadcast_to(scale[0], (b, d)), jnp.broadcast_to(shift[0], (b, d)),
                      scale[1:1 + b], shift[1:1 + b],
                      jnp.broadcast_to(gate[0], (b, d)), gate[1:1 + b], zeros, zeros], axis=1)
    proj = _inproj(xa, modb, _pack_w_in(p['w_in']), tc)
    wq, wk, wv = _pack_mla(p['mla_w_uq'], p['mla_w_ukv'])
    ya = _mla(proj, wq, wk, wv, p['mla_q_norm'], p['mla_kv_norm'], mla_tab[0], mla_tab[1], tc)
    lam_init = 0.8 - 0.6 * math.exp(-0.3 * layer_idx)
    yd = _diff(proj, p['diff_lambda_q'], p['diff_lambda_k'], p['diff_norm'], diff_tab[0], diff_tab[1],
               lam_init, tc)
    yb = _s5(proj, _s5_weights(p['s5_lambda_re'], p['s5_lambda_im'], p['s5_log_dt'], p['s5_b_re'],
                               p['s5_b_im'], p['s5_c_re'], p['s5_c_im'], p['s5_d']), tc)
    yc = _ssd(proj, p['ssd_conv_w'], p['ssd_conv_b'], p['ssd_dt_bias'], p['ssd_a_log'], p['ssd_d'],
              p['ssd_norm'], tc)
    merged = _merge(proj, ya, yb, yc, yd, p['b_merge'], p['w_branch'].astype(BF16),
                    p['s5_w_glu'].astype(BF16), p['s5_b_glu'])
    return _out(merged, xa, modb, p['w_out'].astype(BF16), p['ln_g'], p['ln_b'], tc, last)


def kernel(x, c, ctx, c_ctx, w_ada, b_ada, w_in, mla_q_norm, mla_w_uq, mla_kv_norm, mla_w_ukv,
           s5_lambda_re, s5_lambda_im, s5_log_dt, s5_b_re, s5_b_im, s5_c_re, s5_c_im, s5_d,
           s5_w_glu, s5_b_glu, ssd_conv_w, ssd_conv_b, ssd_dt_bias, ssd_a_log, ssd_d, ssd_norm,
           diff_lambda_q, diff_lambda_k, diff_norm, b_merge, w_branch, w_out, ln_g, ln_b):
    b, tl, d = x.shape
    tc = ctx.shape[1]
    stacked = dict(
        w_ada=w_ada, b_ada=b_ada, w_in=w_in, mla_q_norm=mla_q_norm, mla_w_uq=mla_w_uq,
        mla_kv_norm=mla_kv_norm, mla_w_ukv=mla_w_ukv, s5_lambda_re=s5_lambda_re, s5_lambda_im=s5_lambda_im,
        s5_log_dt=s5_log_dt, s5_b_re=s5_b_re, s5_b_im=s5_b_im, s5_c_re=s5_c_re, s5_c_im=s5_c_im, s5_d=s5_d,
        s5_w_glu=s5_w_glu, s5_b_glu=s5_b_glu, ssd_conv_w=ssd_conv_w, ssd_conv_b=ssd_conv_b,
        ssd_dt_bias=ssd_dt_bias, ssd_a_log=ssd_a_log, ssd_d=ssd_d, ssd_norm=ssd_norm,
        diff_lambda_q=diff_lambda_q, diff_lambda_k=diff_lambda_k, diff_norm=diff_norm, b_merge=b_merge,
        w_branch=w_branch, w_out=w_out, ln_g=ln_g, ln_b=ln_b)
    tables = _rope_tables(tl, tc)
    cs = jnp.concatenate([c_ctx[None], c, jnp.zeros((8 - 1 - b, d), c.dtype)], axis=0)
    xa = jnp.concatenate([ctx, x], axis=1)
    depth = w_in.shape[0]
    for i in range(depth):
        p = {k: v[i] for k, v in stacked.items() if k != 'w_ada'}
        p['w_ada_stacked'] = w_ada
        xa = _layer(xa, cs, p, tables, i, tc, i == depth - 1)
    return xa
```

```python
import functools
import math

import numpy as np
import jax
import jax.numpy as jnp
from jax import lax
from jax.experimental import pallas as pl
from jax.experimental.pallas import tpu as pltpu

F32 = jnp.float32
BF16 = jnp.bfloat16

D_MODEL = 2048
DEPTH = 2
GRID_W = 64
N_BRANCH = 4
BRANCH_W = 1024
ROPE_DIM = 64
ROPE_BASE = 10000.0

MLA_HEADS = 8
MLA_NOPE = 128
MLA_ROPE = ROPE_DIM
MLA_V = 128
MLA_Q_LORA = 512
MLA_KV_LORA = 256

S5_GROUP = 16
S5_GROUPS = BRANCH_W // S5_GROUP
S5_STATE = 64
S5_CHUNK = 16

SSD_HEAD_DIM = 64
SSD_HEADS = BRANCH_W // SSD_HEAD_DIM
SSD_GROUPS = 4
SSD_STATE = 128
SSD_CHUNK = 128
SSD_CONV_CH = BRANCH_W + 2 * SSD_GROUPS * SSD_STATE

DIFF_HEAD_DIM = ROPE_DIM
DIFF_HEADS = BRANCH_W // (2 * DIFF_HEAD_DIM)

LN_EPS = 1e-5
RMS_EPS = 1e-6
DEEPNORM_ALPHA = (2 * DEPTH) ** 0.25

V7X_VMEM_LIMIT_BYTES = 56 * 1024 * 1024
Q_BLOCK = 256
ATTN_Q_BLOCK = 512
LOG2E = math.log2(math.e)

OFF_GM = 0
OFF_XBC = 8192
OFF_GA = 10240
OFF_U = 11264
OFF_GB = 12288
OFF_Z = 13312
OFF_QD = 14336
OFF_KD = 15360
OFF_VD = 16384
OFF_GD = 17408
OFF_CQ = 18432
OFF_CKV = 18944
OFF_KR = 19200
OFF_DT = 19328
N_PACK = 19456
IN_TILE_N = 1024


def _sigmoid(x):
    return 0.5 * (1.0 + jnp.tanh(0.5 * x))


def _silu(x):
    return x * _sigmoid(x)


def _gelu_tanh(x):
    return 0.5 * x * (1.0 + jnp.tanh(math.sqrt(2.0 / math.pi) * (x + 0.044715 * (x * x * x))))


def _softplus(x):
    return jnp.maximum(x, 0.0) + jnp.log(1.0 + jnp.exp(-jnp.abs(x)))


def _params(*sem):
    return pltpu.CompilerParams(dimension_semantics=sem, vmem_limit_bytes=V7X_VMEM_LIMIT_BYTES)


def _ada_kernel(c_ref, w_ref, b_ref, o_ref):
    s = _silu(c_ref[...])
    o_ref[...] = jnp.dot(s.astype(BF16), w_ref[0].astype(BF16),
                         preferred_element_type=F32) + b_ref[...]


def _ada(cs, w_ada, b_ada, layer):
    n = w_ada.shape[2]
    tn = 512
    return pl.pallas_call(
        _ada_kernel,
        grid=(n // tn,),
        in_specs=[pl.BlockSpec((8, D_MODEL), lambda j: (0, 0)),
                  pl.BlockSpec((1, D_MODEL, tn), lambda j: (layer, 0, j)),
                  pl.BlockSpec((1, tn), lambda j: (0, j))],
        out_specs=pl.BlockSpec((8, tn), lambda j: (0, j)),
        out_shape=jax.ShapeDtypeStruct((8, n), F32),
        compiler_params=_params("arbitrary"),
        name="ada",
    )(cs, w_ada, b_ada.reshape(1, n))


def _inproj_kernel(x_ref, mod_ref, w_ref, o_ref, h_ref, *, tc, tm):
    i = pl.program_id(1)

    @pl.when(pl.program_id(2) == 0)
    def _():
        x = x_ref[0]
        mu = jnp.mean(x, axis=-1, keepdims=True)
        xc = x - mu
        var = jnp.mean(xc * xc, axis=-1, keepdims=True)
        xn = xc * lax.rsqrt(var + LN_EPS)
        row = i * tm + lax.broadcasted_iota(jnp.int32, (tm, 1), 0)
        is_ctx = row < tc
        m = mod_ref[0]
        scale = jnp.where(is_ctx, m[0:1], m[2:3])
        shift = jnp.where(is_ctx, m[1:2], m[3:4])
        h_ref[...] = (xn * (1.0 + scale) + shift).astype(BF16)

    o_ref[0] = jnp.dot(h_ref[...], w_ref[...], preferred_element_type=F32).astype(o_ref.dtype)


def _inproj(xa, mod, w_pack, tc):
    b, tt, d = xa.shape
    tm = tt // 4
    tn = IN_TILE_N
    return pl.pallas_call(
        functools.partial(_inproj_kernel, tc=tc, tm=tm),
        grid=(b, tt // tm, N_PACK // tn),
        in_specs=[pl.BlockSpec((1, tm, d), lambda bi, i, j: (bi, i, 0)),
                  pl.BlockSpec((1, 8, d), lambda bi, i, j: (bi, 0, 0)),
                  pl.BlockSpec((d, tn), lambda bi, i, j: (0, j))],
        out_specs=pl.BlockSpec((1, tm, tn), lambda bi, i, j: (bi, i, j)),
        out_shape=jax.ShapeDtypeStruct((b, tt, N_PACK), BF16),
        scratch_shapes=[pltpu.VMEM((tm, d), BF16)],
        compiler_params=_params("arbitrary", "arbitrary", "arbitrary"),
        name="inproj",
    )(xa, mod, w_pack)


def _mla_kernel(cq_ref, ckv_ref, kr_ref, wq_ref, wk_ref, wv_ref, qn_ref, kvn_ref,
                ct_ref, st_ref, ctq_ref, stq_ref, o_ref, k_s, v_s, *, nk, scale):
    @pl.when(pl.program_id(2) == 0)
    def _():
        ckv = ckv_ref[0].astype(F32)
        r = lax.rsqrt(jnp.mean(ckv * ckv, axis=-1, keepdims=True) + RMS_EPS)
        ckvn = (ckv * r * kvn_ref[...]).astype(BF16)
        k_s[:, 0:MLA_NOPE] = jnp.dot(ckvn, wk_ref[0], preferred_element_type=F32).astype(BF16)
        v_s[...] = jnp.dot(ckvn, wv_ref[0], preferred_element_type=F32).astype(BF16)
        kr = kr_ref[0].astype(F32)
        k_s[:, MLA_NOPE:] = (kr * ct_ref[...] + pltpu.roll(kr * st_ref[...], 64, 1)).astype(BF16)

    cq = cq_ref[0].astype(F32)
    r = lax.rsqrt(jnp.mean(cq * cq, axis=-1, keepdims=True) + RMS_EPS)
    cqn = (cq * r * qn_ref[...]).astype(BF16)
    q = jnp.dot(cqn, wq_ref[0], preferred_element_type=F32)
    qh = q[:, MLA_NOPE:]
    qr = qh * ctq_ref[...] + pltpu.roll(qh * stq_ref[...], 64, 1)
    qf = (jnp.concatenate([q[:, :MLA_NOPE], qr], axis=1) * (scale * LOG2E)).astype(BF16)

    def half(lo, hi):
        s = lax.dot_general(qf, k_s[lo:hi, :], (((1,), (1,)), ((), ())), preferred_element_type=F32)
        m = jnp.max(s, axis=1, keepdims=True)
        p = jnp.exp2(s - m)
        return m, jnp.sum(p, axis=1, keepdims=True), jnp.dot(p.astype(BF16), v_s[lo:hi, :],
                                                              preferred_element_type=F32)

    ma, la, oa = half(0, nk // 2)
    mb, lb, ob = half(nk // 2, nk)
    m = jnp.maximum(ma, mb)
    wa, wb = jnp.exp2(ma - m), jnp.exp2(mb - m)
    o_ref[0] = ((oa * wa + ob * wb) * (1.0 / (la * wa + lb * wb))).astype(o_ref.dtype)


def _once(shape, index_map):
    return pl.BlockSpec(shape, index_map, pipeline_mode=pl.Buffered(1))


def _mla(proj, wq, wk, wv, q_norm, kv_norm, ct, st, tc, ctx_out):
    b, tt, _ = proj.shape
    scale = (MLA_NOPE + MLA_ROPE) ** -0.5
    qn, kvn = q_norm.reshape(1, -1), kv_norm.reshape(1, -1)

    def call(cq, cq_blk, ctq, stq, nq, tq, nk, name):
        return pl.pallas_call(
            functools.partial(_mla_kernel, nk=nk, scale=scale),
            grid=(b, MLA_HEADS, nq // tq),
            in_specs=[
                pl.BlockSpec((1, tq, MLA_Q_LORA), lambda bi, h, i: (bi, i, cq_blk)),
                _once((1, nk, MLA_KV_LORA), lambda bi, h, i: (bi, 0, OFF_CKV // MLA_KV_LORA)),
                _once((1, nk, 128), lambda bi, h, i: (bi, 0, OFF_KR // 128)),
                pl.BlockSpec((1, MLA_Q_LORA, 256), lambda bi, h, i: (h, 0, 0)),
                pl.BlockSpec((1, MLA_KV_LORA, MLA_NOPE), lambda bi, h, i: (h, 0, 0)),
                pl.BlockSpec((1, MLA_KV_LORA, MLA_V), lambda bi, h, i: (h, 0, 0)),
                pl.BlockSpec((1, MLA_Q_LORA), lambda bi, h, i: (0, 0)),
                pl.BlockSpec((1, MLA_KV_LORA), lambda bi, h, i: (0, 0)),
                _once((nk, 128), lambda bi, h, i: (0, 0)),
                _once((nk, 128), lambda bi, h, i: (0, 0)),
                pl.BlockSpec((tq, 128), lambda bi, h, i: (i, 0)),
                pl.BlockSpec((tq, 128), lambda bi, h, i: (i, 0)),
            ],
            out_specs=pl.BlockSpec((1, tq, MLA_V), lambda bi, h, i: (bi, i, h)),
            out_shape=jax.ShapeDtypeStruct((b, nq, BRANCH_W), BF16),
            scratch_shapes=[pltpu.VMEM((nk, 256), BF16), pltpu.VMEM((nk, MLA_V), BF16)],
            compiler_params=_params("arbitrary", "arbitrary", "arbitrary"),
            name=name,
        )(cq, proj, proj, wq, wk, wv, qn, kvn, ct, st, ctq, stq)

    cq_lat = proj[:, tc:, OFF_CQ:OFF_CQ + MLA_Q_LORA]
    y_lat = call(cq_lat, 0, ct[tc:], st[tc:], tt - tc, ATTN_Q_BLOCK, tt, "mla_attn")
    if ctx_out:
        y_ctx = call(proj, OFF_CQ // MLA_Q_LORA, ct, st, tc, tc, tc, "mla_attn_ctx")
    else:
        y_ctx = jnp.zeros((b, tc, BRANCH_W), BF16)
    return jnp.concatenate([y_ctx, y_lat], axis=1)


def _diff_kernel(q_ref, k_ref, v_ref, lq_ref, lk_ref, g_ref, ct_ref, st_ref, ctq_ref, stq_ref,
                 o_ref, k_s, *, nk, lam_init):
    lane = lax.broadcasted_iota(jnp.int32, (1, 128), 1)
    first_half = (lane & 32) == 0

    def rope(x, c, s):
        xs = jnp.where(first_half, pltpu.roll(x, 96, 1), pltpu.roll(x, 32, 1))
        return x * c + xs * s

    @pl.when(pl.program_id(2) == 0)
    def _():
        k_s[...] = rope(k_ref[0].astype(F32), ct_ref[...], st_ref[...]).astype(BF16)

    q = rope(q_ref[0].astype(F32), ctq_ref[...], stq_ref[...]) * (DIFF_HEAD_DIM ** -0.5 * LOG2E)
    q1 = jnp.where(lane < 64, q, 0.0).astype(BF16)
    q2 = jnp.where(lane >= 64, q, 0.0).astype(BF16)
    lqk = lq_ref[...] * lk_ref[...]
    lam = (jnp.exp(jnp.sum(lqk[0:1], axis=1, keepdims=True))
           - jnp.exp(jnp.sum(lqk[1:2], axis=1, keepdims=True)) + lam_init)

    def half(qm, lo, hi):
        s = lax.dot_general(qm, k_s[lo:hi, :], (((1,), (1,)), ((), ())), preferred_element_type=F32)
        m = jnp.max(s, axis=1, keepdims=True)
        e = jnp.exp2(s - m)
        return m, jnp.sum(e, axis=1, keepdims=True), jnp.dot(e.astype(BF16), v_ref[0, lo:hi, :],
                                                              preferred_element_type=F32)

    def one_map(qm):
        ma, la, oa = half(qm, 0, nk // 2)
        mb, lb, ob = half(qm, nk // 2, nk)
        m = jnp.maximum(ma, mb)
        wa, wb = jnp.exp2(ma - m), jnp.exp2(mb - m)
        return oa * wa + ob * wb, la * wa + lb * wb

    o1, l1 = one_map(q1)
    o2, l2 = one_map(q2)
    o = o1 * (1.0 / l1) - o2 * (lam / l2)
    y = o * lax.rsqrt(jnp.mean(o * o, axis=-1, keepdims=True) + RMS_EPS)
    o_ref[0] = (y * g_ref[...] * (1.0 - lam_init)).astype(o_ref.dtype)


def _diff(proj, lam_q, lam_k, norm_g, ct, st, lam_init, tc, ctx_out):
    b, tt, _ = proj.shape
    g = norm_g.reshape(1, -1)

    def call(q, q_blk, ctq, stq, nq, tq, nk, name):
        return pl.pallas_call(
            functools.partial(_diff_kernel, nk=nk, lam_init=lam_init),
            grid=(b, DIFF_HEADS, nq // tq),
            in_specs=[
                pl.BlockSpec((1, tq, 128), lambda bi, h, i: (bi, i, q_blk + h)),
                pl.BlockSpec((1, nk, 128), lambda bi, h, i: (bi, 0, OFF_KD // 128 + h)),
                pl.BlockSpec((1, nk, 128), lambda bi, h, i: (bi, 0, OFF_VD // 128 + h)),
                pl.BlockSpec((2, DIFF_HEAD_DIM), lambda bi, h, i: (0, 0)),
                pl.BlockSpec((2, DIFF_HEAD_DIM), lambda bi, h, i: (0, 0)),
                pl.BlockSpec((1, 128), lambda bi, h, i: (0, 0)),
                _once((nk, 128), lambda bi, h, i: (0, 0)),
                _once((nk, 128), lambda bi, h, i: (0, 0)),
                pl.BlockSpec((tq, 128), lambda bi, h, i: (i, 0)),
                pl.BlockSpec((tq, 128), lambda bi, h, i: (i, 0)),
            ],
            out_specs=pl.BlockSpec((1, tq, 128), lambda bi, h, i: (bi, i, h)),
            out_shape=jax.ShapeDtypeStruct((b, nq, BRANCH_W), BF16),
            scratch_shapes=[pltpu.VMEM((nk, 128), BF16)],
            compiler_params=_params("arbitrary", "arbitrary", "arbitrary"),
            name=name,
        )(q, proj, proj, lam_q, lam_k, g, ct, st, ctq, stq)

    q_lat = proj[:, tc:, OFF_QD:OFF_QD + BRANCH_W]
    y_lat = call(q_lat, 0, ct[tc:], st[tc:], tt - tc, ATTN_Q_BLOCK, tt, "diff_attn")
    if ctx_out:
        y_ctx = call(proj, OFF_QD // 128, ct, st, tc, tc, tc, "diff_attn_ctx")
    else:
        y_ctx = jnp.zeros((b, tc, BRANCH_W), BF16)
    return jnp.concatenate([y_ctx, y_lat], axis=1)


def _s5_kernel(x_ref, t_ref, uh_ref, hy_ref, p_ref, o_ref, tok_s, xf_s, of_s, vf_s, vfs_s, vb_s, vbs_s,
               hf_s, hb_s, *, nb, nch, nctx):
    q = S5_CHUNK
    r = nb * nch
    g8 = pl.program_id(0) % 8
    lane_blk = lax.broadcasted_iota(jnp.int32, (1, 128), 1) // S5_GROUP

    @pl.when(g8 == 0)
    def _():
        tok_s[...] = x_ref[...].astype(F32)
        for i in range(q):
            xf_s[i] = tok_s[pl.ds(i, r, stride=q), :]
        of_s[...] = jnp.zeros_like(of_s)

    cols = []
    for k in range(q // 8):
        acc = jnp.zeros((r, 128), F32)
        for i8 in range(8):
            rot = pltpu.roll(xf_s[k * 8 + i8], lax.rem((i8 - g8) * S5_GROUP + 128, 128), 1)
            acc = jnp.where(lane_blk == i8, rot, acc)
        cols.append(acc)
    x = jnp.concatenate(cols, axis=1).astype(BF16)

    yloc = jnp.dot(x, t_ref[0], preferred_element_type=F32)
    v = jnp.dot(x, uh_ref[0], preferred_element_type=F32)
    vf_s[...] = v[:, 0:128]
    vfs_s[...] = v[:, 128:256]
    vb_s[...] = v[:, 256:384]
    vbs_s[...] = v[:, 384:512]
    pm = p_ref[0]
    zero = jnp.zeros((nb, 128), F32)

    def scan(order, v_ref, vs_ref, h_ref, p1, p2):
        h, hs = zero, zero
        for c in order:
            rows = pl.ds(c, nb, stride=nch)
            h_ref[rows, :] = h
            h, hs = h * p1 + hs * p2 + v_ref[rows, :], hs * p1 - h * p2 + vs_ref[rows, :]

    scan(range(nch), vf_s, vfs_s, hf_s, pm[0:1], pm[1:2])
    scan(list(range(nctx - 1, -1, -1)) + list(range(nch - 1, nctx - 1, -1)), vb_s, vbs_s, hb_s,
         pm[2:3], pm[3:4])
    hcat = jnp.concatenate([hf_s[...], hb_s[...]], axis=1).astype(BF16)
    y = yloc + jnp.dot(hcat, hy_ref[0], preferred_element_type=F32)

    for k in range(q // 8):
        yk = y[:, k * 128:(k + 1) * 128]
        for i8 in range(8):
            rot = pltpu.roll(yk, lax.rem((g8 - i8) * S5_GROUP + 128, 128), 1)
            i = k * 8 + i8
            of_s[i] = jnp.where(lane_blk == g8, rot, of_s[i])

    @pl.when(g8 == 7)
    def _():
        for i in range(q):
            tok_s[pl.ds(i, r, stride=q), :] = of_s[i]
        o_ref[...] = tok_s[...].astype(o_ref.dtype)


def _s5_weights(lam_re, lam_im, log_dt, b_re, b_im, c_re, c_im, d):
    q = S5_CHUNK
    g, p, s = S5_GROUPS, S5_STATE, S5_GROUP
    hi = lax.Precision.HIGHEST
    lr, li = lam_re.astype(F32), lam_im.astype(F32)
    dt = jnp.exp(log_dt.astype(F32))[..., None]
    tau = jnp.arange(q + 1, dtype=F32)[None, :, None, None]
    mag = jnp.exp((lr * dt)[:, None] * tau)
    ang = (li * dt)[:, None] * tau
    ar, ai = mag * jnp.cos(ang), mag * jnp.sin(ang)
    den = lr * lr + li * li
    fr = ((ar[:, 1] - 1.0) * lr + ai[:, 1] * li) / den
    fi = (ai[:, 1] * lr - (ar[:, 1] - 1.0) * li) / den
    br, bi = b_re.astype(F32), b_im.astype(F32)
    bbr = fr[..., None] * br - fi[..., None] * bi
    bbi = fr[..., None] * bi + fi[..., None] * br
    cr, ci = c_re.astype(F32)[:, None], c_im.astype(F32)[:, None]
    car = cr * ar[:, :, :, None, :] - ci * ai[:, :, :, None, :]
    cai = cr * ai[:, :, :, None, :] + ci * ar[:, :, :, None, :]
    bbr_t = jnp.swapaxes(bbr, 2, 3)[:, None, :, None]
    bbi_t = jnp.swapaxes(bbi, 2, 3)[:, None, :, None]
    kern = jnp.sum(car[:, :q, :, :, None, :] * bbr_t - cai[:, :q, :, :, None, :] * bbi_t,
                   axis=-1)
    ii = np.arange(q)
    lag = ii[None, :, None] - ii[:, None, None]
    sel_f = jnp.asarray(lag == ii[None, None, :], F32)
    sel_b = jnp.asarray(-lag == ii[None, None, :], F32)
    tmat = (jnp.einsum('jit,tgos->gjsio', sel_f, kern[0], precision=hi)
            + jnp.einsum('jit,tgos->gjsio', sel_b, kern[1], precision=hi))
    eye = jnp.eye(q, dtype=F32)[:, None, :, None] * jnp.eye(s, dtype=F32)[None, :, None, :]
    tmat = tmat + eye[None] * d.astype(F32).reshape(g, 1, s, 1, 1)
    tmat = tmat.reshape(g, q * s, q * s)

    def chunk_in(are, aim, bre, bim):
        re = jnp.einsum('jgp,gpi->gjip', are, bre) - jnp.einsum('jgp,gpi->gjip', aim, bim)
        im = jnp.einsum('jgp,gpi->gjip', are, bim) + jnp.einsum('jgp,gpi->gjip', aim, bre)
        return jnp.concatenate([re, im, im, re], axis=-1)

    uh = jnp.concatenate([chunk_in(ar[0, :q][::-1], ai[0, :q][::-1], bbr[0], bbi[0]),
                          chunk_in(ar[1, :q], ai[1, :q], bbr[1], bbi[1])], axis=-1).reshape(g, q * s, 8 * p)

    def state_out(re, im):
        return jnp.concatenate([jnp.transpose(re, (1, 3, 0, 2)), -jnp.transpose(im, (1, 3, 0, 2))], axis=1)

    hy = jnp.concatenate([state_out(car[0, 1:], cai[0, 1:]),
                          state_out(car[1, 1:][::-1], cai[1, 1:][::-1])], axis=1).reshape(g, 4 * p, q * s)
    pm = jnp.stack([jnp.concatenate([ar[0, q], ar[0, q]], -1), jnp.concatenate([-ai[0, q], ai[0, q]], -1),
                    jnp.concatenate([ar[1, q], ar[1, q]], -1), jnp.concatenate([-ai[1, q], ai[1, q]], -1)],
                   axis=1)
    pm = jnp.concatenate([pm, jnp.zeros_like(pm)], axis=1)
    return tmat.astype(BF16), uh.astype(BF16), hy.astype(BF16), pm


def _s5(proj, weights, tc):
    b, tt, _ = proj.shape
    q, g, s = S5_CHUNK, S5_GROUPS, S5_GROUP
    nch = tt // q
    tmat, uh, hy, pm = weights
    r = nch * b
    rows = b * tt
    y = pl.pallas_call(
        functools.partial(_s5_kernel, nb=b, nch=nch, nctx=tc // q),
        grid=(g,),
        in_specs=[_once((rows, 128), lambda gi: (0, OFF_U // 128 + gi // 8)),
                  pl.BlockSpec((1, q * s, q * s), lambda gi: (gi, 0, 0)),
                  pl.BlockSpec((1, q * s, 512), lambda gi: (gi, 0, 0)),
                  pl.BlockSpec((1, 256, q * s), lambda gi: (gi, 0, 0)),
                  pl.BlockSpec((1, 8, 128), lambda gi: (gi, 0, 0))],
        out_specs=pl.BlockSpec((rows, 128), lambda gi: (0, gi // 8)),
        out_shape=jax.ShapeDtypeStruct((rows, BRANCH_W), BF16),
        scratch_shapes=[pltpu.VMEM((rows, 128), F32), pltpu.VMEM((q, r, 128), F32),
                        pltpu.VMEM((q, r, 128), F32)] + [pltpu.VMEM((r, 128), F32)] * 6,
        compiler_params=_params("arbitrary"),
        name="s5",
    )(proj.reshape(rows, N_PACK), tmat, uh, hy, pm)
    return y.reshape(b, tt, BRANCH_W)


def _ssd_kernel(*refs, backward, first, nctx_chunks, nchunks):
    if first:
        (xbc_ref, prev_ref, next_ref, z_ref, dt_ref, cw_ref, cb_ref, dtb_ref, alog_ref, dsk_ref,
         sel_ref, o_ref, st_s) = refs
        yin_ref = ng_ref = None
    else:
        (xbc_ref, prev_ref, next_ref, z_ref, dt_ref, cw_ref, cb_ref, dtb_ref, alog_ref, dsk_ref,
         sel_ref, yin_ref, ng_ref, o_ref, st_s) = refs
    qn = SSD_CHUNK
    step = pl.program_id(1)
    if backward:
        c = jnp.where(step < nctx_chunks, nctx_chunks - 1 - step, nchunks - 1 - (step - nctx_chunks))
    else:
        c = step

    @pl.when(step == 0)
    def _():
        st_s[...] = jnp.zeros_like(st_s)

    x = xbc_ref[0].astype(F32)
    has_prev = jnp.logical_and(c != 0, c != nctx_chunks)
    has_next = jnp.logical_and(c != nctx_chunks - 1, c != nchunks - 1)
    prow = jnp.where(has_prev, prev_ref[0, 7:8, :].astype(F32), 0.0)
    nrow = jnp.where(has_next, next_ref[0, 0:1, :].astype(F32), 0.0)
    rid = lax.broadcasted_iota(jnp.int32, (qn, 1), 0)
    xm = jnp.where(rid == 0, prow, pltpu.roll(x, 1, 0))
    xp = jnp.where(rid == qn - 1, nrow, pltpu.roll(x, qn - 1, 0))
    cw = cw_ref[...]
    conv = xm * cw[0:1] + x * cw[1:2] + xp * cw[2:3] + cb_ref[...]
    act = _silu(conv)
    xs = act[:, :BRANCH_W]
    gn = SSD_GROUPS * SSD_STATE
    bm = act[:, BRANCH_W:BRANCH_W + gn].astype(BF16)
    cm = act[:, BRANCH_W + gn:].astype(BF16)

    dt = _softplus(dt_ref[0].astype(F32) + dtb_ref[...])
    a = -jnp.exp(alog_ref[...])
    da = dt * a
    ri = lax.broadcasted_iota(jnp.int32, (qn, qn), 0)
    ci = lax.broadcasted_iota(jnp.int32, (qn, qn), 1)
    causal = (ci >= ri) if backward else (ci <= ri)
    ones_tri = jnp.where(causal, 1.0, 0.0).astype(BF16)
    d1 = da.astype(BF16)
    r1 = da - d1.astype(F32)
    d2 = r1.astype(BF16)
    d3 = (r1 - d2.astype(F32)).astype(BF16)
    cum = (jnp.dot(ones_tri, d1, preferred_element_type=F32)
           + jnp.dot(ones_tri, d2, preferred_element_type=F32)
           + jnp.dot(ones_tri, d3, preferred_element_type=F32))
    edge = cum[0:1] if backward else cum[qn - 1:qn]
    cum_t = cum.T
    dt_t = dt.T
    w_edge = dt * jnp.exp(edge - cum)
    e_in = jnp.exp(cum)

    def expand(v):
        v1 = v.astype(BF16)
        v2 = (v - v1.astype(F32)).astype(BF16)
        return (jnp.dot(v1, sel_ref[...], preferred_element_type=F32)
                + jnp.dot(v2, sel_ref[...], preferred_element_type=F32))

    xw = (xs * expand(w_edge)).astype(BF16)
    e_in_x = expand(e_in)
    xs_b = xs.astype(BF16)
    rpg = SSD_HEADS // SSD_GROUPS
    gw = rpg * SSD_HEAD_DIM
    ys = []
    for g in range(SSD_GROUPS):
        bg = bm[:, g * SSD_STATE:(g + 1) * SSD_STATE]
        cg = cm[:, g * SSD_STATE:(g + 1) * SSD_STATE]
        cb = lax.dot_general(cg, bg, (((1,), (1,)), ((), ())), preferred_element_type=F32)
        st_g = st_s[g * gw:(g + 1) * gw, :]
        y_off = lax.dot_general(cg, st_g.astype(BF16), (((1,), (1,)), ((), ())),
                                preferred_element_type=F32)
        yg = y_off * e_in_x[:, g * gw:(g + 1) * gw]
        parts = []
        for r in range(rpg):
            h = g * rpg + r
            dec = jnp.exp(jnp.where(causal, cum[:, h:h + 1] - cum_t[h:h + 1, :], -jnp.inf))
            wmat = (cb * dec * dt_t[h:h + 1, :]).astype(BF16)
            parts.append(jnp.dot(wmat, xs_b[:, h * SSD_HEAD_DIM:(h + 1) * SSD_HEAD_DIM],
                                 preferred_element_type=F32))
        ys.append(yg + jnp.concatenate(parts, axis=1))
        new = lax.dot_general(xw[:, g * gw:(g + 1) * gw], bg, (((0,), (0,)), ((), ())),
                              preferred_element_type=F32)
        for r in range(rpg):
            h = g * rpg + r
            rows = slice(g * gw + r * SSD_HEAD_DIM, g * gw + (r + 1) * SSD_HEAD_DIM)
            cd = jnp.exp(edge[:, h:h + 1])
            st_s[rows, :] = st_s[rows, :] * cd + new[r * SSD_HEAD_DIM:(r + 1) * SSD_HEAD_DIM, :]
    y = jnp.concatenate(ys, axis=1)
    if first:
        o_ref[0] = y + xs * dsk_ref[...]
    else:
        y = y + yin_ref[0]
        zz = z_ref[0].astype(F32)
        y = y * _silu(zz)
        y = y * lax.rsqrt(jnp.mean(y * y, axis=-1, keepdims=True) + RMS_EPS) * ng_ref[...]
        o_ref[0] = y.astype(o_ref.dtype)


def _ssd_pass(proj, conv_w, conv_b, dt_bias, a_log, d_skip, sel, yin, norm_g, tc, backward):
    b, tt, _ = proj.shape
    qn = SSD_CHUNK
    nchunks = tt // qn
    nctx = tc // qn
    first = yin is None
    last_blk8 = tt // 8 - 1

    def chunk_of(s):
        if backward:
            return jnp.where(s < nctx, nctx - 1 - s, nchunks - 1 - (s - nctx))
        return s

    xbc_blk = OFF_XBC // SSD_CONV_CH
    in_specs = [
        pl.BlockSpec((1, qn, SSD_CONV_CH), lambda bi, s: (bi, chunk_of(s), xbc_blk)),
        pl.BlockSpec((1, 8, SSD_CONV_CH),
                     lambda bi, s: (bi, jnp.maximum(chunk_of(s) * (qn // 8) - 1, 0), xbc_blk)),
        pl.BlockSpec((1, 8, SSD_CONV_CH),
                     lambda bi, s: (bi, jnp.minimum((chunk_of(s) + 1) * (qn // 8), last_blk8), xbc_blk)),
        pl.BlockSpec((1, qn, BRANCH_W), lambda bi, s: (bi, chunk_of(s), OFF_Z // BRANCH_W)),
        pl.BlockSpec((1, qn, 128), lambda bi, s: (bi, chunk_of(s), OFF_DT // 128)),
        pl.BlockSpec((8, SSD_CONV_CH), lambda bi, s: (0, 0)),
        pl.BlockSpec((1, SSD_CONV_CH), lambda bi, s: (0, 0)),
        pl.BlockSpec((1, 128), lambda bi, s: (0, 0)),
        pl.BlockSpec((1, 128), lambda bi, s: (0, 0)),
        pl.BlockSpec((1, BRANCH_W), lambda bi, s: (0, 0)),
        pl.BlockSpec((128, BRANCH_W), lambda bi, s: (0, 0)),
    ]
    args = [proj, proj, proj, proj, proj, conv_w, conv_b, dt_bias, a_log, d_skip, sel]
    if not first:
        in_specs += [pl.BlockSpec((1, qn, BRANCH_W), lambda bi, s: (bi, chunk_of(s), 0)),
                     pl.BlockSpec((1, BRANCH_W), lambda bi, s: (0, 0))]
        args += [yin, norm_g]
    return pl.pallas_call(
        functools.partial(_ssd_kernel, backward=backward, first=first, nctx_chunks=nctx, nchunks=nchunks),
        grid=(b, nchunks),
        in_specs=in_specs,
        out_specs=pl.BlockSpec((1, qn, BRANCH_W), lambda bi, s: (bi, chunk_of(s), 0)),
        out_shape=jax.ShapeDtypeStruct((b, tt, BRANCH_W), F32 if first else BF16),
        scratch_shapes=[pltpu.VMEM((SSD_HEADS * SSD_HEAD_DIM, SSD_STATE), F32)],
        compiler_params=_params("arbitrary", "arbitrary"),
        name="ssd_bwd" if backward else "ssd_fwd",
    )(*args)


def _ssd(proj, conv_w, conv_b, dt_bias, a_log, d_skip, norm_g, tc):
    def lanes(v):
        return jnp.pad(v.astype(F32), (0, 128 - SSD_HEADS)).reshape(1, 128)

    cw = jnp.pad(conv_w.astype(F32), ((0, 5), (0, 0)))
    cb = conv_b.astype(F32).reshape(1, -1)
    dsk = jnp.repeat(d_skip.astype(F32), SSD_HEAD_DIM).reshape(1, -1)
    sel = (jnp.arange(128)[:, None] == (jnp.arange(BRANCH_W)[None, :] // SSD_HEAD_DIM)).astype(BF16)
    y1 = _ssd_pass(proj, cw, cb, lanes(dt_bias[0]), lanes(a_log[0]), dsk, sel, None, None, tc, False)
    return _ssd_pass(proj, cw, cb, lanes(dt_bias[1]), lanes(a_log[1]), jnp.zeros_like(dsk), sel,
                     y1, norm_g.astype(F32).reshape(1, -1), tc, True)


def _merge_kernel(ya_ref, ga_ref, yb_ref, gb_ref, yc_ref, yd_ref, gd_ref, gm_ref, bm_ref,
                  wb_ref, wg_ref, bg_ref, o_ref):
    def f(ref):
        return ref[0].astype(F32)

    g = _gelu_tanh(f(yb_ref))
    glu = g * _sigmoid(jnp.dot(g.astype(BF16), wg_ref[...], preferred_element_type=F32) + bg_ref[...])
    branches = (f(ya_ref) * _silu(f(ga_ref)), glu * _silu(f(gb_ref)), f(yc_ref), f(yd_ref) * _silu(f(gd_ref)))
    acc = None
    for n, br in enumerate(branches):
        gate = _sigmoid(gm_ref[0, :, n * D_MODEL:(n + 1) * D_MODEL].astype(F32) + bm_ref[n:n + 1, :])
        term = gate * jnp.dot(br.astype(BF16), wb_ref[n], preferred_element_type=F32)
        acc = term if acc is None else acc + term
    o_ref[0] = acc.astype(o_ref.dtype)


def _merge(proj, ya, yb, yc, yd, b_merge, w_branch, w_glu, b_glu):
    b, tt, _ = proj.shape
    tm = Q_BLOCK
    bw = BRANCH_W

    def pspec(off):
        return pl.BlockSpec((1, tm, bw), lambda bi, i: (bi, i, off // bw))

    yspec = pl.BlockSpec((1, tm, bw), lambda bi, i: (bi, i, 0))
    once = dict(pipeline_mode=pl.Buffered(1))
    return pl.pallas_call(
        _merge_kernel,
        grid=(b, tt // tm),
        in_specs=[yspec, pspec(OFF_GA), yspec, pspec(OFF_GB), yspec, yspec, pspec(OFF_GD),
                  pl.BlockSpec((1, tm, N_BRANCH * D_MODEL), lambda bi, i: (bi, i, 0)),
                  pl.BlockSpec((N_BRANCH, D_MODEL), lambda bi, i: (0, 0)),
                  pl.BlockSpec((N_BRANCH, bw, D_MODEL), lambda bi, i: (0, 0, 0), **once),
                  pl.BlockSpec((bw, bw), lambda bi, i: (0, 0), **once),
                  pl.BlockSpec((1, bw), lambda bi, i: (0, 0))],
        out_specs=pl.BlockSpec((1, tm, D_MODEL), lambda bi, i: (bi, i, 0)),
        out_shape=jax.ShapeDtypeStruct((b, tt, D_MODEL), BF16),
        compiler_params=_params("arbitrary", "arbitrary"),
        name="merge",
    )(ya, proj, yb, proj, yc, yd, proj, proj, b_merge, w_branch, w_glu, b_glu.reshape(1, -1))


def _out_kernel(m_ref, x_ref, mod_ref, w_ref, lg_ref, lb_ref, o_ref, *, tc, tm, row0):
    i = pl.program_id(1)
    out = jnp.dot(m_ref[0], w_ref[...], preferred_element_type=F32)
    row = row0 + i * tm + lax.broadcasted_iota(jnp.int32, (tm, 1), 0)
    m = mod_ref[0]
    gate = jnp.where(row < tc, m[4:5], m[5:6])
    v = DEEPNORM_ALPHA * x_ref[0] + gate * out
    mu = jnp.mean(v, axis=-1, keepdims=True)
    vc = v - mu
    var = jnp.mean(vc * vc, axis=-1, keepdims=True)
    o_ref[0] = vc * lax.rsqrt(var + LN_EPS) * lg_ref[...] + lb_ref[...]


def _out(merged, xa, mod, w_out, ln_g, ln_b, tc, latent_only):
    b, tt, d = xa.shape
    tm = Q_BLOCK
    skip = tc // tm if latent_only else 0
    nblk = tt // tm - skip
    spec = pl.BlockSpec((1, tm, d), lambda bi, i: (bi, i + skip, 0))
    return pl.pallas_call(
        functools.partial(_out_kernel, tc=tc, tm=tm, row0=skip * tm),
        grid=(b, nblk),
        in_specs=[spec, spec,
                  pl.BlockSpec((1, 8, d), lambda bi, i: (bi, 0, 0)),
                  pl.BlockSpec((d, d), lambda bi, i: (0, 0)),
                  pl.BlockSpec((1, d), lambda bi, i: (0, 0)),
                  pl.BlockSpec((1, d), lambda bi, i: (0, 0))],
        out_specs=pl.BlockSpec((1, tm, d), lambda bi, i: (bi, i, 0)),
        out_shape=jax.ShapeDtypeStruct((b, nblk * tm, d), F32),
        compiler_params=_params("arbitrary", "arbitrary"),
        name="out_proj",
    )(merged, xa, mod, w_out, ln_g.reshape(1, -1), ln_b.reshape(1, -1))


def _half_split(w, heads):
    k = w.shape[0]
    return w.reshape(k, heads, ROPE_DIM // 2, 2).transpose(0, 1, 3, 2).reshape(k, heads * ROPE_DIM)


def _pack_w_in(w):
    sp = np.cumsum([0, MLA_Q_LORA, MLA_KV_LORA, MLA_ROPE, BRANCH_W, BRANCH_W, BRANCH_W, BRANCH_W,
                    SSD_CONV_CH, SSD_HEADS, BRANCH_W, BRANCH_W, BRANCH_W, BRANCH_W, N_BRANCH * D_MODEL])
    seg = [w[:, sp[k]:sp[k + 1]] for k in range(14)]
    cq, ckv, kr, ga, u, gb, z, xbc, dt, qd, kd, vd, gd, gm = seg
    kr_hs = _half_split(kr, 1)
    kr_sw = jnp.concatenate([kr_hs[:, 32:], kr_hs[:, :32]], axis=1)
    pad = jnp.zeros((w.shape[0], 128 - SSD_HEADS), w.dtype)
    packed = jnp.concatenate([gm, xbc, ga, u, gb, z, _half_split(qd, 2 * DIFF_HEADS),
                              _half_split(kd, 2 * DIFF_HEADS), vd, gd, cq, ckv, kr_hs, kr_sw, dt, pad],
                             axis=1)
    return packed.astype(BF16)


def _pack_mla(w_uq, w_ukv):
    k = w_uq.shape[0]
    wq = w_uq.reshape(k, MLA_HEADS, MLA_NOPE + MLA_ROPE)
    rope = wq[:, :, MLA_NOPE:].reshape(k, MLA_HEADS, ROPE_DIM // 2, 2)
    ev, od = rope[..., 0], rope[..., 1]
    wq = jnp.concatenate([wq[:, :, :MLA_NOPE], ev, od, od, ev], axis=-1)
    wkv = w_ukv.reshape(w_ukv.shape[0], MLA_HEADS, MLA_NOPE + MLA_V)
    return (wq.transpose(1, 0, 2).astype(BF16),
            wkv[:, :, :MLA_NOPE].transpose(1, 0, 2).astype(BF16),
            wkv[:, :, MLA_NOPE:].transpose(1, 0, 2).astype(BF16))


def _rope_tables(tl, tc):
    rows = tl // GRID_W
    row_id = jnp.repeat(jnp.arange(rows, dtype=F32), GRID_W)
    col_id = jnp.tile(jnp.arange(GRID_W, dtype=F32), rows)
    quarter = ROPE_DIM // 4
    inv_freq = ROPE_BASE ** (-jnp.arange(quarter, dtype=F32) / quarter)
    ang = jnp.concatenate([row_id[:, None] * inv_freq, col_id[:, None] * inv_freq], axis=-1)
    cos = jnp.concatenate([jnp.ones((tc, ROPE_DIM // 2), F32), jnp.cos(ang)], axis=0)
    sin = jnp.concatenate([jnp.zeros((tc, ROPE_DIM // 2), F32), jnp.sin(ang)], axis=0)
    zero = jnp.zeros_like(cos)
    mla = (jnp.concatenate([cos, cos, zero, zero], axis=1), jnp.concatenate([zero, zero, -sin, sin], axis=1))
    diff = (jnp.concatenate([cos, cos, cos, cos], axis=1), jnp.concatenate([-sin, sin, -sin, sin], axis=1))
    return mla, diff


def _layer(xa, cs, p, tables, layer_idx, tc, last):
    b = xa.shape[0]
    mla_tab, diff_tab = tables
    mod = _ada(cs, p['w_ada_stacked'], p['b_ada'], layer_idx)
    d = D_MODEL
    shift, scale, gate = mod[:, :d], mod[:, d:2 * d], mod[:, 2 * d:]
    zeros = jnp.zeros((b, d), F32)
    modb = jnp.stack([jnp.broadcast_to(scale[0], (b, d)), jnp.broadcast_to(shift[0], (b, d)),
                      scale[1:1 + b], shift[1:1 + b],
                      jnp.broadcast_to(gate[0], (b, d)), gate[1:1 + b], zeros, zeros], axis=1)
    proj = _inproj(xa, modb, _pack_w_in(p['w_in']), tc)
    wq, wk, wv = _pack_mla(p['mla_w_uq'], p['mla_w_ukv'])
    ya = _mla(proj, wq, wk, wv, p['mla_q_norm'], p['mla_kv_norm'], mla_tab[0], mla_tab[1], tc, not last)
    lam_init = 0.8 - 0.6 * math.exp(-0.3 * layer_idx)
    yd = _diff(proj, p['diff_lambda_q'], p['diff_lambda_k'], p['diff_norm'], diff_tab[0], diff_tab[1],
               lam_init, tc, not last)
    yb = _s5(proj, _s5_weights(p['s5_lambda_re'], p['s5_lambda_im'], p['s5_log_dt'], p['s5_b_re'],
                               p['s5_b_im'], p['s5_c_re'], p['s5_c_im'], p['s5_d']), tc)
    yc = _ssd(proj, p['ssd_conv_w'], p['ssd_conv_b'], p['ssd_dt_bias'], p['ssd_a_log'], p['ssd_d'],
              p['ssd_norm'], tc)
    merged = _merge(proj, ya, yb, yc, yd, p['b_merge'], p['w_branch'].astype(BF16),
                    p['s5_w_glu'].astype(BF16), p['s5_b_glu'])
    return _out(merged, xa, modb, p['w_out'].astype(BF16), p['ln_g'], p['ln_b'], tc, last)


def kernel(x, c, ctx, c_ctx, w_ada, b_ada, w_in, mla_q_norm, mla_w_uq, mla_kv_norm, mla_w_ukv,
           s5_lambda_re, s5_lambda_im, s5_log_dt, s5_b_re, s5_b_im, s5_c_re, s5_c_im, s5_d,
           s5_w_glu, s5_b_glu, ssd_conv_w, ssd_conv_b, ssd_dt_bias, ssd_a_log, ssd_d, ssd_norm,
           diff_lambda_q, diff_lambda_k, diff_norm, b_merge, w_branch, w_out, ln_g, ln_b):
    b, tl, d = x.shape
    tc = ctx.shape[1]
    stacked = dict(
        w_ada=w_ada, b_ada=b_ada, w_in=w_in, mla_q_norm=mla_q_norm, mla_w_uq=mla_w_uq,
        mla_kv_norm=mla_kv_norm, mla_w_ukv=mla_w_ukv, s5_lambda_re=s5_lambda_re, s5_lambda_im=s5_lambda_im,
        s5_log_dt=s5_log_dt, s5_b_re=s5_b_re, s5_b_im=s5_b_im, s5_c_re=s5_c_re, s5_c_im=s5_c_im, s5_d=s5_d,
        s5_w_glu=s5_w_glu, s5_b_glu=s5_b_glu, ssd_conv_w=ssd_conv_w, ssd_conv_b=ssd_conv_b,
        ssd_dt_bias=ssd_dt_bias, ssd_a_log=ssd_a_log, ssd_d=ssd_d, ssd_norm=ssd_norm,
        diff_lambda_q=diff_lambda_q, diff_lambda_k=diff_lambda_k, diff_norm=diff_norm, b_merge=b_merge,
        w_branch=w_branch, w_out=w_out, ln_g=ln_g, ln_b=ln_b)
    tables = _rope_tables(tl, tc)
    cs = jnp.concatenate([c_ctx[None], c, jnp.zeros((8 - 1 - b, d), c.dtype)], axis=0)
    xa = jnp.concatenate([ctx, x], axis=1)
    depth = w_in.shape[0]
    for i in range(depth):
        p = {k: v[i] for k, v in stacked.items() if k != 'w_ada'}
        p['w_ada_stacked'] = w_ada
        xa = _layer(xa, cs, p, tables, i, tc, i == depth - 1)
    return xa
```

```python
import functools
import math

import numpy as np
import jax
import jax.numpy as jnp
from jax import lax
from jax.experimental import pallas as pl
from jax.experimental.pallas import tpu as pltpu

F32 = jnp.float32
BF16 = jnp.bfloat16

D_MODEL = 2048
DEPTH = 2
GRID_W = 64
N_BRANCH = 4
BRANCH_W = 1024
ROPE_DIM = 64
ROPE_BASE = 10000.0

MLA_HEADS = 8
MLA_NOPE = 128
MLA_ROPE = ROPE_DIM
MLA_V = 128
MLA_Q_LORA = 512
MLA_KV_LORA = 256

S5_GROUP = 16
S5_GROUPS = BRANCH_W // S5_GROUP
S5_STATE = 64
S5_CHUNK = 16

SSD_HEAD_DIM = 64
SSD_HEADS = BRANCH_W // SSD_HEAD_DIM
SSD_GROUPS = 4
SSD_STATE = 128
SSD_CHUNK = 128
SSD_CONV_CH = BRANCH_W + 2 * SSD_GROUPS * SSD_STATE

DIFF_HEAD_DIM = ROPE_DIM
DIFF_HEADS = BRANCH_W // (2 * DIFF_HEAD_DIM)

LN_EPS = 1e-5
RMS_EPS = 1e-6
DEEPNORM_ALPHA = (2 * DEPTH) ** 0.25

V7X_VMEM_LIMIT_BYTES = 56 * 1024 * 1024
Q_BLOCK = 256
ATTN_HEADS_PER_STEP = 2
LOG2E = math.log2(math.e)

OFF_GM = 0
OFF_XBC = 8192
OFF_GA = 10240
OFF_U = 11264
OFF_GB = 12288
OFF_Z = 13312
OFF_QD = 14336
OFF_KD = 15360
OFF_VD = 16384
OFF_GD = 17408
OFF_CQ = 18432
OFF_CKV = 18944
OFF_KR = 19200
OFF_DT = 19328
N_PACK = 19456
IN_TILE_N = 1024


def _sigmoid(x):
    return 0.5 * (1.0 + jnp.tanh(0.5 * x))


def _silu(x):
    return x * _sigmoid(x)


def _gelu_tanh(x):
    return 0.5 * x * (1.0 + jnp.tanh(math.sqrt(2.0 / math.pi) * (x + 0.044715 * (x * x * x))))


def _softplus(x):
    return jnp.maximum(x, 0.0) + jnp.log(1.0 + jnp.exp(-jnp.abs(x)))


def _params(*sem):
    return pltpu.CompilerParams(dimension_semantics=sem, vmem_limit_bytes=V7X_VMEM_LIMIT_BYTES)


def _ada_kernel(c_ref, w_ref, b_ref, o_ref):
    s = _silu(c_ref[...])
    o_ref[...] = jnp.dot(s.astype(BF16), w_ref[0].astype(BF16),
                         preferred_element_type=F32) + b_ref[...]


def _ada(cs, w_ada, b_ada, layer):
    n = w_ada.shape[2]
    tn = 512
    return pl.pallas_call(
        _ada_kernel,
        grid=(n // tn,),
        in_specs=[pl.BlockSpec((8, D_MODEL), lambda j: (0, 0)),
                  pl.BlockSpec((1, D_MODEL, tn), lambda j: (layer, 0, j)),
                  pl.BlockSpec((1, tn), lambda j: (0, j))],
        out_specs=pl.BlockSpec((8, tn), lambda j: (0, j)),
        out_shape=jax.ShapeDtypeStruct((8, n), F32),
        compiler_params=_params("arbitrary"),
        name="ada",
    )(cs, w_ada, b_ada.reshape(1, n))


def _inproj_kernel(x_ref, mod_ref, w_ref, o_ref, h_ref, *, tc, tm):
    i = pl.program_id(1)

    @pl.when(pl.program_id(2) == 0)
    def _():
        x = x_ref[0]
        mu = jnp.mean(x, axis=-1, keepdims=True)
        xc = x - mu
        var = jnp.mean(xc * xc, axis=-1, keepdims=True)
        xn = xc * lax.rsqrt(var + LN_EPS)
        row = i * tm + lax.broadcasted_iota(jnp.int32, (tm, 1), 0)
        is_ctx = row < tc
        m = mod_ref[0]
        scale = jnp.where(is_ctx, m[0:1], m[2:3])
        shift = jnp.where(is_ctx, m[1:2], m[3:4])
        h_ref[...] = (xn * (1.0 + scale) + shift).astype(BF16)

    o_ref[0] = jnp.dot(h_ref[...], w_ref[...], preferred_element_type=F32).astype(o_ref.dtype)


def _inproj(xa, mod, w_pack, tc):
    b, tt, d = xa.shape
    tm = tt // 4
    tn = IN_TILE_N
    return pl.pallas_call(
        functools.partial(_inproj_kernel, tc=tc, tm=tm),
        grid=(b, tt // tm, N_PACK // tn),
        in_specs=[pl.BlockSpec((1, tm, d), lambda bi, i, j: (bi, i, 0)),
                  pl.BlockSpec((1, 8, d), lambda bi, i, j: (bi, 0, 0)),
                  pl.BlockSpec((d, tn), lambda bi, i, j: (0, j))],
        out_specs=pl.BlockSpec((1, tm, tn), lambda bi, i, j: (bi, i, j)),
        out_shape=jax.ShapeDtypeStruct((b, tt, N_PACK), BF16),
        scratch_shapes=[pltpu.VMEM((tm, d), BF16)],
        compiler_params=_params("arbitrary", "arbitrary", "arbitrary"),
        name="inproj",
    )(xa, mod, w_pack)


def _mla_kernel(cq_ref, ckv_ref, kr_ref, wq_ref, wk_ref, wv_ref, qn_ref, kvn_ref,
                ct_ref, st_ref, ctq_ref, stq_ref, o_ref, k_s, v_s, *, tc, tt, scale):
    i = pl.program_id(2)
    hps = ATTN_HEADS_PER_STEP

    @pl.when(i == 0)
    def _():
        ckv = ckv_ref[0].astype(F32)
        r = lax.rsqrt(jnp.mean(ckv * ckv, axis=-1, keepdims=True) + RMS_EPS)
        ckvn = (ckv * r * kvn_ref[...]).astype(BF16)
        kr = kr_ref[0].astype(F32)
        kroped = (kr * ct_ref[...] + pltpu.roll(kr * st_ref[...], 64, 1)).astype(BF16)
        for hh in range(hps):
            k_s[hh, :, 0:MLA_NOPE] = jnp.dot(ckvn, wk_ref[hh], preferred_element_type=F32).astype(BF16)
            k_s[hh, :, MLA_NOPE:] = kroped
            v_s[hh] = jnp.dot(ckvn, wv_ref[hh], preferred_element_type=F32).astype(BF16)

    cq = cq_ref[0].astype(F32)
    r = lax.rsqrt(jnp.mean(cq * cq, axis=-1, keepdims=True) + RMS_EPS)
    cqn = (cq * r * qn_ref[...]).astype(BF16)
    qfs = []
    for hh in range(hps):
        q = jnp.dot(cqn, wq_ref[hh], preferred_element_type=F32)
        qh = q[:, MLA_NOPE:]
        qr = qh * ctq_ref[...] + pltpu.roll(qh * stq_ref[...], 64, 1)
        qfs.append((jnp.concatenate([q[:, :MLA_NOPE], qr], axis=1) * (scale * LOG2E)).astype(BF16))

    def attend(nk):
        outs = []
        for hh in range(hps):
            s = lax.dot_general(qfs[hh], k_s[hh, 0:nk, :], (((1,), (1,)), ((), ())),
                                preferred_element_type=F32)
            p = jnp.exp2(s - jnp.max(s, axis=1, keepdims=True))
            l = jnp.sum(p, axis=1, keepdims=True)
            o = jnp.dot(p.astype(BF16), v_s[hh, 0:nk, :], preferred_element_type=F32)
            outs.append(o * (1.0 / l))
        o_ref[0] = jnp.concatenate(outs, axis=1).astype(o_ref.dtype)

    @pl.when(i == 0)
    def _():
        attend(tc)

    @pl.when(i > 0)
    def _():
        attend(tt)


def _once(shape, index_map):
    return pl.BlockSpec(shape, index_map, pipeline_mode=pl.Buffered(1))


def _mla(proj, wq, wk, wv, q_norm, kv_norm, ct, st, tc):
    b, tt, _ = proj.shape
    tq = Q_BLOCK
    assert tc == tq
    scale = (MLA_NOPE + MLA_ROPE) ** -0.5
    hps = ATTN_HEADS_PER_STEP
    return pl.pallas_call(
        functools.partial(_mla_kernel, tc=tc, tt=tt, scale=scale),
        grid=(b, MLA_HEADS // hps, tt // tq),
        in_specs=[
            pl.BlockSpec((1, tq, MLA_Q_LORA), lambda bi, h, i: (bi, i, OFF_CQ // MLA_Q_LORA)),
            _once((1, tt, MLA_KV_LORA), lambda bi, h, i: (bi, 0, OFF_CKV // MLA_KV_LORA)),
            _once((1, tt, 128), lambda bi, h, i: (bi, 0, OFF_KR // 128)),
            pl.BlockSpec((hps, MLA_Q_LORA, 256), lambda bi, h, i: (h, 0, 0)),
            pl.BlockSpec((hps, MLA_KV_LORA, MLA_NOPE), lambda bi, h, i: (h, 0, 0)),
            pl.BlockSpec((hps, MLA_KV_LORA, MLA_V), lambda bi, h, i: (h, 0, 0)),
            pl.BlockSpec((1, MLA_Q_LORA), lambda bi, h, i: (0, 0)),
            pl.BlockSpec((1, MLA_KV_LORA), lambda bi, h, i: (0, 0)),
            _once((tt, 128), lambda bi, h, i: (0, 0)),
            _once((tt, 128), lambda bi, h, i: (0, 0)),
            pl.BlockSpec((tq, 128), lambda bi, h, i: (i, 0)),
            pl.BlockSpec((tq, 128), lambda bi, h, i: (i, 0)),
        ],
        out_specs=pl.BlockSpec((1, tq, hps * MLA_V), lambda bi, h, i: (bi, i, h)),
        out_shape=jax.ShapeDtypeStruct((b, tt, BRANCH_W), BF16),
        scratch_shapes=[pltpu.VMEM((hps, tt, 256), BF16), pltpu.VMEM((hps, tt, MLA_V), BF16)],
        compiler_params=_params("arbitrary", "arbitrary", "arbitrary"),
        name="mla_attn",
    )(proj, proj, proj, wq, wk, wv, q_norm.reshape(1, -1), kv_norm.reshape(1, -1), ct, st, ct, st)


def _diff_kernel(q_ref, k_ref, v_ref, lq_ref, lk_ref, g_ref, ct_ref, st_ref, ctq_ref, stq_ref,
                 o_ref, k_s, *, tc, tt, lam_init):
    i = pl.program_id(2)
    hps = ATTN_HEADS_PER_STEP
    lane = lax.broadcasted_iota(jnp.int32, (1, 128), 1)
    first_half = (lane & 32) == 0

    def rope(x, c, s):
        xs = jnp.where(first_half, pltpu.roll(x, 96, 1), pltpu.roll(x, 32, 1))
        return x * c + xs * s

    @pl.when(i == 0)
    def _():
        for hh in range(hps):
            k_s[hh] = rope(k_ref[0, :, hh * 128:(hh + 1) * 128].astype(F32), ct_ref[...],
                           st_ref[...]).astype(BF16)

    qs = []
    for hh in range(hps):
        q = rope(q_ref[0, :, hh * 128:(hh + 1) * 128].astype(F32), ctq_ref[...], stq_ref[...]) * (
            DIFF_HEAD_DIM ** -0.5 * LOG2E)
        qs.append((jnp.where(lane < 64, q, 0.0).astype(BF16), jnp.where(lane >= 64, q, 0.0).astype(BF16)))
    lqk = lq_ref[...] * lk_ref[...]
    lam = (jnp.exp(jnp.sum(lqk[0:1], axis=1, keepdims=True))
           - jnp.exp(jnp.sum(lqk[1:2], axis=1, keepdims=True)) + lam_init)

    def attend(nk):
        nt = (((1,), (1,)), ((), ()))
        outs = []
        for hh in range(hps):
            kk = k_s[hh, 0:nk, :]
            s1 = lax.dot_general(qs[hh][0], kk, nt, preferred_element_type=F32)
            s2 = lax.dot_general(qs[hh][1], kk, nt, preferred_element_type=F32)
            e1 = jnp.exp2(s1 - jnp.max(s1, axis=1, keepdims=True))
            e2 = jnp.exp2(s2 - jnp.max(s2, axis=1, keepdims=True))
            l1 = jnp.sum(e1, axis=1, keepdims=True)
            l2 = jnp.sum(e2, axis=1, keepdims=True)
            w = e1 - e2 * (lam * l1 / l2)
            o = jnp.dot(w.astype(BF16), v_ref[0, 0:nk, hh * 128:(hh + 1) * 128],
                        preferred_element_type=F32) * (1.0 / l1)
            y = o * lax.rsqrt(jnp.mean(o * o, axis=-1, keepdims=True) + RMS_EPS)
            outs.append(y * g_ref[...] * (1.0 - lam_init))
        o_ref[0] = jnp.concatenate(outs, axis=1).astype(o_ref.dtype)

    @pl.when(i == 0)
    def _():
        attend(tc)

    @pl.when(i > 0)
    def _():
        attend(tt)


def _diff(proj, lam_q, lam_k, norm_g, ct, st, lam_init, tc):
    b, tt, _ = proj.shape
    tq = Q_BLOCK
    assert tc == tq
    hps = ATTN_HEADS_PER_STEP
    hw = 128 * hps
    return pl.pallas_call(
        functools.partial(_diff_kernel, tc=tc, tt=tt, lam_init=lam_init),
        grid=(b, DIFF_HEADS // hps, tt // tq),
        in_specs=[
            pl.BlockSpec((1, tq, hw), lambda bi, h, i: (bi, i, OFF_QD // hw + h)),
            pl.BlockSpec((1, tt, hw), lambda bi, h, i: (bi, 0, OFF_KD // hw + h)),
            pl.BlockSpec((1, tt, hw), lambda bi, h, i: (bi, 0, OFF_VD // hw + h)),
            pl.BlockSpec((2, DIFF_HEAD_DIM), lambda bi, h, i: (0, 0)),
            pl.BlockSpec((2, DIFF_HEAD_DIM), lambda bi, h, i: (0, 0)),
            pl.BlockSpec((1, 128), lambda bi, h, i: (0, 0)),
            _once((tt, 128), lambda bi, h, i: (0, 0)),
            _once((tt, 128), lambda bi, h, i: (0, 0)),
            pl.BlockSpec((tq, 128), lambda bi, h, i: (i, 0)),
            pl.BlockSpec((tq, 128), lambda bi, h, i: (i, 0)),
        ],
        out_specs=pl.BlockSpec((1, tq, hw), lambda bi, h, i: (bi, i, h)),
        out_shape=jax.ShapeDtypeStruct((b, tt, BRANCH_W), BF16),
        scratch_shapes=[pltpu.VMEM((hps, tt, 128), BF16)],
        compiler_params=_params("arbitrary", "arbitrary", "arbitrary"),
        name="diff_attn",
    )(proj, proj, proj, lam_q, lam_k, norm_g.reshape(1, -1), ct, st, ct, st)


def _s5_kernel(x_ref, t_ref, uh_ref, hy_ref, p_ref, o_ref, tok_s, xf_s, of_s, vf_s, vfs_s, vb_s, vbs_s,
               hf_s, hb_s, *, nb, nch, nctx):
    q = S5_CHUNK
    r = nb * nch
    g8 = pl.program_id(0) % 8
    lane_blk = lax.broadcasted_iota(jnp.int32, (1, 128), 1) // S5_GROUP

    @pl.when(g8 == 0)
    def _():
        tok_s[...] = x_ref[...].astype(F32)
        for i in range(q):
            xf_s[i] = tok_s[pl.ds(i, r, stride=q), :]
        of_s[...] = jnp.zeros_like(of_s)

    cols = []
    for k in range(q // 8):
        acc = jnp.zeros((r, 128), F32)
        for i8 in range(8):
            rot = pltpu.roll(xf_s[k * 8 + i8], lax.rem((i8 - g8) * S5_GROUP + 128, 128), 1)
            acc = jnp.where(lane_blk == i8, rot, acc)
        cols.append(acc)
    x = jnp.concatenate(cols, axis=1).astype(BF16)

    yloc = jnp.dot(x, t_ref[0], preferred_element_type=F32)
    v = jnp.dot(x, uh_ref[0], preferred_element_type=F32)
    vf_s[...] = v[:, 0:128]
    vfs_s[...] = v[:, 128:256]
    vb_s[...] = v[:, 256:384]
    vbs_s[...] = v[:, 384:512]
    pm = p_ref[0]
    zero = jnp.zeros((nb, 128), F32)

    def scan(order, v_ref, vs_ref, h_ref, p1, p2):
        h, hs = zero, zero
        for c in order:
            rows = pl.ds(c, nb, stride=nch)
            h_ref[rows, :] = h
            h, hs = h * p1 + hs * p2 + v_ref[rows, :], hs * p1 - h * p2 + vs_ref[rows, :]

    scan(range(nch), vf_s, vfs_s, hf_s, pm[0:1], pm[1:2])
    scan(list(range(nctx - 1, -1, -1)) + list(range(nch - 1, nctx - 1, -1)), vb_s, vbs_s, hb_s,
         pm[2:3], pm[3:4])
    hcat = jnp.concatenate([hf_s[...], hb_s[...]], axis=1).astype(BF16)
    y = yloc + jnp.dot(hcat, hy_ref[0], preferred_element_type=F32)

    for k in range(q // 8):
        yk = y[:, k * 128:(k + 1) * 128]
        for i8 in range(8):
            rot = pltpu.roll(yk, lax.rem((g8 - i8) * S5_GROUP + 128, 128), 1)
            i = k * 8 + i8
            of_s[i] = jnp.where(lane_blk == g8, rot, of_s[i])

    @pl.when(g8 == 7)
    def _():
        for i in range(q):
            tok_s[pl.ds(i, r, stride=q), :] = of_s[i]
        o_ref[...] = tok_s[...].astype(o_ref.dtype)


def _s5_weights(lam_re, lam_im, log_dt, b_re, b_im, c_re, c_im, d):
    q = S5_CHUNK
    g, p, s = S5_GROUPS, S5_STATE, S5_GROUP
    hi = lax.Precision.HIGHEST
    lr, li = lam_re.astype(F32), lam_im.astype(F32)
    dt = jnp.exp(log_dt.astype(F32))[..., None]
    tau = jnp.arange(q + 1, dtype=F32)[None, :, None, None]
    mag = jnp.exp((lr * dt)[:, None] * tau)
    ang = (li * dt)[:, None] * tau
    ar, ai = mag * jnp.cos(ang), mag * jnp.sin(ang)
    den = lr * lr + li * li
    fr = ((ar[:, 1] - 1.0) * lr + ai[:, 1] * li) / den
    fi = (ai[:, 1] * lr - (ar[:, 1] - 1.0) * li) / den
    br, bi = b_re.astype(F32), b_im.astype(F32)
    bbr = fr[..., None] * br - fi[..., None] * bi
    bbi = fr[..., None] * bi + fi[..., None] * br
    cr, ci = c_re.astype(F32)[:, None], c_im.astype(F32)[:, None]
    car = cr * ar[:, :, :, None, :] - ci * ai[:, :, :, None, :]
    cai = cr * ai[:, :, :, None, :] + ci * ar[:, :, :, None, :]
    bbr_t = jnp.swapaxes(bbr, 2, 3)[:, None, :, None]
    bbi_t = jnp.swapaxes(bbi, 2, 3)[:, None, :, None]
    kern = jnp.sum(car[:, :q, :, :, None, :] * bbr_t - cai[:, :q, :, :, None, :] * bbi_t,
                   axis=-1)
    ii = np.arange(q)
    lag = ii[None, :, None] - ii[:, None, None]
    sel_f = jnp.asarray(lag == ii[None, None, :], F32)
    sel_b = jnp.asarray(-lag == ii[None, None, :], F32)
    tmat = (jnp.einsum('jit,tgos->gjsio', sel_f, kern[0], precision=hi)
            + jnp.einsum('jit,tgos->gjsio', sel_b, kern[1], precision=hi))
    eye = jnp.eye(q, dtype=F32)[:, None, :, None] * jnp.eye(s, dtype=F32)[None, :, None, :]
    tmat = tmat + eye[None] * d.astype(F32).reshape(g, 1, s, 1, 1)
    tmat = tmat.reshape(g, q * s, q * s)

    def chunk_in(are, aim, bre, bim):
        re = jnp.einsum('jgp,gpi->gjip', are, bre) - jnp.einsum('jgp,gpi->gjip', aim, bim)
        im = jnp.einsum('jgp,gpi->gjip', are, bim) + jnp.einsum('jgp,gpi->gjip', aim, bre)
        return jnp.concatenate([re, im, im, re], axis=-1)

    uh = jnp.concatenate([chunk_in(ar[0, :q][::-1], ai[0, :q][::-1], bbr[0], bbi[0]),
                          chunk_in(ar[1, :q], ai[1, :q], bbr[1], bbi[1])], axis=-1).reshape(g, q * s, 8 * p)

    def state_out(re, im):
        return jnp.concatenate([jnp.transpose(re, (1, 3, 0, 2)), -jnp.transpose(im, (1, 3, 0, 2))], axis=1)

    hy = jnp.concatenate([state_out(car[0, 1:], cai[0, 1:]),
                          state_out(car[1, 1:][::-1], cai[1, 1:][::-1])], axis=1).reshape(g, 4 * p, q * s)
    pm = jnp.stack([jnp.concatenate([ar[0, q], ar[0, q]], -1), jnp.concatenate([-ai[0, q], ai[0, q]], -1),
                    jnp.concatenate([ar[1, q], ar[1, q]], -1), jnp.concatenate([-ai[1, q], ai[1, q]], -1)],
                   axis=1)
    pm = jnp.concatenate([pm, jnp.zeros_like(pm)], axis=1)
    return tmat.astype(BF16), uh.astype(BF16), hy.astype(BF16), pm


def _s5(proj, weights, tc):
    b, tt, _ = proj.shape
    q, g, s = S5_CHUNK, S5_GROUPS, S5_GROUP
    nch = tt // q
    tmat, uh, hy, pm = weights
    r = nch * b
    rows = b * tt
    y = pl.pallas_call(
        functools.partial(_s5_kernel, nb=b, nch=nch, nctx=tc // q),
        grid=(g,),
        in_specs=[_once((rows, 128), lambda gi: (0, OFF_U // 128 + gi // 8)),
                  pl.BlockSpec((1, q * s, q * s), lambda gi: (gi, 0, 0)),
                  pl.BlockSpec((1, q * s, 512), lambda gi: (gi, 0, 0)),
                  pl.BlockSpec((1, 256, q * s), lambda gi: (gi, 0, 0)),
                  pl.BlockSpec((1, 8, 128), lambda gi: (gi, 0, 0))],
        out_specs=pl.BlockSpec((rows, 128), lambda gi: (0, gi // 8)),
        out_shape=jax.ShapeDtypeStruct((rows, BRANCH_W), BF16),
        scratch_shapes=[pltpu.VMEM((rows, 128), F32), pltpu.VMEM((q, r, 128), F32),
                        pltpu.VMEM((q, r, 128), F32)] + [pltpu.VMEM((r, 128), F32)] * 6,
        compiler_params=_params("arbitrary"),
        name="s5",
    )(proj.reshape(rows, N_PACK), tmat, uh, hy, pm)
    return y.reshape(b, tt, BRANCH_W)


def _ssd_kernel(*refs, backward, first, nctx_chunks, nchunks):
    if first:
        (xbc_ref, prev_ref, next_ref, z_ref, dt_ref, cw_ref, cb_ref, dtb_ref, alog_ref, dsk_ref,
         sel_ref, o_ref, st_s) = refs
        yin_ref = ng_ref = None
    else:
        (xbc_ref, prev_ref, next_ref, z_ref, dt_ref, cw_ref, cb_ref, dtb_ref, alog_ref, dsk_ref,
         sel_ref, yin_ref, ng_ref, o_ref, st_s) = refs
    qn = SSD_CHUNK
    step = pl.program_id(1)
    if backward:
        c = jnp.where(step < nctx_chunks, nctx_chunks - 1 - step, nchunks - 1 - (step - nctx_chunks))
    else:
        c = step

    @pl.when(step == 0)
    def _():
        st_s[...] = jnp.zeros_like(st_s)

    x = xbc_ref[0].astype(F32)
    has_prev = jnp.logical_and(c != 0, c != nctx_chunks)
    has_next = jnp.logical_and(c != nctx_chunks - 1, c != nchunks - 1)
    prow = jnp.where(has_prev, prev_ref[0, 7:8, :].astype(F32), 0.0)
    nrow = jnp.where(has_next, next_ref[0, 0:1, :].astype(F32), 0.0)
    rid = lax.broadcasted_iota(jnp.int32, (qn, 1), 0)
    xm = jnp.where(rid == 0, prow, pltpu.roll(x, 1, 0))
    xp = jnp.where(rid == qn - 1, nrow, pltpu.roll(x, qn - 1, 0))
    cw = cw_ref[...]
    conv = xm * cw[0:1] + x * cw[1:2] + xp * cw[2:3] + cb_ref[...]
    act = _silu(conv)
    xs = act[:, :BRANCH_W]
    gn = SSD_GROUPS * SSD_STATE
    bm = act[:, BRANCH_W:BRANCH_W + gn].astype(BF16)
    cm = act[:, BRANCH_W + gn:].astype(BF16)

    dt = _softplus(dt_ref[0].astype(F32) + dtb_ref[...])
    a = -jnp.exp(alog_ref[...])
    da = dt * a
    ri = lax.broadcasted_iota(jnp.int32, (qn, qn), 0)
    ci = lax.broadcasted_iota(jnp.int32, (qn, qn), 1)
    causal = (ci >= ri) if backward else (ci <= ri)
    ones_tri = jnp.where(causal, 1.0, 0.0).astype(BF16)
    d1 = da.astype(BF16)
    r1 = da - d1.astype(F32)
    d2 = r1.astype(BF16)
    d3 = (r1 - d2.astype(F32)).astype(BF16)
    cum = (jnp.dot(ones_tri, d1, preferred_element_type=F32)
           + jnp.dot(ones_tri, d2, preferred_element_type=F32)
           + jnp.dot(ones_tri, d3, preferred_element_type=F32))
    edge = cum[0:1] if backward else cum[qn - 1:qn]
    cum_t = cum.T
    dt_t = dt.T
    w_edge = dt * jnp.exp(edge - cum)
    e_in = jnp.exp(cum)

    def expand(v):
        v1 = v.astype(BF16)
        v2 = (v - v1.astype(F32)).astype(BF16)
        return (jnp.dot(v1, sel_ref[...], preferred_element_type=F32)
                + jnp.dot(v2, sel_ref[...], preferred_element_type=F32))

    xw = (xs * expand(w_edge)).astype(BF16)
    e_in_x = expand(e_in)
    xs_b = xs.astype(BF16)
    rpg = SSD_HEADS // SSD_GROUPS
    gw = rpg * SSD_HEAD_DIM
    ys = []
    for g in range(SSD_GROUPS):
        bg = bm[:, g * SSD_STATE:(g + 1) * SSD_STATE]
        cg = cm[:, g * SSD_STATE:(g + 1) * SSD_STATE]
        cb = lax.dot_general(cg, bg, (((1,), (1,)), ((), ())), preferred_element_type=F32)
        st_g = st_s[g * gw:(g + 1) * gw, :]
        y_off = lax.dot_general(cg, st_g.astype(BF16), (((1,), (1,)), ((), ())),
                                preferred_element_type=F32)
        yg = y_off * e_in_x[:, g * gw:(g + 1) * gw]
        parts = []
        for r in range(rpg):
            h = g * rpg + r
            dec = jnp.exp(jnp.where(causal, cum[:, h:h + 1] - cum_t[h:h + 1, :], -jnp.inf))
            wmat = (cb * dec * dt_t[h:h + 1, :]).astype(BF16)
            parts.append(jnp.dot(wmat, xs_b[:, h * SSD_HEAD_DIM:(h + 1) * SSD_HEAD_DIM],
                                 preferred_element_type=F32))
        ys.append(yg + jnp.concatenate(parts, axis=1))
        new = lax.dot_general(xw[:, g * gw:(g + 1) * gw], bg, (((0,), (0,)), ((), ())),
                              preferred_element_type=F32)
        for r in range(rpg):
            h = g * rpg + r
            rows = slice(g * gw + r * SSD_HEAD_DIM, g * gw + (r + 1) * SSD_HEAD_DIM)
            cd = jnp.exp(edge[:, h:h + 1])
            st_s[rows, :] = st_s[rows, :] * cd + new[r * SSD_HEAD_DIM:(r + 1) * SSD_HEAD_DIM, :]
    y = jnp.concatenate(ys, axis=1)
    if first:
        o_ref[0] = y + xs * dsk_ref[...]
    else:
        y = y + yin_ref[0]
        zz = z_ref[0].astype(F32)
        y = y * _silu(zz)
        y = y * lax.rsqrt(jnp.mean(y * y, axis=-1, keepdims=True) + RMS_EPS) * ng_ref[...]
        o_ref[0] = y.astype(o_ref.dtype)


def _ssd_pass(proj, conv_w, conv_b, dt_bias, a_log, d_skip, sel, yin, norm_g, tc, backward):
    b, tt, _ = proj.shape
    qn = SSD_CHUNK
    nchunks = tt // qn
    nctx = tc // qn
    first = yin is None
    last_blk8 = tt // 8 - 1

    def chunk_of(s):
        if backward:
            return jnp.where(s < nctx, nctx - 1 - s, nchunks - 1 - (s - nctx))
        return s

    xbc_blk = OFF_XBC // SSD_CONV_CH
    in_specs = [
        pl.BlockSpec((1, qn, SSD_CONV_CH), lambda bi, s: (bi, chunk_of(s), xbc_blk)),
        pl.BlockSpec((1, 8, SSD_CONV_CH),
                     lambda bi, s: (bi, jnp.maximum(chunk_of(s) * (qn // 8) - 1, 0), xbc_blk)),
        pl.BlockSpec((1, 8, SSD_CONV_CH),
                     lambda bi, s: (bi, jnp.minimum((chunk_of(s) + 1) * (qn // 8), last_blk8), xbc_blk)),
        pl.BlockSpec((1, qn, BRANCH_W), lambda bi, s: (bi, chunk_of(s), OFF_Z // BRANCH_W)),
        pl.BlockSpec((1, qn, 128), lambda bi, s: (bi, chunk_of(s), OFF_DT // 128)),
        pl.BlockSpec((8, SSD_CONV_CH), lambda bi, s: (0, 0)),
        pl.BlockSpec((1, SSD_CONV_CH), lambda bi, s: (0, 0)),
        pl.BlockSpec((1, 128), lambda bi, s: (0, 0)),
        pl.BlockSpec((1, 128), lambda bi, s: (0, 0)),
        pl.BlockSpec((1, BRANCH_W), lambda bi, s: (0, 0)),
        pl.BlockSpec((128, BRANCH_W), lambda bi, s: (0, 0)),
    ]
    args = [proj, proj, proj, proj, proj, conv_w, conv_b, dt_bias, a_log, d_skip, sel]
    if not first:
        in_specs += [pl.BlockSpec((1, qn, BRANCH_W), lambda bi, s: (bi, chunk_of(s), 0)),
                     pl.BlockSpec((1, BRANCH_W), lambda bi, s: (0, 0))]
        args += [yin, norm_g]
    return pl.pallas_call(
        functools.partial(_ssd_kernel, backward=backward, first=first, nctx_chunks=nctx, nchunks=nchunks),
        grid=(b, nchunks),
        in_specs=in_specs,
        out_specs=pl.BlockSpec((1, qn, BRANCH_W), lambda bi, s: (bi, chunk_of(s), 0)),
        out_shape=jax.ShapeDtypeStruct((b, tt, BRANCH_W), F32 if first else BF16),
        scratch_shapes=[pltpu.VMEM((SSD_HEADS * SSD_HEAD_DIM, SSD_STATE), F32)],
        compiler_params=_params("arbitrary", "arbitrary"),
        name="ssd_bwd" if backward else "ssd_fwd",
    )(*args)


def _ssd(proj, conv_w, conv_b, dt_bias, a_log, d_skip, norm_g, tc):
    def lanes(v):
        return jnp.pad(v.astype(F32), (0, 128 - SSD_HEADS)).reshape(1, 128)

    cw = jnp.pad(conv_w.astype(F32), ((0, 5), (0, 0)))
    cb = conv_b.astype(F32).reshape(1, -1)
    dsk = jnp.repeat(d_skip.astype(F32), SSD_HEAD_DIM).reshape(1, -1)
    sel = (jnp.arange(128)[:, None] == (jnp.arange(BRANCH_W)[None, :] // SSD_HEAD_DIM)).astype(BF16)
    y1 = _ssd_pass(proj, cw, cb, lanes(dt_bias[0]), lanes(a_log[0]), dsk, sel, None, None, tc, False)
    return _ssd_pass(proj, cw, cb, lanes(dt_bias[1]), lanes(a_log[1]), jnp.zeros_like(dsk), sel,
                     y1, norm_g.astype(F32).reshape(1, -1), tc, True)


def _merge_kernel(ya_ref, ga_ref, yb_ref, gb_ref, yc_ref, yd_ref, gd_ref, gm_ref, bm_ref,
                  wb_ref, wg_ref, bg_ref, o_ref):
    def f(ref):
        return ref[0].astype(F32)

    g = _gelu_tanh(f(yb_ref))
    glu = g * _sigmoid(jnp.dot(g.astype(BF16), wg_ref[...], preferred_element_type=F32) + bg_ref[...])
    branches = (f(ya_ref) * _silu(f(ga_ref)), glu * _silu(f(gb_ref)), f(yc_ref), f(yd_ref) * _silu(f(gd_ref)))
    acc = None
    for n, br in enumerate(branches):
        gate = _sigmoid(gm_ref[0, :, n * D_MODEL:(n + 1) * D_MODEL].astype(F32) + bm_ref[n:n + 1, :])
        term = gate * jnp.dot(br.astype(BF16), wb_ref[n], preferred_element_type=F32)
        acc = term if acc is None else acc + term
    o_ref[0] = acc.astype(o_ref.dtype)


def _merge(proj, ya, yb, yc, yd, b_merge, w_branch, w_glu, b_glu):
    b, tt, _ = proj.shape
    tm = Q_BLOCK
    bw = BRANCH_W

    def pspec(off):
        return pl.BlockSpec((1, tm, bw), lambda bi, i: (bi, i, off // bw))

    yspec = pl.BlockSpec((1, tm, bw), lambda bi, i: (bi, i, 0))
    once = dict(pipeline_mode=pl.Buffered(1))
    return pl.pallas_call(
        _merge_kernel,
        grid=(b, tt // tm),
        in_specs=[yspec, pspec(OFF_GA), yspec, pspec(OFF_GB), yspec, yspec, pspec(OFF_GD),
                  pl.BlockSpec((1, tm, N_BRANCH * D_MODEL), lambda bi, i: (bi, i, 0)),
                  pl.BlockSpec((N_BRANCH, D_MODEL), lambda bi, i: (0, 0)),
                  pl.BlockSpec((N_BRANCH, bw, D_MODEL), lambda bi, i: (0, 0, 0), **once),
                  pl.BlockSpec((bw, bw), lambda bi, i: (0, 0), **once),
                  pl.BlockSpec((1, bw), lambda bi, i: (0, 0))],
        out_specs=pl.BlockSpec((1, tm, D_MODEL), lambda bi, i: (bi, i, 0)),
        out_shape=jax.ShapeDtypeStruct((b, tt, D_MODEL), BF16),
        compiler_params=_params("arbitrary", "arbitrary"),
        name="merge",
    )(ya, proj, yb, proj, yc, yd, proj, proj, b_merge, w_branch, w_glu, b_glu.reshape(1, -1))


def _out_kernel(m_ref, x_ref, mod_ref, w_ref, lg_ref, lb_ref, o_ref, *, tc, tm, row0):
    i = pl.program_id(1)
    out = jnp.dot(m_ref[0], w_ref[...], preferred_element_type=F32)
    row = row0 + i * tm + lax.broadcasted_iota(jnp.int32, (tm, 1), 0)
    m = mod_ref[0]
    gate = jnp.where(row < tc, m[4:5], m[5:6])
    v = DEEPNORM_ALPHA * x_ref[0] + gate * out
    mu = jnp.mean(v, axis=-1, keepdims=True)
    vc = v - mu
    var = jnp.mean(vc * vc, axis=-1, keepdims=True)
    o_ref[0] = vc * lax.rsqrt(var + LN_EPS) * lg_ref[...] + lb_ref[...]


def _out(merged, xa, mod, w_out, ln_g, ln_b, tc, latent_only):
    b, tt, d = xa.shape
    tm = Q_BLOCK
    skip = tc // tm if latent_only else 0
    nblk = tt // tm - skip
    spec = pl.BlockSpec((1, tm, d), lambda bi, i: (bi, i + skip, 0))
    return pl.pallas_call(
        functools.partial(_out_kernel, tc=tc, tm=tm, row0=skip * tm),
        grid=(b, nblk),
        in_specs=[spec, spec,
                  pl.BlockSpec((1, 8, d), lambda bi, i: (bi, 0, 0)),
                  pl.BlockSpec((d, d), lambda bi, i: (0, 0)),
                  pl.BlockSpec((1, d), lambda bi, i: (0, 0)),
                  pl.BlockSpec((1, d), lambda bi, i: (0, 0))],
        out_specs=pl.BlockSpec((1, tm, d), lambda bi, i: (bi, i, 0)),
        out_shape=jax.ShapeDtypeStruct((b, nblk * tm, d), F32),
        compiler_params=_params("arbitrary", "arbitrary"),
        name="out_proj",
    )(merged, xa, mod, w_out, ln_g.reshape(1, -1), ln_b.reshape(1, -1))


def _half_split(w, heads):
    k = w.shape[0]
    return w.reshape(k, heads, ROPE_DIM // 2, 2).transpose(0, 1, 3, 2).reshape(k, heads * ROPE_DIM)


def _pack_w_in(w):
    sp = np.cumsum([0, MLA_Q_LORA, MLA_KV_LORA, MLA_ROPE, BRANCH_W, BRANCH_W, BRANCH_W, BRANCH_W,
                    SSD_CONV_CH, SSD_HEADS, BRANCH_W, BRANCH_W, BRANCH_W, BRANCH_W, N_BRANCH * D_MODEL])
    seg = [w[:, sp[k]:sp[k + 1]] for k in range(14)]
    cq, ckv, kr, ga, u, gb, z, xbc, dt, qd, kd, vd, gd, gm = seg
    kr_hs = _half_split(kr, 1)
    kr_sw = jnp.concatenate([kr_hs[:, 32:], kr_hs[:, :32]], axis=1)
    pad = jnp.zeros((w.shape[0], 128 - SSD_HEADS), w.dtype)
    packed = jnp.concatenate([gm, xbc, ga, u, gb, z, _half_split(qd, 2 * DIFF_HEADS),
                              _half_split(kd, 2 * DIFF_HEADS), vd, gd, cq, ckv, kr_hs, kr_sw, dt, pad],
                             axis=1)
    return packed.astype(BF16)


def _pack_mla(w_uq, w_ukv):
    k = w_uq.shape[0]
    wq = w_uq.reshape(k, MLA_HEADS, MLA_NOPE + MLA_ROPE)
    rope = wq[:, :, MLA_NOPE:].reshape(k, MLA_HEADS, ROPE_DIM // 2, 2)
    ev, od = rope[..., 0], rope[..., 1]
    wq = jnp.concatenate([wq[:, :, :MLA_NOPE], ev, od, od, ev], axis=-1)
    wkv = w_ukv.reshape(w_ukv.shape[0], MLA_HEADS, MLA_NOPE + MLA_V)
    return (wq.transpose(1, 0, 2).astype(BF16),
            wkv[:, :, :MLA_NOPE].transpose(1, 0, 2).astype(BF16),
            wkv[:, :, MLA_NOPE:].transpose(1, 0, 2).astype(BF16))


def _rope_tables(tl, tc):
    rows = tl // GRID_W
    row_id = jnp.repeat(jnp.arange(rows, dtype=F32), GRID_W)
    col_id = jnp.tile(jnp.arange(GRID_W, dtype=F32), rows)
    quarter = ROPE_DIM // 4
    inv_freq = ROPE_BASE ** (-jnp.arange(quarter, dtype=F32) / quarter)
    ang = jnp.concatenate([row_id[:, None] * inv_freq, col_id[:, None] * inv_freq], axis=-1)
    cos = jnp.concatenate([jnp.ones((tc, ROPE_DIM // 2), F32), jnp.cos(ang)], axis=0)
    sin = jnp.concatenate([jnp.zeros((tc, ROPE_DIM // 2), F32), jnp.sin(ang)], axis=0)
    zero = jnp.zeros_like(cos)
    mla = (jnp.concatenate([cos, cos, zero, zero], axis=1), jnp.concatenate([zero, zero, -sin, sin], axis=1))
    diff = (jnp.concatenate([cos, cos, cos, cos], axis=1), jnp.concatenate([-sin, sin, -sin, sin], axis=1))
    return mla, diff


def _layer(xa, cs, p, tables, layer_idx, tc, last):
    b = xa.shape[0]
    mla_tab, diff_tab = tables
    mod = _ada(cs, p['w_ada_stacked'], p['b_ada'], layer_idx)
    d = D_MODEL
    shift, scale, gate = mod[:, :d], mod[:, d:2 * d], mod[:, 2 * d:]
    zeros = jnp.zeros((b, d), F32)
    modb = jnp.stack([jnp.broadcast_to(scale[0], (b, d)), jnp.broadcast_to(shift[0], (b, d)),
                      scale[1:1 + b], shift[1:1 + b],
                      jnp.broadcast_to(gate[0], (b, d)), gate[1:1 + b], zeros, zeros], axis=1)
    proj = _inproj(xa, modb, _pack_w_in(p['w_in']), tc)
    wq, wk, wv = _pack_mla(p['mla_w_uq'], p['mla_w_ukv'])
    ya = _mla(proj, wq, wk, wv, p['mla_q_norm'], p['mla_kv_norm'], mla_tab[0], mla_tab[1], tc)
    lam_init = 0.8 - 0.6 * math.exp(-0.3 * layer_idx)
    yd = _diff(proj, p['diff_lambda_q'], p['diff_lambda_k'], p['diff_norm'], diff_tab[0], diff_tab[1],
               lam_init, tc)
    yb = _s5(proj, _s5_weights(p['s5_lambda_re'], p['s5_lambda_im'], p['s5_log_dt'], p['s5_b_re'],
                               p['s5_b_im'], p['s5_c_re'], p['s5_c_im'], p['s5_d']), tc)
    yc = _ssd(proj, p['ssd_conv_w'], p['ssd_conv_b'], p['ssd_dt_bias'], p['ssd_a_log'], p['ssd_d'],
              p['ssd_norm'], tc)
    merged = _merge(proj, ya, yb, yc, yd, p['b_merge'], p['w_branch'].astype(BF16),
                    p['s5_w_glu'].astype(BF16), p['s5_b_glu'])
    return _out(merged, xa, modb, p['w_out'].astype(BF16), p['ln_g'], p['ln_b'], tc, last)


def kernel(x, c, ctx, c_ctx, w_ada, b_ada, w_in, mla_q_norm, mla_w_uq, mla_kv_norm, mla_w_ukv,
           s5_lambda_re, s5_lambda_im, s5_log_dt, s5_b_re, s5_b_im, s5_c_re, s5_c_im, s5_d,
           s5_w_glu, s5_b_glu, ssd_conv_w, ssd_conv_b, ssd_dt_bias, ssd_a_log, ssd_d, ssd_norm,
           diff_lambda_q, diff_lambda_k, diff_norm, b_merge, w_branch, w_out, ln_g, ln_b):
    b, tl, d = x.shape
    tc = ctx.shape[1]
    stacked = dict(
        w_ada=w_ada, b_ada=b_ada, w_in=w_in, mla_q_norm=mla_q_norm, mla_w_uq=mla_w_uq,
        mla_kv_norm=mla_kv_norm, mla_w_ukv=mla_w_ukv, s5_lambda_re=s5_lambda_re, s5_lambda_im=s5_lambda_im,
        s5_log_dt=s5_log_dt, s5_b_re=s5_b_re, s5_b_im=s5_b_im, s5_c_re=s5_c_re, s5_c_im=s5_c_im, s5_d=s5_d,
        s5_w_glu=s5_w_glu, s5_b_glu=s5_b_glu, ssd_conv_w=ssd_conv_w, ssd_conv_b=ssd_conv_b,
        ssd_dt_bias=ssd_dt_bias, ssd_a_log=ssd_a_log, ssd_d=ssd_d, ssd_norm=ssd_norm,
        diff_lambda_q=diff_lambda_q, diff_lambda_k=diff_lambda_k, diff_norm=diff_norm, b_merge=b_merge,
        w_branch=w_branch, w_out=w_out, ln_g=ln_g, ln_b=ln_b)
    tables = _rope_tables(tl, tc)
    cs = jnp.concatenate([c_ctx[None], c, jnp.zeros((8 - 1 - b, d), c.dtype)], axis=0)
    xa = jnp.concatenate([ctx, x], axis=1)
    depth = w_in.shape[0]
    for i in range(depth):
        p = {k: v[i] for k, v in stacked.items() if k != 'w_ada'}
        p['w_ada_stacked'] = w_ada
        xa = _layer(xa, cs, p, tables, i, tc, i == depth - 1)
    return xa
```

```python
import functools
import math

import numpy as np
import jax
import jax.numpy as jnp
from jax import lax
from jax.experimental import pallas as pl
from jax.experimental.pallas import tpu as pltpu

F32 = jnp.float32
BF16 = jnp.bfloat16

D_MODEL = 2048
DEPTH = 2
GRID_W = 64
N_BRANCH = 4
BRANCH_W = 1024
ROPE_DIM = 64
ROPE_BASE = 10000.0

MLA_HEADS = 8
MLA_NOPE = 128
MLA_ROPE = ROPE_DIM
MLA_V = 128
MLA_Q_LORA = 512
MLA_KV_LORA = 256

S5_GROUP = 16
S5_GROUPS = BRANCH_W // S5_GROUP
S5_STATE = 64
S5_CHUNK = 16

SSD_HEAD_DIM = 64
SSD_HEADS = BRANCH_W // SSD_HEAD_DIM
SSD_GROUPS = 4
SSD_STATE = 128
SSD_CHUNK = 128
SSD_CONV_CH = BRANCH_W + 2 * SSD_GROUPS * SSD_STATE

DIFF_HEAD_DIM = ROPE_DIM
DIFF_HEADS = BRANCH_W // (2 * DIFF_HEAD_DIM)

LN_EPS = 1e-5
RMS_EPS = 1e-6
DEEPNORM_ALPHA = (2 * DEPTH) ** 0.25

V7X_VMEM_LIMIT_BYTES = 56 * 1024 * 1024
Q_BLOCK = 256
ATTN_HEADS_PER_STEP = 2
LOG2E = math.log2(math.e)

OFF_GM = 0
OFF_XBC = 8192
OFF_GA = 10240
OFF_U = 11264
OFF_GB = 12288
OFF_Z = 13312
OFF_QD = 14336
OFF_KD = 15360
OFF_VD = 16384
OFF_GD = 17408
OFF_CQ = 18432
OFF_CKV = 18944
OFF_KR = 19200
OFF_DT = 19328
N_PACK = 19456
IN_TILE_N = 1024


def _sigmoid(x):
    return 0.5 * (1.0 + jnp.tanh(0.5 * x))


def _silu(x):
    return x * _sigmoid(x)


def _gelu_tanh(x):
    return 0.5 * x * (1.0 + jnp.tanh(math.sqrt(2.0 / math.pi) * (x + 0.044715 * (x * x * x))))


def _softplus(x):
    return jnp.maximum(x, 0.0) + jnp.log(1.0 + jnp.exp(-jnp.abs(x)))


def _params(*sem):
    return pltpu.CompilerParams(dimension_semantics=sem, vmem_limit_bytes=V7X_VMEM_LIMIT_BYTES)


def _ada_kernel(c_ref, w_ref, b_ref, o_ref):
    s = _silu(c_ref[...])
    o_ref[...] = jnp.dot(s.astype(BF16), w_ref[0].astype(BF16),
                         preferred_element_type=F32) + b_ref[...]


def _ada(cs, w_ada, b_ada, layer):
    n = w_ada.shape[2]
    tn = 512
    return pl.pallas_call(
        _ada_kernel,
        grid=(n // tn,),
        in_specs=[pl.BlockSpec((8, D_MODEL), lambda j: (0, 0)),
                  pl.BlockSpec((1, D_MODEL, tn), lambda j: (layer, 0, j)),
                  pl.BlockSpec((1, tn), lambda j: (0, j))],
        out_specs=pl.BlockSpec((8, tn), lambda j: (0, j)),
        out_shape=jax.ShapeDtypeStruct((8, n), F32),
        compiler_params=_params("arbitrary"),
        name="ada",
    )(cs, w_ada, b_ada.reshape(1, n))


def _inproj_kernel(x_ref, mod_ref, w_ref, o_ref, h_ref, *, tc, tm):
    i = pl.program_id(1)

    @pl.when(pl.program_id(2) == 0)
    def _():
        x = x_ref[0]
        mu = jnp.mean(x, axis=-1, keepdims=True)
        xc = x - mu
        var = jnp.mean(xc * xc, axis=-1, keepdims=True)
        xn = xc * lax.rsqrt(var + LN_EPS)
        row = i * tm + lax.broadcasted_iota(jnp.int32, (tm, 1), 0)
        is_ctx = row < tc
        m = mod_ref[0]
        scale = jnp.where(is_ctx, m[0:1], m[2:3])
        shift = jnp.where(is_ctx, m[1:2], m[3:4])
        h_ref[...] = (xn * (1.0 + scale) + shift).astype(BF16)

    o_ref[0] = jnp.dot(h_ref[...], w_ref[...], preferred_element_type=F32).astype(o_ref.dtype)


def _inproj(xa, mod, w_pack, tc):
    b, tt, d = xa.shape
    tm = tt // 4
    tn = IN_TILE_N
    return pl.pallas_call(
        functools.partial(_inproj_kernel, tc=tc, tm=tm),
        grid=(b, tt // tm, N_PACK // tn),
        in_specs=[pl.BlockSpec((1, tm, d), lambda bi, i, j: (bi, i, 0)),
                  pl.BlockSpec((1, 8, d), lambda bi, i, j: (bi, 0, 0)),
                  pl.BlockSpec((d, tn), lambda bi, i, j: (0, j))],
        out_specs=pl.BlockSpec((1, tm, tn), lambda bi, i, j: (bi, i, j)),
        out_shape=jax.ShapeDtypeStruct((b, tt, N_PACK), BF16),
        scratch_shapes=[pltpu.VMEM((tm, d), BF16)],
        compiler_params=_params("arbitrary", "arbitrary", "arbitrary"),
        name="inproj",
    )(xa, mod, w_pack)


def _mla_kernel(cq_ref, ckv_ref, kr_ref, wq_ref, wk_ref, wv_ref, qn_ref, kvn_ref,
                ct_ref, st_ref, ctq_ref, stq_ref, o_ref, k_s, v_s, *, tc, tt, scale):
    i = pl.program_id(2)
    hps = ATTN_HEADS_PER_STEP

    @pl.when(i == 0)
    def _():
        ckv = ckv_ref[0].astype(F32)
        r = lax.rsqrt(jnp.mean(ckv * ckv, axis=-1, keepdims=True) + RMS_EPS)
        ckvn = (ckv * r * kvn_ref[...]).astype(BF16)
        kr = kr_ref[0].astype(F32)
        kroped = (kr * ct_ref[...] + pltpu.roll(kr * st_ref[...], 64, 1)).astype(BF16)
        for hh in range(hps):
            k_s[hh, :, 0:MLA_NOPE] = jnp.dot(ckvn, wk_ref[hh], preferred_element_type=F32).astype(BF16)
            k_s[hh, :, MLA_NOPE:] = kroped
            v_s[hh] = jnp.dot(ckvn, wv_ref[hh], preferred_element_type=F32).astype(BF16)

    cq = cq_ref[0].astype(F32)
    r = lax.rsqrt(jnp.mean(cq * cq, axis=-1, keepdims=True) + RMS_EPS)
    cqn = (cq * r * qn_ref[...]).astype(BF16)
    qfs = []
    for hh in range(hps):
        q = jnp.dot(cqn, wq_ref[hh], preferred_element_type=F32)
        qh = q[:, MLA_NOPE:]
        qr = qh * ctq_ref[...] + pltpu.roll(qh * stq_ref[...], 64, 1)
        qfs.append((jnp.concatenate([q[:, :MLA_NOPE], qr], axis=1) * (scale * LOG2E)).astype(BF16))

    def attend(nk):
        outs = []
        for hh in range(hps):
            s = lax.dot_general(qfs[hh], k_s[hh, 0:nk, :], (((1,), (1,)), ((), ())),
                                preferred_element_type=F32)
            p = jnp.exp2(s - jnp.max(s, axis=1, keepdims=True))
            l = jnp.sum(p, axis=1, keepdims=True)
            o = jnp.dot(p.astype(BF16), v_s[hh, 0:nk, :], preferred_element_type=F32)
            outs.append(o * (1.0 / l))
        o_ref[0] = jnp.concatenate(outs, axis=1).astype(o_ref.dtype)

    @pl.when(i == 0)
    def _():
        attend(tc)

    @pl.when(i > 0)
    def _():
        attend(tt)


def _once(shape, index_map):
    return pl.BlockSpec(shape, index_map, pipeline_mode=pl.Buffered(1))


def _mla(proj, wq, wk, wv, q_norm, kv_norm, ct, st, tc):
    b, tt, _ = proj.shape
    tq = Q_BLOCK
    assert tc == tq
    scale = (MLA_NOPE + MLA_ROPE) ** -0.5
    hps = ATTN_HEADS_PER_STEP
    return pl.pallas_call(
        functools.partial(_mla_kernel, tc=tc, tt=tt, scale=scale),
        grid=(b, MLA_HEADS // hps, tt // tq),
        in_specs=[
            pl.BlockSpec((1, tq, MLA_Q_LORA), lambda bi, h, i: (bi, i, OFF_CQ // MLA_Q_LORA)),
            _once((1, tt, MLA_KV_LORA), lambda bi, h, i: (bi, 0, OFF_CKV // MLA_KV_LORA)),
            _once((1, tt, 128), lambda bi, h, i: (bi, 0, OFF_KR // 128)),
            pl.BlockSpec((hps, MLA_Q_LORA, 256), lambda bi, h, i: (h, 0, 0)),
            pl.BlockSpec((hps, MLA_KV_LORA, MLA_NOPE), lambda bi, h, i: (h, 0, 0)),
            pl.BlockSpec((hps, MLA_KV_LORA, MLA_V), lambda bi, h, i: (h, 0, 0)),
            pl.BlockSpec((1, MLA_Q_LORA), lambda bi, h, i: (0, 0)),
            pl.BlockSpec((1, MLA_KV_LORA), lambda bi, h, i: (0, 0)),
            _once((tt, 128), lambda bi, h, i: (0, 0)),
            _once((tt, 128), lambda bi, h, i: (0, 0)),
            pl.BlockSpec((tq, 128), lambda bi, h, i: (i, 0)),
            pl.BlockSpec((tq, 128), lambda bi, h, i: (i, 0)),
        ],
        out_specs=pl.BlockSpec((1, tq, hps * MLA_V), lambda bi, h, i: (bi, i, h)),
        out_shape=jax.ShapeDtypeStruct((b, tt, BRANCH_W), BF16),
        scratch_shapes=[pltpu.VMEM((hps, tt, 256), BF16), pltpu.VMEM((hps, tt, MLA_V), BF16)],
        compiler_params=_params("arbitrary", "arbitrary", "arbitrary"),
        name="mla_attn",
    )(proj, proj, proj, wq, wk, wv, q_norm.reshape(1, -1), kv_norm.reshape(1, -1), ct, st, ct, st)


def _diff_kernel(q_ref, k_ref, v_ref, lq_ref, lk_ref, g_ref, ct_ref, st_ref, ctq_ref, stq_ref,
                 o_ref, k_s, *, tc, tt, lam_init):
    i = pl.program_id(2)
    hps = ATTN_HEADS_PER_STEP
    lane = lax.broadcasted_iota(jnp.int32, (1, 128), 1)
    first_half = (lane & 32) == 0

    def rope(x, c, s):
        xs = jnp.where(first_half, pltpu.roll(x, 96, 1), pltpu.roll(x, 32, 1))
        return x * c + xs * s

    @pl.when(i == 0)
    def _():
        for hh in range(hps):
            k_s[hh] = rope(k_ref[0, :, hh * 128:(hh + 1) * 128].astype(F32), ct_ref[...],
                           st_ref[...]).astype(BF16)

    qs = []
    for hh in range(hps):
        q = rope(q_ref[0, :, hh * 128:(hh + 1) * 128].astype(F32), ctq_ref[...], stq_ref[...]) * (
            DIFF_HEAD_DIM ** -0.5 * LOG2E)
        qs.append((jnp.where(lane < 64, q, 0.0).astype(BF16), jnp.where(lane >= 64, q, 0.0).astype(BF16)))
    lqk = lq_ref[...] * lk_ref[...]
    lam = (jnp.exp(jnp.sum(lqk[0:1], axis=1, keepdims=True))
           - jnp.exp(jnp.sum(lqk[1:2], axis=1, keepdims=True)) + lam_init)

    def attend(nk):
        nt = (((1,), (1,)), ((), ()))
        outs = []
        for hh in range(hps):
            kk = k_s[hh, 0:nk, :]
            s1 = lax.dot_general(qs[hh][0], kk, nt, preferred_element_type=F32)
            s2 = lax.dot_general(qs[hh][1], kk, nt, preferred_element_type=F32)
            e1 = jnp.exp2(s1 - jnp.max(s1, axis=1, keepdims=True))
            e2 = jnp.exp2(s2 - jnp.max(s2, axis=1, keepdims=True))
            l1 = jnp.sum(e1, axis=1, keepdims=True)
            l2 = jnp.sum(e2, axis=1, keepdims=True)
            w = e1 - e2 * (lam * l1 / l2)
            o = jnp.dot(w.astype(BF16), v_ref[0, 0:nk, hh * 128:(hh + 1) * 128],
                        preferred_element_type=F32) * (1.0 / l1)
            y = o * lax.rsqrt(jnp.mean(o * o, axis=-1, keepdims=True) + RMS_EPS)
            outs.append(y * g_ref[...] * (1.0 - lam_init))
        o_ref[0] = jnp.concatenate(outs, axis=1).astype(o_ref.dtype)

    @pl.when(i == 0)
    def _():
        attend(tc)

    @pl.when(i > 0)
    def _():
        attend(tt)


def _diff(proj, lam_q, lam_k, norm_g, ct, st, lam_init, tc):
    b, tt, _ = proj.shape
    tq = Q_BLOCK
    assert tc == tq
    hps = ATTN_HEADS_PER_STEP
    hw = 128 * hps
    return pl.pallas_call(
        functools.partial(_diff_kernel, tc=tc, tt=tt, lam_init=lam_init),
        grid=(b, DIFF_HEADS // hps, tt // tq),
        in_specs=[
            pl.BlockSpec((1, tq, hw), lambda bi, h, i: (bi, i, OFF_QD // hw + h)),
            pl.BlockSpec((1, tt, hw), lambda bi, h, i: (bi, 0, OFF_KD // hw + h)),
            pl.BlockSpec((1, tt, hw), lambda bi, h, i: (bi, 0, OFF_VD // hw + h)),
            pl.BlockSpec((2, DIFF_HEAD_DIM), lambda bi, h, i: (0, 0)),
            pl.BlockSpec((2, DIFF_HEAD_DIM), lambda bi, h, i: (0, 0)),
            pl.BlockSpec((1, 128), lambda bi, h, i: (0, 0)),
            _once((tt, 128), lambda bi, h, i: (0, 0)),
            _once((tt, 128), lambda bi, h, i: (0, 0)),
            pl.BlockSpec((tq, 128), lambda bi, h, i: (i, 0)),
            pl.BlockSpec((tq, 128), lambda bi, h, i: (i, 0)),
        ],
        out_specs=pl.BlockSpec((1, tq, hw), lambda bi, h, i: (bi, i, h)),
        out_shape=jax.ShapeDtypeStruct((b, tt, BRANCH_W), BF16),
        scratch_shapes=[pltpu.VMEM((hps, tt, 128), BF16)],
        compiler_params=_params("arbitrary", "arbitrary", "arbitrary"),
        name="diff_attn",
    )(proj, proj, proj, lam_q, lam_k, norm_g.reshape(1, -1), ct, st, ct, st)


def _s5_kernel(x_ref, t_ref, uh_ref, hy_ref, p_ref, o_ref, tok_s, xf_s, ys_s, vf_s, vfs_s, vb_s, vbs_s,
               hf_s, hb_s, *, nb, nch, nctx):
    q = S5_CHUNK
    r = nb * nch
    g8 = pl.program_id(0) % 8
    lane_blk = lax.broadcasted_iota(jnp.int32, (1, 128), 1) // S5_GROUP
    u32 = jnp.uint32

    @pl.when(g8 == 0)
    def _():
        tok_s[...] = x_ref[...].astype(F32)
        for i in range(q):
            xf_s[i] = pltpu.bitcast(tok_s[pl.ds(i, r, stride=q), :].astype(BF16), u32)

    cols = []
    for k in range(q // 8):
        acc = jnp.zeros((r // 2, 128), u32)
        for i8 in range(8):
            rot = pltpu.roll(xf_s[k * 8 + i8], lax.rem((i8 - g8) * S5_GROUP + 128, 128), 1)
            acc = jnp.where(lane_blk == i8, rot, acc)
        cols.append(acc)
    x = pltpu.bitcast(jnp.concatenate(cols, axis=1), BF16)

    yloc = jnp.dot(x, t_ref[0], preferred_element_type=F32)
    v = jnp.dot(x, uh_ref[0], preferred_element_type=F32)
    vf_s[...] = v[:, 0:128]
    vfs_s[...] = v[:, 128:256]
    vb_s[...] = v[:, 256:384]
    vbs_s[...] = v[:, 384:512]
    pm = p_ref[0]
    zero = jnp.zeros((nb, 128), F32)

    def scan(order, v_ref, vs_ref, h_ref, p1, p2):
        h, hs = zero, zero
        for c in order:
            rows = pl.ds(c, nb, stride=nch)
            h_ref[rows, :] = h
            h, hs = h * p1 + hs * p2 + v_ref[rows, :], hs * p1 - h * p2 + vs_ref[rows, :]

    scan(range(nch), vf_s, vfs_s, hf_s, pm[0:1], pm[1:2])
    scan(list(range(nctx - 1, -1, -1)) + list(range(nch - 1, nctx - 1, -1)), vb_s, vbs_s, hb_s,
         pm[2:3], pm[3:4])
    hcat = jnp.concatenate([hf_s[...], hb_s[...]], axis=1).astype(BF16)
    y = yloc + jnp.dot(hcat, hy_ref[0], preferred_element_type=F32)
    ys_s[g8] = pltpu.bitcast(y.astype(BF16), u32)

    @pl.when(g8 == 7)
    def _():
        for k in range(q // 8):
            for i8 in range(8):
                out = jnp.zeros((r // 2, 128), u32)
                for gg in range(8):
                    yk = ys_s[gg, :, k * 128:(k + 1) * 128]
                    rot = yk if gg == i8 else pltpu.roll(yk, (gg - i8) * S5_GROUP % 128, 1)
                    out = jnp.where(lane_blk == gg, rot, out)
                tok_s[pl.ds(k * 8 + i8, r, stride=q), :] = pltpu.bitcast(out, BF16).astype(F32)
        o_ref[...] = tok_s[...].astype(o_ref.dtype)


def _s5_weights(lam_re, lam_im, log_dt, b_re, b_im, c_re, c_im, d):
    q = S5_CHUNK
    g, p, s = S5_GROUPS, S5_STATE, S5_GROUP
    hi = lax.Precision.HIGHEST
    lr, li = lam_re.astype(F32), lam_im.astype(F32)
    dt = jnp.exp(log_dt.astype(F32))[..., None]
    tau = jnp.arange(q + 1, dtype=F32)[None, :, None, None]
    mag = jnp.exp((lr * dt)[:, None] * tau)
    ang = (li * dt)[:, None] * tau
    ar, ai = mag * jnp.cos(ang), mag * jnp.sin(ang)
    den = lr * lr + li * li
    fr = ((ar[:, 1] - 1.0) * lr + ai[:, 1] * li) / den
    fi = (ai[:, 1] * lr - (ar[:, 1] - 1.0) * li) / den
    br, bi = b_re.astype(F32), b_im.astype(F32)
    bbr = fr[..., None] * br - fi[..., None] * bi
    bbi = fr[..., None] * bi + fi[..., None] * br
    cr, ci = c_re.astype(F32)[:, None], c_im.astype(F32)[:, None]
    car = cr * ar[:, :, :, None, :] - ci * ai[:, :, :, None, :]
    cai = cr * ai[:, :, :, None, :] + ci * ar[:, :, :, None, :]
    bbr_t = jnp.swapaxes(bbr, 2, 3)[:, None, :, None]
    bbi_t = jnp.swapaxes(bbi, 2, 3)[:, None, :, None]
    kern = jnp.sum(car[:, :q, :, :, None, :] * bbr_t - cai[:, :q, :, :, None, :] * bbi_t,
                   axis=-1)
    ii = np.arange(q)
    lag = ii[None, :, None] - ii[:, None, None]
    sel_f = jnp.asarray(lag == ii[None, None, :], F32)
    sel_b = jnp.asarray(-lag == ii[None, None, :], F32)
    tmat = (jnp.einsum('jit,tgos->gjsio', sel_f, kern[0], precision=hi)
            + jnp.einsum('jit,tgos->gjsio', sel_b, kern[1], precision=hi))
    eye = jnp.eye(q, dtype=F32)[:, None, :, None] * jnp.eye(s, dtype=F32)[None, :, None, :]
    tmat = tmat + eye[None] * d.astype(F32).reshape(g, 1, s, 1, 1)
    tmat = tmat.reshape(g, q * s, q * s)

    def chunk_in(are, aim, bre, bim):
        re = jnp.einsum('jgp,gpi->gjip', are, bre) - jnp.einsum('jgp,gpi->gjip', aim, bim)
        im = jnp.einsum('jgp,gpi->gjip', are, bim) + jnp.einsum('jgp,gpi->gjip', aim, bre)
        return jnp.concatenate([re, im, im, re], axis=-1)

    uh = jnp.concatenate([chunk_in(ar[0, :q][::-1], ai[0, :q][::-1], bbr[0], bbi[0]),
                          chunk_in(ar[1, :q], ai[1, :q], bbr[1], bbi[1])], axis=-1).reshape(g, q * s, 8 * p)

    def state_out(re, im):
        return jnp.concatenate([jnp.transpose(re, (1, 3, 0, 2)), -jnp.transpose(im, (1, 3, 0, 2))], axis=1)

    hy = jnp.concatenate([state_out(car[0, 1:], cai[0, 1:]),
                          state_out(car[1, 1:][::-1], cai[1, 1:][::-1])], axis=1).reshape(g, 4 * p, q * s)
    pm = jnp.stack([jnp.concatenate([ar[0, q], ar[0, q]], -1), jnp.concatenate([-ai[0, q], ai[0, q]], -1),
                    jnp.concatenate([ar[1, q], ar[1, q]], -1), jnp.concatenate([-ai[1, q], ai[1, q]], -1)],
                   axis=1)
    pm = jnp.concatenate([pm, jnp.zeros_like(pm)], axis=1)
    return tmat.astype(BF16), uh.astype(BF16), hy.astype(BF16), pm


def _s5(proj, weights, tc):
    b, tt, _ = proj.shape
    q, g, s = S5_CHUNK, S5_GROUPS, S5_GROUP
    nch = tt // q
    tmat, uh, hy, pm = weights
    r = nch * b
    rows = b * tt
    y = pl.pallas_call(
        functools.partial(_s5_kernel, nb=b, nch=nch, nctx=tc // q),
        grid=(g,),
        in_specs=[_once((rows, 128), lambda gi: (0, OFF_U // 128 + gi // 8)),
                  pl.BlockSpec((1, q * s, q * s), lambda gi: (gi, 0, 0)),
                  pl.BlockSpec((1, q * s, 512), lambda gi: (gi, 0, 0)),
                  pl.BlockSpec((1, 256, q * s), lambda gi: (gi, 0, 0)),
                  pl.BlockSpec((1, 8, 128), lambda gi: (gi, 0, 0))],
        out_specs=pl.BlockSpec((rows, 128), lambda gi: (0, gi // 8)),
        out_shape=jax.ShapeDtypeStruct((rows, BRANCH_W), BF16),
        scratch_shapes=[pltpu.VMEM((rows, 128), F32), pltpu.VMEM((q, r // 2, 128), jnp.uint32),
                        pltpu.VMEM((8, r // 2, q * s), jnp.uint32)] + [pltpu.VMEM((r, 128), F32)] * 6,
        compiler_params=_params("arbitrary"),
        name="s5",
    )(proj.reshape(rows, N_PACK), tmat, uh, hy, pm)
    return y.reshape(b, tt, BRANCH_W)


def _ssd_kernel(*refs, backward, first, nctx_chunks, nchunks):
    if first:
        (xbc_ref, prev_ref, next_ref, z_ref, dt_ref, cw_ref, cb_ref, dtb_ref, alog_ref, dsk_ref,
         sel_ref, o_ref, st_s) = refs
        yin_ref = ng_ref = None
    else:
        (xbc_ref, prev_ref, next_ref, z_ref, dt_ref, cw_ref, cb_ref, dtb_ref, alog_ref, dsk_ref,
         sel_ref, yin_ref, ng_ref, o_ref, st_s) = refs
    qn = SSD_CHUNK
    step = pl.program_id(1)
    if backward:
        c = jnp.where(step < nctx_chunks, nctx_chunks - 1 - step, nchunks - 1 - (step - nctx_chunks))
    else:
        c = step

    @pl.when(step == 0)
    def _():
        st_s[...] = jnp.zeros_like(st_s)

    x = xbc_ref[0].astype(F32)
    has_prev = jnp.logical_and(c != 0, c != nctx_chunks)
    has_next = jnp.logical_and(c != nctx_chunks - 1, c != nchunks - 1)
    prow = jnp.where(has_prev, prev_ref[0, 7:8, :].astype(F32), 0.0)
    nrow = jnp.where(has_next, next_ref[0, 0:1, :].astype(F32), 0.0)
    rid = lax.broadcasted_iota(jnp.int32, (qn, 1), 0)
    xm = jnp.where(rid == 0, prow, pltpu.roll(x, 1, 0))
    xp = jnp.where(rid == qn - 1, nrow, pltpu.roll(x, qn - 1, 0))
    cw = cw_ref[...]
    conv = xm * cw[0:1] + x * cw[1:2] + xp * cw[2:3] + cb_ref[...]
    act = _silu(conv)
    xs = act[:, :BRANCH_W]
    gn = SSD_GROUPS * SSD_STATE
    bm = act[:, BRANCH_W:BRANCH_W + gn].astype(BF16)
    cm = act[:, BRANCH_W + gn:].astype(BF16)

    dt = _softplus(dt_ref[0].astype(F32) + dtb_ref[...])
    a = -jnp.exp(alog_ref[...])
    da = dt * a
    ri = lax.broadcasted_iota(jnp.int32, (qn, qn), 0)
    ci = lax.broadcasted_iota(jnp.int32, (qn, qn), 1)
    causal = (ci >= ri) if backward else (ci <= ri)
    ones_tri = jnp.where(causal, 1.0, 0.0).astype(BF16)
    d1 = da.astype(BF16)
    r1 = da - d1.astype(F32)
    d2 = r1.astype(BF16)
    d3 = (r1 - d2.astype(F32)).astype(BF16)
    cum = (jnp.dot(ones_tri, d1, preferred_element_type=F32)
           + jnp.dot(ones_tri, d2, preferred_element_type=F32)
           + jnp.dot(ones_tri, d3, preferred_element_type=F32))
    edge = cum[0:1] if backward else cum[qn - 1:qn]
    cum_t = cum.T
    dt_t = dt.T
    w_edge = dt * jnp.exp(edge - cum)
    e_in = jnp.exp(cum)

    def expand(v):
        v1 = v.astype(BF16)
        v2 = (v - v1.astype(F32)).astype(BF16)
        return (jnp.dot(v1, sel_ref[...], preferred_element_type=F32)
                + jnp.dot(v2, sel_ref[...], preferred_element_type=F32))

    xw = (xs * expand(w_edge)).astype(BF16)
    e_in_x = expand(e_in)
    xs_b = xs.astype(BF16)
    rpg = SSD_HEADS // SSD_GROUPS
    gw = rpg * SSD_HEAD_DIM
    ys = []
    for g in range(SSD_GROUPS):
        bg = bm[:, g * SSD_STATE:(g + 1) * SSD_STATE]
        cg = cm[:, g * SSD_STATE:(g + 1) * SSD_STATE]
        cb = lax.dot_general(cg, bg, (((1,), (1,)), ((), ())), preferred_element_type=F32)
        st_g = st_s[g * gw:(g + 1) * gw, :]
        y_off = lax.dot_general(cg, st_g.astype(BF16), (((1,), (1,)), ((), ())),
                                preferred_element_type=F32)
        yg = y_off * e_in_x[:, g * gw:(g + 1) * gw]
        parts = []
        for r in range(rpg):
            h = g * rpg + r
            dec = jnp.exp(jnp.where(causal, cum[:, h:h + 1] - cum_t[h:h + 1, :], -jnp.inf))
            wmat = (cb * dec * dt_t[h:h + 1, :]).astype(BF16)
            parts.append(jnp.dot(wmat, xs_b[:, h * SSD_HEAD_DIM:(h + 1) * SSD_HEAD_DIM],
                                 preferred_element_type=F32))
        ys.append(yg + jnp.concatenate(parts, axis=1))
        new = lax.dot_general(xw[:, g * gw:(g + 1) * gw], bg, (((0,), (0,)), ((), ())),
                              preferred_element_type=F32)
        for r in range(rpg):
            h = g * rpg + r
            rows = slice(g * gw + r * SSD_HEAD_DIM, g * gw + (r + 1) * SSD_HEAD_DIM)
            cd = jnp.exp(edge[:, h:h + 1])
            st_s[rows, :] = st_s[rows, :] * cd + new[r * SSD_HEAD_DIM:(r + 1) * SSD_HEAD_DIM, :]
    y = jnp.concatenate(ys, axis=1)
    if first:
        o_ref[0] = y + xs * dsk_ref[...]
    else:
        y = y + yin_ref[0]
        zz = z_ref[0].astype(F32)
        y = y * _silu(zz)
        y = y * lax.rsqrt(jnp.mean(y * y, axis=-1, keepdims=True) + RMS_EPS) * ng_ref[...]
        o_ref[0] = y.astype(o_ref.dtype)


def _ssd_pass(proj, conv_w, conv_b, dt_bias, a_log, d_skip, sel, yin, norm_g, tc, backward):
    b, tt, _ = proj.shape
    qn = SSD_CHUNK
    nchunks = tt // qn
    nctx = tc // qn
    first = yin is None
    last_blk8 = tt // 8 - 1

    def chunk_of(s):
        if backward:
            return jnp.where(s < nctx, nctx - 1 - s, nchunks - 1 - (s - nctx))
        return s

    xbc_blk = OFF_XBC // SSD_CONV_CH
    in_specs = [
        pl.BlockSpec((1, qn, SSD_CONV_CH), lambda bi, s: (bi, chunk_of(s), xbc_blk)),
        pl.BlockSpec((1, 8, SSD_CONV_CH),
                     lambda bi, s: (bi, jnp.maximum(chunk_of(s) * (qn // 8) - 1, 0), xbc_blk)),
        pl.BlockSpec((1, 8, SSD_CONV_CH),
                     lambda bi, s: (bi, jnp.minimum((chunk_of(s) + 1) * (qn // 8), last_blk8), xbc_blk)),
        pl.BlockSpec((1, qn, BRANCH_W), lambda bi, s: (bi, chunk_of(s), OFF_Z // BRANCH_W)),
        pl.BlockSpec((1, qn, 128), lambda bi, s: (bi, chunk_of(s), OFF_DT // 128)),
        pl.BlockSpec((8, SSD_CONV_CH), lambda bi, s: (0, 0)),
        pl.BlockSpec((1, SSD_CONV_CH), lambda bi, s: (0, 0)),
        pl.BlockSpec((1, 128), lambda bi, s: (0, 0)),
        pl.BlockSpec((1, 128), lambda bi, s: (0, 0)),
        pl.BlockSpec((1, BRANCH_W), lambda bi, s: (0, 0)),
        pl.BlockSpec((128, BRANCH_W), lambda bi, s: (0, 0)),
    ]
    args = [proj, proj, proj, proj, proj, conv_w, conv_b, dt_bias, a_log, d_skip, sel]
    if not first:
        in_specs += [pl.BlockSpec((1, qn, BRANCH_W), lambda bi, s: (bi, chunk_of(s), 0)),
                     pl.BlockSpec((1, BRANCH_W), lambda bi, s: (0, 0))]
        args += [yin, norm_g]
    return pl.pallas_call(
        functools.partial(_ssd_kernel, backward=backward, first=first, nctx_chunks=nctx, nchunks=nchunks),
        grid=(b, nchunks),
        in_specs=in_specs,
        out_specs=pl.BlockSpec((1, qn, BRANCH_W), lambda bi, s: (bi, chunk_of(s), 0)),
        out_shape=jax.ShapeDtypeStruct((b, tt, BRANCH_W), F32 if first else BF16),
        scratch_shapes=[pltpu.VMEM((SSD_HEADS * SSD_HEAD_DIM, SSD_STATE), F32)],
        compiler_params=_params("arbitrary", "arbitrary"),
        name="ssd_bwd" if backward else "ssd_fwd",
    )(*args)


def _ssd(proj, conv_w, conv_b, dt_bias, a_log, d_skip, norm_g, tc):
    def lanes(v):
        return jnp.pad(v.astype(F32), (0, 128 - SSD_HEADS)).reshape(1, 128)

    cw = jnp.pad(conv_w.astype(F32), ((0, 5), (0, 0)))
    cb = conv_b.astype(F32).reshape(1, -1)
    dsk = jnp.repeat(d_skip.astype(F32), SSD_HEAD_DIM).reshape(1, -1)
    sel = (jnp.arange(128)[:, None] == (jnp.arange(BRANCH_W)[None, :] // SSD_HEAD_DIM)).astype(BF16)
    y1 = _ssd_pass(proj, cw, cb, lanes(dt_bias[0]), lanes(a_log[0]), dsk, sel, None, None, tc, False)
    return _ssd_pass(proj, cw, cb, lanes(dt_bias[1]), lanes(a_log[1]), jnp.zeros_like(dsk), sel,
                     y1, norm_g.astype(F32).reshape(1, -1), tc, True)


def _merge_kernel(ya_ref, ga_ref, yb_ref, gb_ref, yc_ref, yd_ref, gd_ref, gm_ref, bm_ref,
                  wb_ref, wg_ref, bg_ref, o_ref):
    def f(ref):
        return ref[0].astype(F32)

    g = _gelu_tanh(f(yb_ref))
    glu = g * _sigmoid(jnp.dot(g.astype(BF16), wg_ref[...], preferred_element_type=F32) + bg_ref[...])
    branches = (f(ya_ref) * _silu(f(ga_ref)), glu * _silu(f(gb_ref)), f(yc_ref), f(yd_ref) * _silu(f(gd_ref)))
    acc = None
    for n, br in enumerate(branches):
        gate = _sigmoid(gm_ref[0, :, n * D_MODEL:(n + 1) * D_MODEL].astype(F32) + bm_ref[n:n + 1, :])
        term = gate * jnp.dot(br.astype(BF16), wb_ref[n], preferred_element_type=F32)
        acc = term if acc is None else acc + term
    o_ref[0] = acc.astype(o_ref.dtype)


def _merge(proj, ya, yb, yc, yd, b_merge, w_branch, w_glu, b_glu):
    b, tt, _ = proj.shape
    tm = Q_BLOCK
    bw = BRANCH_W

    def pspec(off):
        return pl.BlockSpec((1, tm, bw), lambda bi, i: (bi, i, off // bw))

    yspec = pl.BlockSpec((1, tm, bw), lambda bi, i: (bi, i, 0))
    once = dict(pipeline_mode=pl.Buffered(1))
    return pl.pallas_call(
        _merge_kernel,
        grid=(b, tt // tm),
        in_specs=[yspec, pspec(OFF_GA), yspec, pspec(OFF_GB), yspec, yspec, pspec(OFF_GD),
                  pl.BlockSpec((1, tm, N_BRANCH * D_MODEL), lambda bi, i: (bi, i, 0)),
                  pl.BlockSpec((N_BRANCH, D_MODEL), lambda bi, i: (0, 0)),
                  pl.BlockSpec((N_BRANCH, bw, D_MODEL), lambda bi, i: (0, 0, 0), **once),
                  pl.BlockSpec((bw, bw), lambda bi, i: (0, 0), **once),
                  pl.BlockSpec((1, bw), lambda bi, i: (0, 0))],
        out_specs=pl.BlockSpec((1, tm, D_MODEL), lambda bi, i: (bi, i, 0)),
        out_shape=jax.ShapeDtypeStruct((b, tt, D_MODEL), BF16),
        compiler_params=_params("arbitrary", "arbitrary"),
        name="merge",
    )(ya, proj, yb, proj, yc, yd, proj, proj, b_merge, w_branch, w_glu, b_glu.reshape(1, -1))


def _out_kernel(m_ref, x_ref, mod_ref, w_ref, lg_ref, lb_ref, o_ref, *, tc, tm, row0):
    i = pl.program_id(1)
    out = jnp.dot(m_ref[0], w_ref[...], preferred_element_type=F32)
    row = row0 + i * tm + lax.broadcasted_iota(jnp.int32, (tm, 1), 0)
    m = mod_ref[0]
    gate = jnp.where(row < tc, m[4:5], m[5:6])
    v = DEEPNORM_ALPHA * x_ref[0] + gate * out
    mu = jnp.mean(v, axis=-1, keepdims=True)
    vc = v - mu
    var = jnp.mean(vc * vc, axis=-1, keepdims=True)
    o_ref[0] = vc * lax.rsqrt(var + LN_EPS) * lg_ref[...] + lb_ref[...]


def _out(merged, xa, mod, w_out, ln_g, ln_b, tc, latent_only):
    b, tt, d = xa.shape
    tm = Q_BLOCK
    skip = tc // tm if latent_only else 0
    nblk = tt // tm - skip
    spec = pl.BlockSpec((1, tm, d), lambda bi, i: (bi, i + skip, 0))
    return pl.pallas_call(
        functools.partial(_out_kernel, tc=tc, tm=tm, row0=skip * tm),
        grid=(b, nblk),
        in_specs=[spec, spec,
                  pl.BlockSpec((1, 8, d), lambda bi, i: (bi, 0, 0)),
                  pl.BlockSpec((d, d), lambda bi, i: (0, 0)),
                  pl.BlockSpec((1, d), lambda bi, i: (0, 0)),
                  pl.BlockSpec((1, d), lambda bi, i: (0, 0))],
        out_specs=pl.BlockSpec((1, tm, d), lambda bi, i: (bi, i, 0)),
        out_shape=jax.ShapeDtypeStruct((b, nblk * tm, d), F32),
        compiler_params=_params("arbitrary", "arbitrary"),
        name="out_proj",
    )(merged, xa, mod, w_out, ln_g.reshape(1, -1), ln_b.reshape(1, -1))


def _half_split(w, heads):
    k = w.shape[0]
    return w.reshape(k, heads, ROPE_DIM // 2, 2).transpose(0, 1, 3, 2).reshape(k, heads * ROPE_DIM)


def _pack_w_in(w):
    sp = np.cumsum([0, MLA_Q_LORA, MLA_KV_LORA, MLA_ROPE, BRANCH_W, BRANCH_W, BRANCH_W, BRANCH_W,
                    SSD_CONV_CH, SSD_HEADS, BRANCH_W, BRANCH_W, BRANCH_W, BRANCH_W, N_BRANCH * D_MODEL])
    seg = [w[:, sp[k]:sp[k + 1]] for k in range(14)]
    cq, ckv, kr, ga, u, gb, z, xbc, dt, qd, kd, vd, gd, gm = seg
    kr_hs = _half_split(kr, 1)
    kr_sw = jnp.concatenate([kr_hs[:, 32:], kr_hs[:, :32]], axis=1)
    pad = jnp.zeros((w.shape[0], 128 - SSD_HEADS), w.dtype)
    packed = jnp.concatenate([gm, xbc, ga, u, gb, z, _half_split(qd, 2 * DIFF_HEADS),
                              _half_split(kd, 2 * DIFF_HEADS), vd, gd, cq, ckv, kr_hs, kr_sw, dt, pad],
                             axis=1)
    return packed.astype(BF16)


def _pack_mla(w_uq, w_ukv):
    k = w_uq.shape[0]
    wq = w_uq.reshape(k, MLA_HEADS, MLA_NOPE + MLA_ROPE)
    rope = wq[:, :, MLA_NOPE:].reshape(k, MLA_HEADS, ROPE_DIM // 2, 2)
    ev, od = rope[..., 0], rope[..., 1]
    wq = jnp.concatenate([wq[:, :, :MLA_NOPE], ev, od, od, ev], axis=-1)
    wkv = w_ukv.reshape(w_ukv.shape[0], MLA_HEADS, MLA_NOPE + MLA_V)
    return (wq.transpose(1, 0, 2).astype(BF16),
            wkv[:, :, :MLA_NOPE].transpose(1, 0, 2).astype(BF16),
            wkv[:, :, MLA_NOPE:].transpose(1, 0, 2).astype(BF16))


def _rope_tables(tl, tc):
    rows = tl // GRID_W
    row_id = jnp.repeat(jnp.arange(rows, dtype=F32), GRID_W)
    col_id = jnp.tile(jnp.arange(GRID_W, dtype=F32), rows)
    quarter = ROPE_DIM // 4
    inv_freq = ROPE_BASE ** (-jnp.arange(quarter, dtype=F32) / quarter)
    ang = jnp.concatenate([row_id[:, None] * inv_freq, col_id[:, None] * inv_freq], axis=-1)
    cos = jnp.concatenate([jnp.ones((tc, ROPE_DIM // 2), F32), jnp.cos(ang)], axis=0)
    sin = jnp.concatenate([jnp.zeros((tc, ROPE_DIM // 2), F32), jnp.sin(ang)], axis=0)
    zero = jnp.zeros_like(cos)
    mla = (jnp.concatenate([cos, cos, zero, zero], axis=1), jnp.concatenate([zero, zero, -sin, sin], axis=1))
    diff = (jnp.concatenate([cos, cos, cos, cos], axis=1), jnp.concatenate([-sin, sin, -sin, sin], axis=1))
    return mla, diff


def _layer(xa, cs, p, tables, layer_idx, tc, last):
    b = xa.shape[0]
    mla_tab, diff_tab = tables
    mod = _ada(cs, p['w_ada_stacked'], p['b_ada'], layer_idx)
    d = D_MODEL
    shift, scale, gate = mod[:, :d], mod[:, d:2 * d], mod[:, 2 * d:]
    zeros = jnp.zeros((b, d), F32)
    modb = jnp.stack([jnp.broadcast_to(scale[0], (b, d)), jnp.broadcast_to(shift[0], (b, d)),
                      scale[1:1 + b], shift[1:1 + b],
                      jnp.broadcast_to(gate[0], (b, d)), gate[1:1 + b], zeros, zeros], axis=1)
    proj = _inproj(xa, modb, _pack_w_in(p['w_in']), tc)
    wq, wk, wv = _pack_mla(p['mla_w_uq'], p['mla_w_ukv'])
    ya = _mla(proj, wq, wk, wv, p['mla_q_norm'], p['mla_kv_norm'], mla_tab[0], mla_tab[1], tc)
    lam_init = 0.8 - 0.6 * math.exp(-0.3 * layer_idx)
    yd = _diff(proj, p['diff_lambda_q'], p['diff_lambda_k'], p['diff_norm'], diff_tab[0], diff_tab[1],
               lam_init, tc)
    yb = _s5(proj, _s5_weights(p['s5_lambda_re'], p['s5_lambda_im'], p['s5_log_dt'], p['s5_b_re'],
                               p['s5_b_im'], p['s5_c_re'], p['s5_c_im'], p['s5_d']), tc)
    yc = _ssd(proj, p['ssd_conv_w'], p['ssd_conv_b'], p['ssd_dt_bias'], p['ssd_a_log'], p['ssd_d'],
              p['ssd_norm'], tc)
    merged = _merge(proj, ya, yb, yc, yd, p['b_merge'], p['w_branch'].astype(BF16),
                    p['s5_w_glu'].astype(BF16), p['s5_b_glu'])
    return _out(merged, xa, modb, p['w_out'].astype(BF16), p['ln_g'], p['ln_b'], tc, last)


def kernel(x, c, ctx, c_ctx, w_ada, b_ada, w_in, mla_q_norm, mla_w_uq, mla_kv_norm, mla_w_ukv,
           s5_lambda_re, s5_lambda_im, s5_log_dt, s5_b_re, s5_b_im, s5_c_re, s5_c_im, s5_d,
           s5_w_glu, s5_b_glu, ssd_conv_w, ssd_conv_b, ssd_dt_bias, ssd_a_log, ssd_d, ssd_norm,
           diff_lambda_q, diff_lambda_k, diff_norm, b_merge, w_branch, w_out, ln_g, ln_b):
    b, tl, d = x.shape
    tc = ctx.shape[1]
    stacked = dict(
        w_ada=w_ada, b_ada=b_ada, w_in=w_in, mla_q_norm=mla_q_norm, mla_w_uq=mla_w_uq,
        mla_kv_norm=mla_kv_norm, mla_w_ukv=mla_w_ukv, s5_lambda_re=s5_lambda_re, s5_lambda_im=s5_lambda_im,
        s5_log_dt=s5_log_dt, s5_b_re=s5_b_re, s5_b_im=s5_b_im, s5_c_re=s5_c_re, s5_c_im=s5_c_im, s5_d=s5_d,
        s5_w_glu=s5_w_glu, s5_b_glu=s5_b_glu, ssd_conv_w=ssd_conv_w, ssd_conv_b=ssd_conv_b,
        ssd_dt_bias=ssd_dt_bias, ssd_a_log=ssd_a_log, ssd_d=ssd_d, ssd_norm=ssd_norm,
        diff_lambda_q=diff_lambda_q, diff_lambda_k=diff_lambda_k, diff_norm=diff_norm, b_merge=b_merge,
        w_branch=w_branch, w_out=w_out, ln_g=ln_g, ln_b=ln_b)
    tables = _rope_tables(tl, tc)
    cs = jnp.concatenate([c_ctx[None], c, jnp.zeros((8 - 1 - b, d), c.dtype)], axis=0)
    xa = jnp.concatenate([ctx, x], axis=1)
    depth = w_in.shape[0]
    for i in range(depth):
        p = {k: v[i] for k, v in stacked.items() if k != 'w_ada'}
        p['w_ada_stacked'] = w_ada
        xa = _layer(xa, cs, p, tables, i, tc, i == depth - 1)
    return xa
```

```python
import functools
import math

import numpy as np
import jax
import jax.numpy as jnp
from jax import lax
from jax.experimental import pallas as pl
from jax.experimental.pallas import tpu as pltpu

F32 = jnp.float32
BF16 = jnp.bfloat16

D_MODEL = 2048
DEPTH = 2
GRID_W = 64
N_BRANCH = 4
BRANCH_W = 1024
ROPE_DIM = 64
ROPE_BASE = 10000.0

MLA_HEADS = 8
MLA_NOPE = 128
MLA_ROPE = ROPE_DIM
MLA_V = 128
MLA_Q_LORA = 512
MLA_KV_LORA = 256

S5_GROUP = 16
S5_GROUPS = BRANCH_W // S5_GROUP
S5_STATE = 64
S5_CHUNK = 16

SSD_HEAD_DIM = 64
SSD_HEADS = BRANCH_W // SSD_HEAD_DIM
SSD_GROUPS = 4
SSD_STATE = 128
SSD_CHUNK = 128
SSD_CONV_CH = BRANCH_W + 2 * SSD_GROUPS * SSD_STATE

DIFF_HEAD_DIM = ROPE_DIM
DIFF_HEADS = BRANCH_W // (2 * DIFF_HEAD_DIM)

LN_EPS = 1e-5
RMS_EPS = 1e-6
DEEPNORM_ALPHA = (2 * DEPTH) ** 0.25

V7X_VMEM_LIMIT_BYTES = 56 * 1024 * 1024
Q_BLOCK = 256
ATTN_HEADS_PER_STEP = 2
LOG2E = math.log2(math.e)

OFF_GM = 0
OFF_XBC = 8192
OFF_GA = 10240
OFF_U = 11264
OFF_GB = 12288
OFF_Z = 13312
OFF_QD = 14336
OFF_KD = 15360
OFF_VD = 16384
OFF_GD = 17408
OFF_CQ = 18432
OFF_CKV = 18944
OFF_KR = 19200
OFF_DT = 19328
N_PACK = 19456
IN_TILE_N = 1024


def _sigmoid(x):
    return 0.5 * (1.0 + jnp.tanh(0.5 * x))


def _silu(x):
    return x * _sigmoid(x)


def _gelu_tanh(x):
    return 0.5 * x * (1.0 + jnp.tanh(math.sqrt(2.0 / math.pi) * (x + 0.044715 * (x * x * x))))


def _softplus(x):
    return jnp.maximum(x, 0.0) + jnp.log(1.0 + jnp.exp(-jnp.abs(x)))


def _params(*sem):
    return pltpu.CompilerParams(dimension_semantics=sem, vmem_limit_bytes=V7X_VMEM_LIMIT_BYTES)


def _ada_kernel(c_ref, w_ref, b_ref, o_ref):
    s = _silu(c_ref[...])
    o_ref[...] = jnp.dot(s.astype(BF16), w_ref[0].astype(BF16),
                         preferred_element_type=F32) + b_ref[...]


def _ada(cs, w_ada, b_ada, layer):
    n = w_ada.shape[2]
    tn = 512
    return pl.pallas_call(
        _ada_kernel,
        grid=(n // tn,),
        in_specs=[pl.BlockSpec((8, D_MODEL), lambda j: (0, 0)),
                  pl.BlockSpec((1, D_MODEL, tn), lambda j: (layer, 0, j)),
                  pl.BlockSpec((1, tn), lambda j: (0, j))],
        out_specs=pl.BlockSpec((8, tn), lambda j: (0, j)),
        out_shape=jax.ShapeDtypeStruct((8, n), F32),
        compiler_params=_params("arbitrary"),
        name="ada",
    )(cs, w_ada, b_ada.reshape(1, n))


def _inproj_kernel(x_ref, mod_ref, w_ref, o_ref, h_ref, *, tc, tm):
    i = pl.program_id(1)

    @pl.when(pl.program_id(2) == 0)
    def _():
        x = x_ref[0]
        mu = jnp.mean(x, axis=-1, keepdims=True)
        xc = x - mu
        var = jnp.mean(xc * xc, axis=-1, keepdims=True)
        xn = xc * lax.rsqrt(var + LN_EPS)
        row = i * tm + lax.broadcasted_iota(jnp.int32, (tm, 1), 0)
        is_ctx = row < tc
        m = mod_ref[0]
        scale = jnp.where(is_ctx, m[0:1], m[2:3])
        shift = jnp.where(is_ctx, m[1:2], m[3:4])
        h_ref[...] = (xn * (1.0 + scale) + shift).astype(BF16)

    o_ref[0] = jnp.dot(h_ref[...], w_ref[...], preferred_element_type=F32).astype(o_ref.dtype)


def _inproj(xa, mod, w_pack, tc):
    b, tt, d = xa.shape
    tm = tt // 4
    tn = IN_TILE_N
    return pl.pallas_call(
        functools.partial(_inproj_kernel, tc=tc, tm=tm),
        grid=(b, tt // tm, N_PACK // tn),
        in_specs=[pl.BlockSpec((1, tm, d), lambda bi, i, j: (bi, i, 0)),
                  pl.BlockSpec((1, 8, d), lambda bi, i, j: (bi, 0, 0)),
                  pl.BlockSpec((d, tn), lambda bi, i, j: (0, j))],
        out_specs=pl.BlockSpec((1, tm, tn), lambda bi, i, j: (bi, i, j)),
        out_shape=jax.ShapeDtypeStruct((b, tt, N_PACK), BF16),
        scratch_shapes=[pltpu.VMEM((tm, d), BF16)],
        compiler_params=_params("arbitrary", "arbitrary", "arbitrary"),
        name="inproj",
    )(xa, mod, w_pack)


def _mla_kernel(cq_ref, ckv_ref, kr_ref, wq_ref, wk_ref, wv_ref, qn_ref, kvn_ref,
                ct_ref, st_ref, ctq_ref, stq_ref, o_ref, k_s, v_s, *, tc, tt, scale, skip_ctx):
    hps = ATTN_HEADS_PER_STEP

    @pl.when(pl.program_id(2) == 0)
    def _():
        ckv = ckv_ref[0].astype(F32)
        r = lax.rsqrt(jnp.mean(ckv * ckv, axis=-1, keepdims=True) + RMS_EPS)
        ckvn = (ckv * r * kvn_ref[...]).astype(BF16)
        kr = kr_ref[0].astype(F32)
        kroped = (kr * ct_ref[...] + pltpu.roll(kr * st_ref[...], 64, 1)).astype(BF16)
        for hh in range(hps):
            k_s[hh, :, 0:MLA_NOPE] = jnp.dot(ckvn, wk_ref[hh], preferred_element_type=F32).astype(BF16)
            k_s[hh, :, MLA_NOPE:] = kroped
            v_s[hh] = jnp.dot(ckvn, wv_ref[hh], preferred_element_type=F32).astype(BF16)

    cq = cq_ref[0].astype(F32)
    r = lax.rsqrt(jnp.mean(cq * cq, axis=-1, keepdims=True) + RMS_EPS)
    cqn = (cq * r * qn_ref[...]).astype(BF16)
    qfs = []
    for hh in range(hps):
        q = jnp.dot(cqn, wq_ref[hh], preferred_element_type=F32)
        qh = q[:, MLA_NOPE:]
        qr = qh * ctq_ref[...] + pltpu.roll(qh * stq_ref[...], 64, 1)
        qfs.append((jnp.concatenate([q[:, :MLA_NOPE], qr], axis=1) * (scale * LOG2E)).astype(BF16))

    def attend(nk):
        outs = []
        for hh in range(hps):
            s = lax.dot_general(qfs[hh], k_s[hh, 0:nk, :], (((1,), (1,)), ((), ())),
                                preferred_element_type=F32)
            p = jnp.exp2(s - jnp.max(s, axis=1, keepdims=True))
            l = jnp.sum(p, axis=1, keepdims=True)
            o = jnp.dot(p.astype(BF16), v_s[hh, 0:nk, :], preferred_element_type=F32)
            outs.append(o * (1.0 / l))
        o_ref[0] = jnp.concatenate(outs, axis=1).astype(o_ref.dtype)

    _attend_blocks(attend, tc, tt, skip_ctx)


def _attend_blocks(attend, tc, tt, skip_ctx):
    if skip_ctx:
        attend(tt)
        return

    @pl.when(pl.program_id(2) == 0)
    def _():
        attend(tc)

    @pl.when(pl.program_id(2) > 0)
    def _():
        attend(tt)


def _once(shape, index_map):
    return pl.BlockSpec(shape, index_map, pipeline_mode=pl.Buffered(1))


def _mla(proj, wq, wk, wv, q_norm, kv_norm, ct, st, tc, skip_ctx):
    b, tt, _ = proj.shape
    tq = Q_BLOCK
    assert tc == tq
    scale = (MLA_NOPE + MLA_ROPE) ** -0.5
    hps = ATTN_HEADS_PER_STEP
    sk = int(skip_ctx)
    return pl.pallas_call(
        functools.partial(_mla_kernel, tc=tc, tt=tt, scale=scale, skip_ctx=skip_ctx),
        grid=(b, MLA_HEADS // hps, tt // tq - sk),
        in_specs=[
            pl.BlockSpec((1, tq, MLA_Q_LORA), lambda bi, h, i: (bi, i + sk, OFF_CQ // MLA_Q_LORA)),
            _once((1, tt, MLA_KV_LORA), lambda bi, h, i: (bi, 0, OFF_CKV // MLA_KV_LORA)),
            _once((1, tt, 128), lambda bi, h, i: (bi, 0, OFF_KR // 128)),
            pl.BlockSpec((hps, MLA_Q_LORA, 256), lambda bi, h, i: (h, 0, 0)),
            pl.BlockSpec((hps, MLA_KV_LORA, MLA_NOPE), lambda bi, h, i: (h, 0, 0)),
            pl.BlockSpec((hps, MLA_KV_LORA, MLA_V), lambda bi, h, i: (h, 0, 0)),
            pl.BlockSpec((1, MLA_Q_LORA), lambda bi, h, i: (0, 0)),
            pl.BlockSpec((1, MLA_KV_LORA), lambda bi, h, i: (0, 0)),
            _once((tt, 128), lambda bi, h, i: (0, 0)),
            _once((tt, 128), lambda bi, h, i: (0, 0)),
            pl.BlockSpec((tq, 128), lambda bi, h, i: (i + sk, 0)),
            pl.BlockSpec((tq, 128), lambda bi, h, i: (i + sk, 0)),
        ],
        out_specs=pl.BlockSpec((1, tq, hps * MLA_V), lambda bi, h, i: (bi, i + sk, h)),
        out_shape=jax.ShapeDtypeStruct((b, tt, BRANCH_W), BF16),
        scratch_shapes=[pltpu.VMEM((hps, tt, 256), BF16), pltpu.VMEM((hps, tt, MLA_V), BF16)],
        compiler_params=_params("arbitrary", "arbitrary", "arbitrary"),
        name="mla_attn",
    )(proj, proj, proj, wq, wk, wv, q_norm.reshape(1, -1), kv_norm.reshape(1, -1), ct, st, ct, st)


def _diff_kernel(q_ref, k_ref, v_ref, lq_ref, lk_ref, g_ref, ct_ref, st_ref, ctq_ref, stq_ref,
                 o_ref, k_s, *, tc, tt, lam_init, skip_ctx):
    hps = ATTN_HEADS_PER_STEP
    lane = lax.broadcasted_iota(jnp.int32, (1, 128), 1)
    first_half = (lane & 32) == 0

    def rope(x, c, s):
        xs = jnp.where(first_half, pltpu.roll(x, 96, 1), pltpu.roll(x, 32, 1))
        return x * c + xs * s

    @pl.when(pl.program_id(2) == 0)
    def _():
        for hh in range(hps):
            k_s[hh] = rope(k_ref[0, :, hh * 128:(hh + 1) * 128].astype(F32), ct_ref[...],
                           st_ref[...]).astype(BF16)

    qs = []
    for hh in range(hps):
        q = rope(q_ref[0, :, hh * 128:(hh + 1) * 128].astype(F32), ctq_ref[...], stq_ref[...]) * (
            DIFF_HEAD_DIM ** -0.5 * LOG2E)
        qs.append((jnp.where(lane < 64, q, 0.0).astype(BF16), jnp.where(lane >= 64, q, 0.0).astype(BF16)))
    lqk = lq_ref[...] * lk_ref[...]
    lam = (jnp.exp(jnp.sum(lqk[0:1], axis=1, keepdims=True))
           - jnp.exp(jnp.sum(lqk[1:2], axis=1, keepdims=True)) + lam_init)

    def attend(nk):
        nt = (((1,), (1,)), ((), ()))
        outs = []
        for hh in range(hps):
            kk = k_s[hh, 0:nk, :]
            s1 = lax.dot_general(qs[hh][0], kk, nt, preferred_element_type=F32)
            s2 = lax.dot_general(qs[hh][1], kk, nt, preferred_element_type=F32)
            e1 = jnp.exp2(s1 - jnp.max(s1, axis=1, keepdims=True))
            e2 = jnp.exp2(s2 - jnp.max(s2, axis=1, keepdims=True))
            l1 = jnp.sum(e1, axis=1, keepdims=True)
            l2 = jnp.sum(e2, axis=1, keepdims=True)
            w = e1 - e2 * (lam * l1 / l2)
            o = jnp.dot(w.astype(BF16), v_ref[0, 0:nk, hh * 128:(hh + 1) * 128],
                        preferred_element_type=F32) * (1.0 / l1)
            y = o * lax.rsqrt(jnp.mean(o * o, axis=-1, keepdims=True) + RMS_EPS)
            outs.append(y * g_ref[...] * (1.0 - lam_init))
        o_ref[0] = jnp.concatenate(outs, axis=1).astype(o_ref.dtype)

    _attend_blocks(attend, tc, tt, skip_ctx)


def _diff(proj, lam_q, lam_k, norm_g, ct, st, lam_init, tc, skip_ctx):
    b, tt, _ = proj.shape
    tq = Q_BLOCK
    assert tc == tq
    hps = ATTN_HEADS_PER_STEP
    hw = 128 * hps
    sk = int(skip_ctx)
    return pl.pallas_call(
        functools.partial(_diff_kernel, tc=tc, tt=tt, lam_init=lam_init, skip_ctx=skip_ctx),
        grid=(b, DIFF_HEADS // hps, tt // tq - sk),
        in_specs=[
            pl.BlockSpec((1, tq, hw), lambda bi, h, i: (bi, i + sk, OFF_QD // hw + h)),
            pl.BlockSpec((1, tt, hw), lambda bi, h, i: (bi, 0, OFF_KD // hw + h)),
            pl.BlockSpec((1, tt, hw), lambda bi, h, i: (bi, 0, OFF_VD // hw + h)),
            pl.BlockSpec((2, DIFF_HEAD_DIM), lambda bi, h, i: (0, 0)),
            pl.BlockSpec((2, DIFF_HEAD_DIM), lambda bi, h, i: (0, 0)),
            pl.BlockSpec((1, 128), lambda bi, h, i: (0, 0)),
            _once((tt, 128), lambda bi, h, i: (0, 0)),
            _once((tt, 128), lambda bi, h, i: (0, 0)),
            pl.BlockSpec((tq, 128), lambda bi, h, i: (i + sk, 0)),
            pl.BlockSpec((tq, 128), lambda bi, h, i: (i + sk, 0)),
        ],
        out_specs=pl.BlockSpec((1, tq, hw), lambda bi, h, i: (bi, i + sk, h)),
        out_shape=jax.ShapeDtypeStruct((b, tt, BRANCH_W), BF16),
        scratch_shapes=[pltpu.VMEM((hps, tt, 128), BF16)],
        compiler_params=_params("arbitrary", "arbitrary", "arbitrary"),
        name="diff_attn",
    )(proj, proj, proj, lam_q, lam_k, norm_g.reshape(1, -1), ct, st, ct, st)


def _s5_kernel(x_ref, t_ref, uh_ref, hy_ref, p_ref, o_ref, tok_s, xf_s, ys_s, vf_s, vfs_s, vb_s, vbs_s,
               hf_s, hb_s, *, nb, nch, nctx):
    q = S5_CHUNK
    r = nb * nch
    g8 = pl.program_id(0) % 8
    lane_blk = lax.broadcasted_iota(jnp.int32, (1, 128), 1) // S5_GROUP
    u32 = jnp.uint32

    @pl.when(g8 == 0)
    def _():
        tok_s[...] = x_ref[...].astype(F32)
        for i in range(q):
            xf_s[i] = pltpu.bitcast(tok_s[pl.ds(i, r, stride=q), :].astype(BF16), u32)

    cols = []
    for k in range(q // 8):
        acc = jnp.zeros((r // 2, 128), u32)
        for i8 in range(8):
            rot = pltpu.roll(xf_s[k * 8 + i8], lax.rem((i8 - g8) * S5_GROUP + 128, 128), 1)
            acc = jnp.where(lane_blk == i8, rot, acc)
        cols.append(acc)
    x = pltpu.bitcast(jnp.concatenate(cols, axis=1), BF16)

    yloc = jnp.dot(x, t_ref[0], preferred_element_type=F32)
    v = jnp.dot(x, uh_ref[0], preferred_element_type=F32)
    vf_s[...] = v[:, 0:128]
    vfs_s[...] = v[:, 128:256]
    vb_s[...] = v[:, 256:384]
    vbs_s[...] = v[:, 384:512]
    pm = p_ref[0]
    zero = jnp.zeros((nb, 128), F32)

    def scan(order, v_ref, vs_ref, h_ref, p1, p2):
        h, hs = zero, zero
        for c in order:
            rows = pl.ds(c, nb, stride=nch)
            h_ref[rows, :] = h
            h, hs = h * p1 + hs * p2 + v_ref[rows, :], hs * p1 - h * p2 + vs_ref[rows, :]

    scan(range(nch), vf_s, vfs_s, hf_s, pm[0:1], pm[1:2])
    scan(list(range(nctx - 1, -1, -1)) + list(range(nch - 1, nctx - 1, -1)), vb_s, vbs_s, hb_s,
         pm[2:3], pm[3:4])
    hcat = jnp.concatenate([hf_s[...], hb_s[...]], axis=1).astype(BF16)
    y = yloc + jnp.dot(hcat, hy_ref[0], preferred_element_type=F32)
    ys_s[g8] = pltpu.bitcast(y.astype(BF16), u32)

    @pl.when(g8 == 7)
    def _():
        for k in range(q // 8):
            for i8 in range(8):
                out = jnp.zeros((r // 2, 128), u32)
                for gg in range(8):
                    yk = ys_s[gg, :, k * 128:(k + 1) * 128]
                    rot = yk if gg == i8 else pltpu.roll(yk, (gg - i8) * S5_GROUP % 128, 1)
                    out = jnp.where(lane_blk == gg, rot, out)
                tok_s[pl.ds(k * 8 + i8, r, stride=q), :] = pltpu.bitcast(out, BF16).astype(F32)
        o_ref[...] = tok_s[...].astype(o_ref.dtype)


def _s5_weights(lam_re, lam_im, log_dt, b_re, b_im, c_re, c_im, d):
    q = S5_CHUNK
    g, p, s = S5_GROUPS, S5_STATE, S5_GROUP
    hi = lax.Precision.HIGHEST
    lr, li = lam_re.astype(F32), lam_im.astype(F32)
    dt = jnp.exp(log_dt.astype(F32))[..., None]
    tau = jnp.arange(q + 1, dtype=F32)[None, :, None, None]
    mag = jnp.exp((lr * dt)[:, None] * tau)
    ang = (li * dt)[:, None] * tau
    ar, ai = mag * jnp.cos(ang), mag * jnp.sin(ang)
    den = lr * lr + li * li
    fr = ((ar[:, 1] - 1.0) * lr + ai[:, 1] * li) / den
    fi = (ai[:, 1] * lr - (ar[:, 1] - 1.0) * li) / den
    br, bi = b_re.astype(F32), b_im.astype(F32)
    bbr = fr[..., None] * br - fi[..., None] * bi
    bbi = fr[..., None] * bi + fi[..., None] * br
    cr, ci = c_re.astype(F32)[:, None], c_im.astype(F32)[:, None]
    car = cr * ar[:, :, :, None, :] - ci * ai[:, :, :, None, :]
    cai = cr * ai[:, :, :, None, :] + ci * ar[:, :, :, None, :]
    car_x = jnp.repeat(jnp.swapaxes(car[:, :q], 3, 4), s, axis=-1)
    cai_x = jnp.repeat(jnp.swapaxes(cai[:, :q], 3, 4), s, axis=-1)
    bbr_x = jnp.tile(bbr, (1, 1, 1, s))[:, None]
    bbi_x = jnp.tile(bbi, (1, 1, 1, s))[:, None]
    kern = jnp.sum(car_x * bbr_x - cai_x * bbi_x, axis=3).reshape(2, q, g, s, s)
    ii = np.arange(q)
    lag = ii[None, :, None] - ii[:, None, None]
    sel_f = jnp.asarray(lag == ii[None, None, :], F32)
    sel_b = jnp.asarray(-lag == ii[None, None, :], F32)
    tmat = (jnp.einsum('jit,tgos->gjsio', sel_f, kern[0], precision=hi)
            + jnp.einsum('jit,tgos->gjsio', sel_b, kern[1], precision=hi))
    eye = jnp.eye(q, dtype=F32)[:, None, :, None] * jnp.eye(s, dtype=F32)[None, :, None, :]
    tmat = tmat + eye[None] * d.astype(F32).reshape(g, 1, s, 1, 1)
    tmat = tmat.reshape(g, q * s, q * s)

    def chunk_in(are, aim, bre, bim):
        re = jnp.einsum('jgp,gpi->gjip', are, bre) - jnp.einsum('jgp,gpi->gjip', aim, bim)
        im = jnp.einsum('jgp,gpi->gjip', are, bim) + jnp.einsum('jgp,gpi->gjip', aim, bre)
        return jnp.concatenate([re, im, im, re], axis=-1)

    uh = jnp.concatenate([chunk_in(ar[0, :q][::-1], ai[0, :q][::-1], bbr[0], bbi[0]),
                          chunk_in(ar[1, :q], ai[1, :q], bbr[1], bbi[1])], axis=-1).reshape(g, q * s, 8 * p)

    def state_out(re, im):
        return jnp.concatenate([jnp.transpose(re, (1, 3, 0, 2)), -jnp.transpose(im, (1, 3, 0, 2))], axis=1)

    hy = jnp.concatenate([state_out(car[0, 1:], cai[0, 1:]),
                          state_out(car[1, 1:][::-1], cai[1, 1:][::-1])], axis=1).reshape(g, 4 * p, q * s)
    pm = jnp.stack([jnp.concatenate([ar[0, q], ar[0, q]], -1), jnp.concatenate([-ai[0, q], ai[0, q]], -1),
                    jnp.concatenate([ar[1, q], ar[1, q]], -1), jnp.concatenate([-ai[1, q], ai[1, q]], -1)],
                   axis=1)
    pm = jnp.concatenate([pm, jnp.zeros_like(pm)], axis=1)
    return tmat.astype(BF16), uh.astype(BF16), hy.astype(BF16), pm


def _s5(proj, weights, tc):
    b, tt, _ = proj.shape
    q, g, s = S5_CHUNK, S5_GROUPS, S5_GROUP
    nch = tt // q
    tmat, uh, hy, pm = weights
    r = nch * b
    rows = b * tt
    y = pl.pallas_call(
        functools.partial(_s5_kernel, nb=b, nch=nch, nctx=tc // q),
        grid=(g,),
        in_specs=[_once((rows, 128), lambda gi: (0, OFF_U // 128 + gi // 8)),
                  pl.BlockSpec((1, q * s, q * s), lambda gi: (gi, 0, 0)),
                  pl.BlockSpec((1, q * s, 512), lambda gi: (gi, 0, 0)),
                  pl.BlockSpec((1, 256, q * s), lambda gi: (gi, 0, 0)),
                  pl.BlockSpec((1, 8, 128), lambda gi: (gi, 0, 0))],
        out_specs=pl.BlockSpec((rows, 128), lambda gi: (0, gi // 8)),
        out_shape=jax.ShapeDtypeStruct((rows, BRANCH_W), BF16),
        scratch_shapes=[pltpu.VMEM((rows, 128), F32), pltpu.VMEM((q, r // 2, 128), jnp.uint32),
                        pltpu.VMEM((8, r // 2, q * s), jnp.uint32)] + [pltpu.VMEM((r, 128), F32)] * 6,
        compiler_params=_params("arbitrary"),
        name="s5",
    )(proj.reshape(rows, N_PACK), tmat, uh, hy, pm)
    return y.reshape(b, tt, BRANCH_W)


def _ssd_kernel(*refs, backward, first, nctx_chunks, nchunks):
    if first:
        (xbc_ref, prev_ref, next_ref, z_ref, dt_ref, cw_ref, cb_ref, dtb_ref, alog_ref, dsk_ref,
         sel_ref, o_ref, st_s) = refs
        yin_ref = ng_ref = None
    else:
        (xbc_ref, prev_ref, next_ref, z_ref, dt_ref, cw_ref, cb_ref, dtb_ref, alog_ref, dsk_ref,
         sel_ref, yin_ref, ng_ref, o_ref, st_s) = refs
    qn = SSD_CHUNK
    step = pl.program_id(1)
    if backward:
        c = jnp.where(step < nctx_chunks, nctx_chunks - 1 - step, nchunks - 1 - (step - nctx_chunks))
    else:
        c = step

    @pl.when(step == 0)
    def _():
        st_s[...] = jnp.zeros_like(st_s)

    x = xbc_ref[0].astype(F32)
    has_prev = jnp.logical_and(c != 0, c != nctx_chunks)
    has_next = jnp.logical_and(c != nctx_chunks - 1, c != nchunks - 1)
    prow = jnp.where(has_prev, prev_ref[0, 7:8, :].astype(F32), 0.0)
    nrow = jnp.where(has_next, next_ref[0, 0:1, :].astype(F32), 0.0)
    rid = lax.broadcasted_iota(jnp.int32, (qn, 1), 0)
    xm = jnp.where(rid == 0, prow, pltpu.roll(x, 1, 0))
    xp = jnp.where(rid == qn - 1, nrow, pltpu.roll(x, qn - 1, 0))
    cw = cw_ref[...]
    conv = xm * cw[0:1] + x * cw[1:2] + xp * cw[2:3] + cb_ref[...]
    act = _silu(conv)
    xs = act[:, :BRANCH_W]
    gn = SSD_GROUPS * SSD_STATE
    bm = act[:, BRANCH_W:BRANCH_W + gn].astype(BF16)
    cm = act[:, BRANCH_W + gn:].astype(BF16)

    dt = _softplus(dt_ref[0].astype(F32) + dtb_ref[...])
    a = -jnp.exp(alog_ref[...])
    da = dt * a
    ri = lax.broadcasted_iota(jnp.int32, (qn, qn), 0)
    ci = lax.broadcasted_iota(jnp.int32, (qn, qn), 1)
    causal = (ci >= ri) if backward else (ci <= ri)
    ones_tri = jnp.where(causal, 1.0, 0.0).astype(BF16)
    d1 = da.astype(BF16)
    r1 = da - d1.astype(F32)
    d2 = r1.astype(BF16)
    d3 = (r1 - d2.astype(F32)).astype(BF16)
    cum = (jnp.dot(ones_tri, d1, preferred_element_type=F32)
           + jnp.dot(ones_tri, d2, preferred_element_type=F32)
           + jnp.dot(ones_tri, d3, preferred_element_type=F32))
    edge = cum[0:1] if backward else cum[qn - 1:qn]
    cum_t = cum.T
    dt_t = dt.T
    w_edge = dt * jnp.exp(edge - cum)
    e_in = jnp.exp(cum)

    def expand(v):
        v1 = v.astype(BF16)
        v2 = (v - v1.astype(F32)).astype(BF16)
        return (jnp.dot(v1, sel_ref[...], preferred_element_type=F32)
                + jnp.dot(v2, sel_ref[...], preferred_element_type=F32))

    xw = (xs * expand(w_edge)).astype(BF16)
    e_in_x = expand(e_in)
    xs_b = xs.astype(BF16)
    rpg = SSD_HEADS // SSD_GROUPS
    gw = rpg * SSD_HEAD_DIM
    ys = []
    for g in range(SSD_GROUPS):
        bg = bm[:, g * SSD_STATE:(g + 1) * SSD_STATE]
        cg = cm[:, g * SSD_STATE:(g + 1) * SSD_STATE]
        cb = lax.dot_general(cg, bg, (((1,), (1,)), ((), ())), preferred_element_type=F32)
        st_g = st_s[g * gw:(g + 1) * gw, :]
        y_off = lax.dot_general(cg, st_g.astype(BF16), (((1,), (1,)), ((), ())),
                                preferred_element_type=F32)
        yg = y_off * e_in_x[:, g * gw:(g + 1) * gw]
        parts = []
        for r in range(rpg):
            h = g * rpg + r
            dec = jnp.exp(jnp.where(causal, cum[:, h:h + 1] - cum_t[h:h + 1, :], -jnp.inf))
            wmat = (cb * dec * dt_t[h:h + 1, :]).astype(BF16)
            parts.append(jnp.dot(wmat, xs_b[:, h * SSD_HEAD_DIM:(h + 1) * SSD_HEAD_DIM],
                                 preferred_element_type=F32))
        ys.append(yg + jnp.concatenate(parts, axis=1))
        new = lax.dot_general(xw[:, g * gw:(g + 1) * gw], bg, (((0,), (0,)), ((), ())),
                              preferred_element_type=F32)
        for r in range(rpg):
            h = g * rpg + r
            rows = slice(g * gw + r * SSD_HEAD_DIM, g * gw + (r + 1) * SSD_HEAD_DIM)
            cd = jnp.exp(edge[:, h:h + 1])
            st_s[rows, :] = st_s[rows, :] * cd + new[r * SSD_HEAD_DIM:(r + 1) * SSD_HEAD_DIM, :]
    y = jnp.concatenate(ys, axis=1)
    if first:
        o_ref[0] = y + xs * dsk_ref[...]
    else:
        y = y + yin_ref[0]
        zz = z_ref[0].astype(F32)
        y = y * _silu(zz)
        y = y * lax.rsqrt(jnp.mean(y * y, axis=-1, keepdims=True) + RMS_EPS) * ng_ref[...]
        o_ref[0] = y.astype(o_ref.dtype)


def _ssd_pass(proj, conv_w, conv_b, dt_bias, a_log, d_skip, sel, yin, norm_g, tc, backward):
    b, tt, _ = proj.shape
    qn = SSD_CHUNK
    nchunks = tt // qn
    nctx = tc // qn
    first = yin is None
    last_blk8 = tt // 8 - 1

    def chunk_of(s):
        if backward:
            return jnp.where(s < nctx, nctx - 1 - s, nchunks - 1 - (s - nctx))
        return s

    xbc_blk = OFF_XBC // SSD_CONV_CH
    in_specs = [
        pl.BlockSpec((1, qn, SSD_CONV_CH), lambda bi, s: (bi, chunk_of(s), xbc_blk)),
        pl.BlockSpec((1, 8, SSD_CONV_CH),
                     lambda bi, s: (bi, jnp.maximum(chunk_of(s) * (qn // 8) - 1, 0), xbc_blk)),
        pl.BlockSpec((1, 8, SSD_CONV_CH),
                     lambda bi, s: (bi, jnp.minimum((chunk_of(s) + 1) * (qn // 8), last_blk8), xbc_blk)),
        pl.BlockSpec((1, qn, BRANCH_W), lambda bi, s: (bi, chunk_of(s), OFF_Z // BRANCH_W)),
        pl.BlockSpec((1, qn, 128), lambda bi, s: (bi, chunk_of(s), OFF_DT // 128)),
        pl.BlockSpec((8, SSD_CONV_CH), lambda bi, s: (0, 0)),
        pl.BlockSpec((1, SSD_CONV_CH), lambda bi, s: (0, 0)),
        pl.BlockSpec((1, 128), lambda bi, s: (0, 0)),
        pl.BlockSpec((1, 128), lambda bi, s: (0, 0)),
        pl.BlockSpec((1, BRANCH_W), lambda bi, s: (0, 0)),
        pl.BlockSpec((128, BRANCH_W), lambda bi, s: (0, 0)),
    ]
    args = [proj, proj, proj, proj, proj, conv_w, conv_b, dt_bias, a_log, d_skip, sel]
    if not first:
        in_specs += [pl.BlockSpec((1, qn, BRANCH_W), lambda bi, s: (bi, chunk_of(s), 0)),
                     pl.BlockSpec((1, BRANCH_W), lambda bi, s: (0, 0))]
        args += [yin, norm_g]
    return pl.pallas_call(
        functools.partial(_ssd_kernel, backward=backward, first=first, nctx_chunks=nctx, nchunks=nchunks),
        grid=(b, nchunks),
        in_specs=in_specs,
        out_specs=pl.BlockSpec((1, qn, BRANCH_W), lambda bi, s: (bi, chunk_of(s), 0)),
        out_shape=jax.ShapeDtypeStruct((b, tt, BRANCH_W), F32 if first else BF16),
        scratch_shapes=[pltpu.VMEM((SSD_HEADS * SSD_HEAD_DIM, SSD_STATE), F32)],
        compiler_params=_params("arbitrary", "arbitrary"),
        name="ssd_bwd" if backward else "ssd_fwd",
    )(*args)


def _ssd(proj, conv_w, conv_b, dt_bias, a_log, d_skip, norm_g, tc):
    def lanes(v):
        return jnp.pad(v.astype(F32), (0, 128 - SSD_HEADS)).reshape(1, 128)

    cw = jnp.pad(conv_w.astype(F32), ((0, 5), (0, 0)))
    cb = conv_b.astype(F32).reshape(1, -1)
    dsk = jnp.repeat(d_skip.astype(F32), SSD_HEAD_DIM).reshape(1, -1)
    sel = (jnp.arange(128)[:, None] == (jnp.arange(BRANCH_W)[None, :] // SSD_HEAD_DIM)).astype(BF16)
    y1 = _ssd_pass(proj, cw, cb, lanes(dt_bias[0]), lanes(a_log[0]), dsk, sel, None, None, tc, False)
    return _ssd_pass(proj, cw, cb, lanes(dt_bias[1]), lanes(a_log[1]), jnp.zeros_like(dsk), sel,
                     y1, norm_g.astype(F32).reshape(1, -1), tc, True)


def _merge_kernel(ya_ref, ga_ref, yb_ref, gb_ref, yc_ref, yd_ref, gd_ref, gm_ref, bm_ref,
                  wb_ref, wg_ref, bg_ref, o_ref):
    def f(ref):
        return ref[0].astype(F32)

    g = _gelu_tanh(f(yb_ref))
    glu = g * _sigmoid(jnp.dot(g.astype(BF16), wg_ref[...], preferred_element_type=F32) + bg_ref[...])
    branches = (f(ya_ref) * _silu(f(ga_ref)), glu * _silu(f(gb_ref)), f(yc_ref), f(yd_ref) * _silu(f(gd_ref)))
    acc = None
    for n, br in enumerate(branches):
        gate = _sigmoid(gm_ref[0, :, n * D_MODEL:(n + 1) * D_MODEL].astype(F32) + bm_ref[n:n + 1, :])
        term = gate * jnp.dot(br.astype(BF16), wb_ref[n], preferred_element_type=F32)
        acc = term if acc is None else acc + term
    o_ref[0] = acc.astype(o_ref.dtype)


def _merge(proj, ya, yb, yc, yd, b_merge, w_branch, w_glu, b_glu, skip):
    b, tt, _ = proj.shape
    tm = Q_BLOCK
    bw = BRANCH_W

    def pspec(off):
        return pl.BlockSpec((1, tm, bw), lambda bi, i: (bi, i + skip, off // bw))

    yspec = pl.BlockSpec((1, tm, bw), lambda bi, i: (bi, i + skip, 0))
    once = dict(pipeline_mode=pl.Buffered(1))
    return pl.pallas_call(
        _merge_kernel,
        grid=(b, tt // tm - skip),
        in_specs=[yspec, pspec(OFF_GA), yspec, pspec(OFF_GB), yspec, yspec, pspec(OFF_GD),
                  pl.BlockSpec((1, tm, N_BRANCH * D_MODEL), lambda bi, i: (bi, i + skip, 0)),
                  pl.BlockSpec((N_BRANCH, D_MODEL), lambda bi, i: (0, 0)),
                  pl.BlockSpec((N_BRANCH, bw, D_MODEL), lambda bi, i: (0, 0, 0), **once),
                  pl.BlockSpec((bw, bw), lambda bi, i: (0, 0), **once),
                  pl.BlockSpec((1, bw), lambda bi, i: (0, 0))],
        out_specs=pl.BlockSpec((1, tm, D_MODEL), lambda bi, i: (bi, i + skip, 0)),
        out_shape=jax.ShapeDtypeStruct((b, tt, D_MODEL), BF16),
        compiler_params=_params("arbitrary", "arbitrary"),
        name="merge",
    )(ya, proj, yb, proj, yc, yd, proj, proj, b_merge, w_branch, w_glu, b_glu.reshape(1, -1))


def _out_kernel(m_ref, x_ref, mod_ref, w_ref, lg_ref, lb_ref, o_ref, *, tc, tm, row0):
    i = pl.program_id(1)
    out = jnp.dot(m_ref[0], w_ref[...], preferred_element_type=F32)
    row = row0 + i * tm + lax.broadcasted_iota(jnp.int32, (tm, 1), 0)
    m = mod_ref[0]
    gate = jnp.where(row < tc, m[4:5], m[5:6])
    v = DEEPNORM_ALPHA * x_ref[0] + gate * out
    mu = jnp.mean(v, axis=-1, keepdims=True)
    vc = v - mu
    var = jnp.mean(vc * vc, axis=-1, keepdims=True)
    o_ref[0] = vc * lax.rsqrt(var + LN_EPS) * lg_ref[...] + lb_ref[...]


def _out(merged, xa, mod, w_out, ln_g, ln_b, tc, latent_only):
    b, tt, d = xa.shape
    tm = Q_BLOCK
    skip = tc // tm if latent_only else 0
    nblk = tt // tm - skip
    spec = pl.BlockSpec((1, tm, d), lambda bi, i: (bi, i + skip, 0))
    return pl.pallas_call(
        functools.partial(_out_kernel, tc=tc, tm=tm, row0=skip * tm),
        grid=(b, nblk),
        in_specs=[spec, spec,
                  pl.BlockSpec((1, 8, d), lambda bi, i: (bi, 0, 0)),
                  pl.BlockSpec((d, d), lambda bi, i: (0, 0)),
                  pl.BlockSpec((1, d), lambda bi, i: (0, 0)),
                  pl.BlockSpec((1, d), lambda bi, i: (0, 0))],
        out_specs=pl.BlockSpec((1, tm, d), lambda bi, i: (bi, i, 0)),
        out_shape=jax.ShapeDtypeStruct((b, nblk * tm, d), F32),
        compiler_params=_params("arbitrary", "arbitrary"),
        name="out_proj",
    )(merged, xa, mod, w_out, ln_g.reshape(1, -1), ln_b.reshape(1, -1))


def _half_split(w, heads):
    k = w.shape[0]
    return w.reshape(k, heads, ROPE_DIM // 2, 2).transpose(0, 1, 3, 2).reshape(k, heads * ROPE_DIM)


def _pack_w_in(w):
    sp = np.cumsum([0, MLA_Q_LORA, MLA_KV_LORA, MLA_ROPE, BRANCH_W, BRANCH_W, BRANCH_W, BRANCH_W,
                    SSD_CONV_CH, SSD_HEADS, BRANCH_W, BRANCH_W, BRANCH_W, BRANCH_W, N_BRANCH * D_MODEL])
    seg = [w[:, sp[k]:sp[k + 1]] for k in range(14)]
    cq, ckv, kr, ga, u, gb, z, xbc, dt, qd, kd, vd, gd, gm = seg
    kr_hs = _half_split(kr, 1)
    kr_sw = jnp.concatenate([kr_hs[:, 32:], kr_hs[:, :32]], axis=1)
    pad = jnp.zeros((w.shape[0], 128 - SSD_HEADS), w.dtype)
    packed = jnp.concatenate([gm, xbc, ga, u, gb, z, _half_split(qd, 2 * DIFF_HEADS),
                              _half_split(kd, 2 * DIFF_HEADS), vd, gd, cq, ckv, kr_hs, kr_sw, dt, pad],
                             axis=1)
    return packed.astype(BF16)


def _pack_mla(w_uq, w_ukv):
    k = w_uq.shape[0]
    wq = w_uq.reshape(k, MLA_HEADS, MLA_NOPE + MLA_ROPE)
    rope = wq[:, :, MLA_NOPE:].reshape(k, MLA_HEADS, ROPE_DIM // 2, 2)
    ev, od = rope[..., 0], rope[..., 1]
    wq = jnp.concatenate([wq[:, :, :MLA_NOPE], ev, od, od, ev], axis=-1)
    wkv = w_ukv.reshape(w_ukv.shape[0], MLA_HEADS, MLA_NOPE + MLA_V)
    return (wq.transpose(1, 0, 2).astype(BF16),
            wkv[:, :, :MLA_NOPE].transpose(1, 0, 2).astype(BF16),
            wkv[:, :, MLA_NOPE:].transpose(1, 0, 2).astype(BF16))


def _rope_tables(tl, tc):
    rows = tl // GRID_W
    row_id = jnp.repeat(jnp.arange(rows, dtype=F32), GRID_W)
    col_id = jnp.tile(jnp.arange(GRID_W, dtype=F32), rows)
    quarter = ROPE_DIM // 4
    inv_freq = ROPE_BASE ** (-jnp.arange(quarter, dtype=F32) / quarter)
    ang = jnp.concatenate([row_id[:, None] * inv_freq, col_id[:, None] * inv_freq], axis=-1)
    cos = jnp.concatenate([jnp.ones((tc, ROPE_DIM // 2), F32), jnp.cos(ang)], axis=0)
    sin = jnp.concatenate([jnp.zeros((tc, ROPE_DIM // 2), F32), jnp.sin(ang)], axis=0)
    zero = jnp.zeros_like(cos)
    mla = (jnp.concatenate([cos, cos, zero, zero], axis=1), jnp.concatenate([zero, zero, -sin, sin], axis=1))
    diff = (jnp.concatenate([cos, cos, cos, cos], axis=1), jnp.concatenate([-sin, sin, -sin, sin], axis=1))
    return mla, diff


def _layer(xa, cs, p, tables, layer_idx, tc, last):
    b = xa.shape[0]
    mla_tab, diff_tab = tables
    mod = _ada(cs, p['w_ada_stacked'], p['b_ada'], layer_idx)
    d = D_MODEL
    shift, scale, gate = mod[:, :d], mod[:, d:2 * d], mod[:, 2 * d:]
    zeros = jnp.zeros((b, d), F32)
    modb = jnp.stack([jnp.broadcast_to(scale[0], (b, d)), jnp.broadcast_to(shift[0], (b, d)),
                      scale[1:1 + b], shift[1:1 + b],
                      jnp.broadcast_to(gate[0], (b, d)), gate[1:1 + b], zeros, zeros], axis=1)
    proj = _inproj(xa, modb, _pack_w_in(p['w_in']), tc)
    wq, wk, wv = _pack_mla(p['mla_w_uq'], p['mla_w_ukv'])
    ya = _mla(proj, wq, wk, wv, p['mla_q_norm'], p['mla_kv_norm'], mla_tab[0], mla_tab[1], tc, last)
    lam_init = 0.8 - 0.6 * math.exp(-0.3 * layer_idx)
    yd = _diff(proj, p['diff_lambda_q'], p['diff_lambda_k'], p['diff_norm'], diff_tab[0], diff_tab[1],
               lam_init, tc, last)
    yb = _s5(proj, _s5_weights(p['s5_lambda_re'], p['s5_lambda_im'], p['s5_log_dt'], p['s5_b_re'],
                               p['s5_b_im'], p['s5_c_re'], p['s5_c_im'], p['s5_d']), tc)
    yc = _ssd(proj, p['ssd_conv_w'], p['ssd_conv_b'], p['ssd_dt_bias'], p['ssd_a_log'], p['ssd_d'],
              p['ssd_norm'], tc)
    merged = _merge(proj, ya, yb, yc, yd, p['b_merge'], p['w_branch'].astype(BF16),
                    p['s5_w_glu'].astype(BF16), p['s5_b_glu'], tc // Q_BLOCK if last else 0)
    return _out(merged, xa, modb, p['w_out'].astype(BF16), p['ln_g'], p['ln_b'], tc, last)


def kernel(x, c, ctx, c_ctx, w_ada, b_ada, w_in, mla_q_norm, mla_w_uq, mla_kv_norm, mla_w_ukv,
           s5_lambda_re, s5_lambda_im, s5_log_dt, s5_b_re, s5_b_im, s5_c_re, s5_c_im, s5_d,
           s5_w_glu, s5_b_glu, ssd_conv_w, ssd_conv_b, ssd_dt_bias, ssd_a_log, ssd_d, ssd_norm,
           diff_lambda_q, diff_lambda_k, diff_norm, b_merge, w_branch, w_out, ln_g, ln_b):
    b, tl, d = x.shape
    tc = ctx.shape[1]
    stacked = dict(
        w_ada=w_ada, b_ada=b_ada, w_in=w_in, mla_q_norm=mla_q_norm, mla_w_uq=mla_w_uq,
        mla_kv_norm=mla_kv_norm, mla_w_ukv=mla_w_ukv, s5_lambda_re=s5_lambda_re, s5_lambda_im=s5_lambda_im,
        s5_log_dt=s5_log_dt, s5_b_re=s5_b_re, s5_b_im=s5_b_im, s5_c_re=s5_c_re, s5_c_im=s5_c_im, s5_d=s5_d,
        s5_w_glu=s5_w_glu, s5_b_glu=s5_b_glu, ssd_conv_w=ssd_conv_w, ssd_conv_b=ssd_conv_b,
        ssd_dt_bias=ssd_dt_bias, ssd_a_log=ssd_a_log, ssd_d=ssd_d, ssd_norm=ssd_norm,
        diff_lambda_q=diff_lambda_q, diff_lambda_k=diff_lambda_k, diff_norm=diff_norm, b_merge=b_merge,
        w_branch=w_branch, w_out=w_out, ln_g=ln_g, ln_b=ln_b)
    tables = _rope_tables(tl, tc)
    cs = jnp.concatenate([c_ctx[None], c, jnp.zeros((8 - 1 - b, d), c.dtype)], axis=0)
    xa = jnp.concatenate([ctx, x], axis=1)
    depth = w_in.shape[0]
    for i in range(depth):
        p = {k: v[i] for k, v in stacked.items() if k != 'w_ada'}
        p['w_ada_stacked'] = w_ada
        xa = _layer(xa, cs, p, tables, i, tc, i == depth - 1)
    return xa
```

```python
import functools
import math

import numpy as np
import jax
import jax.numpy as jnp
from jax import lax
from jax.experimental import pallas as pl
from jax.experimental.pallas import tpu as pltpu

F32 = jnp.float32
BF16 = jnp.bfloat16

D_MODEL = 2048
DEPTH = 2
GRID_W = 64
N_BRANCH = 4
BRANCH_W = 1024
ROPE_DIM = 64
ROPE_BASE = 10000.0

MLA_HEADS = 8
MLA_NOPE = 128
MLA_ROPE = ROPE_DIM
MLA_V = 128
MLA_Q_LORA = 512
MLA_KV_LORA = 256

S5_GROUP = 16
S5_GROUPS = BRANCH_W // S5_GROUP
S5_STATE = 64
S5_CHUNK = 16

SSD_HEAD_DIM = 64
SSD_HEADS = BRANCH_W // SSD_HEAD_DIM
SSD_GROUPS = 4
SSD_STATE = 128
SSD_CHUNK = 128
SSD_CONV_CH = BRANCH_W + 2 * SSD_GROUPS * SSD_STATE

DIFF_HEAD_DIM = ROPE_DIM
DIFF_HEADS = BRANCH_W // (2 * DIFF_HEAD_DIM)

LN_EPS = 1e-5
RMS_EPS = 1e-6
DEEPNORM_ALPHA = (2 * DEPTH) ** 0.25

V7X_VMEM_LIMIT_BYTES = 56 * 1024 * 1024
Q_BLOCK = 256
ATTN_HEADS_PER_STEP = 2
LOG2E = math.log2(math.e)

OFF_GM = 0
OFF_XBC = 8192
OFF_GA = 10240
OFF_U = 11264
OFF_GB = 12288
OFF_Z = 13312
OFF_QD = 14336
OFF_KD = 15360
OFF_VD = 16384
OFF_GD = 17408
OFF_CQ = 18432
OFF_CKV = 18944
OFF_KR = 19200
OFF_DT = 19328
N_PACK = 19456
IN_TILE_N = 1024


def _sigmoid(x):
    return 0.5 * (1.0 + jnp.tanh(0.5 * x))


def _silu(x):
    return x * _sigmoid(x)


def _gelu_tanh(x):
    return 0.5 * x * (1.0 + jnp.tanh(math.sqrt(2.0 / math.pi) * (x + 0.044715 * (x * x * x))))


def _softplus(x):
    return jnp.maximum(x, 0.0) + jnp.log(1.0 + jnp.exp(-jnp.abs(x)))


def _params(*sem):
    return pltpu.CompilerParams(dimension_semantics=sem, vmem_limit_bytes=V7X_VMEM_LIMIT_BYTES)


def _ada_kernel(c_ref, w_ref, b_ref, o_ref):
    s = _silu(c_ref[...])
    o_ref[...] = jnp.dot(s.astype(BF16), w_ref[0].astype(BF16),
                         preferred_element_type=F32) + b_ref[...]


def _ada(cs, w_ada, b_ada, layer):
    n = w_ada.shape[2]
    tn = 512
    return pl.pallas_call(
        _ada_kernel,
        grid=(n // tn,),
        in_specs=[pl.BlockSpec((8, D_MODEL), lambda j: (0, 0)),
                  pl.BlockSpec((1, D_MODEL, tn), lambda j: (layer, 0, j)),
                  pl.BlockSpec((1, tn), lambda j: (0, j))],
        out_specs=pl.BlockSpec((8, tn), lambda j: (0, j)),
        out_shape=jax.ShapeDtypeStruct((8, n), F32),
        compiler_params=_params("arbitrary"),
        name="ada",
    )(cs, w_ada, b_ada.reshape(1, n))


def _inproj_kernel(x_ref, mod_ref, w_ref, o_ref, h_ref, *, tc, tm):
    i = pl.program_id(1)

    @pl.when(pl.program_id(2) == 0)
    def _():
        x = x_ref[0]
        mu = jnp.mean(x, axis=-1, keepdims=True)
        xc = x - mu
        var = jnp.mean(xc * xc, axis=-1, keepdims=True)
        xn = xc * lax.rsqrt(var + LN_EPS)
        row = i * tm + lax.broadcasted_iota(jnp.int32, (tm, 1), 0)
        is_ctx = row < tc
        m = mod_ref[0]
        scale = jnp.where(is_ctx, m[0:1], m[2:3])
        shift = jnp.where(is_ctx, m[1:2], m[3:4])
        h_ref[...] = (xn * (1.0 + scale) + shift).astype(BF16)

    o_ref[0] = jnp.dot(h_ref[...], w_ref[...], preferred_element_type=F32).astype(o_ref.dtype)


def _inproj(xa, mod, w_pack, tc):
    b, tt, d = xa.shape
    tm = tt // 4
    tn = IN_TILE_N
    return pl.pallas_call(
        functools.partial(_inproj_kernel, tc=tc, tm=tm),
        grid=(b, tt // tm, N_PACK // tn),
        in_specs=[pl.BlockSpec((1, tm, d), lambda bi, i, j: (bi, i, 0)),
                  pl.BlockSpec((1, 8, d), lambda bi, i, j: (bi, 0, 0)),
                  pl.BlockSpec((d, tn), lambda bi, i, j: (0, j))],
        out_specs=pl.BlockSpec((1, tm, tn), lambda bi, i, j: (bi, i, j)),
        out_shape=jax.ShapeDtypeStruct((b, tt, N_PACK), BF16),
        scratch_shapes=[pltpu.VMEM((tm, d), BF16)],
        compiler_params=_params("arbitrary", "arbitrary", "arbitrary"),
        name="inproj",
    )(xa, mod, w_pack)


def _mla_kernel(cq_ref, ckv_ref, kr_ref, wq_ref, wk_ref, wv_ref, qn_ref, kvn_ref,
                ct_ref, st_ref, ctq_ref, stq_ref, o_ref, k_s, v_s, *, tc, tt, scale, skip_ctx):
    hps = ATTN_HEADS_PER_STEP

    @pl.when(pl.program_id(2) == 0)
    def _():
        ckv = ckv_ref[0].astype(F32)
        r = lax.rsqrt(jnp.mean(ckv * ckv, axis=-1, keepdims=True) + RMS_EPS)
        ckvn = (ckv * r * kvn_ref[...]).astype(BF16)
        kr = kr_ref[0].astype(F32)
        kroped = (kr * ct_ref[...] + pltpu.roll(kr * st_ref[...], 64, 1)).astype(BF16)
        for hh in range(hps):
            k_s[hh, :, 0:MLA_NOPE] = jnp.dot(ckvn, wk_ref[hh], preferred_element_type=F32).astype(BF16)
            k_s[hh, :, MLA_NOPE:] = kroped
            v_s[hh] = jnp.dot(ckvn, wv_ref[hh], preferred_element_type=F32).astype(BF16)

    cq = cq_ref[0].astype(F32)
    r = lax.rsqrt(jnp.mean(cq * cq, axis=-1, keepdims=True) + RMS_EPS)
    cqn = (cq * r * qn_ref[...]).astype(BF16)
    qfs = []
    for hh in range(hps):
        q = jnp.dot(cqn, wq_ref[hh], preferred_element_type=F32)
        qh = q[:, MLA_NOPE:]
        qr = qh * ctq_ref[...] + pltpu.roll(qh * stq_ref[...], 64, 1)
        qfs.append((jnp.concatenate([q[:, :MLA_NOPE], qr], axis=1) * (scale * LOG2E)).astype(BF16))

    def attend(nk):
        outs = []
        for hh in range(hps):
            s = lax.dot_general(qfs[hh], k_s[hh, 0:nk, :], (((1,), (1,)), ((), ())),
                                preferred_element_type=F32)
            p = jnp.exp2(s - jnp.max(s, axis=1, keepdims=True))
            l = jnp.sum(p, axis=1, keepdims=True)
            o = jnp.dot(p.astype(BF16), v_s[hh, 0:nk, :], preferred_element_type=F32)
            outs.append(o * (1.0 / l))
        o_ref[0] = jnp.concatenate(outs, axis=1).astype(o_ref.dtype)

    _attend_blocks(attend, tc, tt, skip_ctx)


def _attend_blocks(attend, tc, tt, skip_ctx):
    if skip_ctx:
        attend(tt)
        return

    @pl.when(pl.program_id(2) == 0)
    def _():
        attend(tc)

    @pl.when(pl.program_id(2) > 0)
    def _():
        attend(tt)


def _once(shape, index_map):
    return pl.BlockSpec(shape, index_map, pipeline_mode=pl.Buffered(1))


def _mla(proj, wq, wk, wv, q_norm, kv_norm, ct, st, tc, skip_ctx):
    b, tt, _ = proj.shape
    tq = Q_BLOCK
    assert tc == tq
    scale = (MLA_NOPE + MLA_ROPE) ** -0.5
    hps = ATTN_HEADS_PER_STEP
    sk = int(skip_ctx)
    return pl.pallas_call(
        functools.partial(_mla_kernel, tc=tc, tt=tt, scale=scale, skip_ctx=skip_ctx),
        grid=(b, MLA_HEADS // hps, tt // tq - sk),
        in_specs=[
            pl.BlockSpec((1, tq, MLA_Q_LORA), lambda bi, h, i: (bi, i + sk, OFF_CQ // MLA_Q_LORA)),
            _once((1, tt, MLA_KV_LORA), lambda bi, h, i: (bi, 0, OFF_CKV // MLA_KV_LORA)),
            _once((1, tt, 128), lambda bi, h, i: (bi, 0, OFF_KR // 128)),
            pl.BlockSpec((hps, MLA_Q_LORA, 256), lambda bi, h, i: (h, 0, 0)),
            pl.BlockSpec((hps, MLA_KV_LORA, MLA_NOPE), lambda bi, h, i: (h, 0, 0)),
            pl.BlockSpec((hps, MLA_KV_LORA, MLA_V), lambda bi, h, i: (h, 0, 0)),
            pl.BlockSpec((1, MLA_Q_LORA), lambda bi, h, i: (0, 0)),
            pl.BlockSpec((1, MLA_KV_LORA), lambda bi, h, i: (0, 0)),
            _once((tt, 128), lambda bi, h, i: (0, 0)),
            _once((tt, 128), lambda bi, h, i: (0, 0)),
            pl.BlockSpec((tq, 128), lambda bi, h, i: (i + sk, 0)),
            pl.BlockSpec((tq, 128), lambda bi, h, i: (i + sk, 0)),
        ],
        out_specs=pl.BlockSpec((1, tq, hps * MLA_V), lambda bi, h, i: (bi, i + sk, h)),
        out_shape=jax.ShapeDtypeStruct((b, tt, BRANCH_W), BF16),
        scratch_shapes=[pltpu.VMEM((hps, tt, 256), BF16), pltpu.VMEM((hps, tt, MLA_V), BF16)],
        compiler_params=_params("arbitrary", "arbitrary", "arbitrary"),
        name="mla_attn",
    )(proj, proj, proj, wq, wk, wv, q_norm.reshape(1, -1), kv_norm.reshape(1, -1), ct, st, ct, st)


def _diff_kernel(q_ref, k_ref, v_ref, lq_ref, lk_ref, g_ref, ct_ref, st_ref, ctq_ref, stq_ref,
                 o_ref, k_s, v_s, *, tc, tt, lam_init, skip_ctx):
    hps = ATTN_HEADS_PER_STEP
    lane = lax.broadcasted_iota(jnp.int32, (1, 128), 1)
    first_half = (lane & 32) == 0

    def rope(x, c, s):
        xs = jnp.where(first_half, pltpu.roll(x, 96, 1), pltpu.roll(x, 32, 1))
        return x * c + xs * s

    @pl.when(pl.program_id(2) == 0)
    def _():
        ones_col = jnp.where(lax.broadcasted_iota(jnp.int32, (tt, 128), 1) == 0, 1.0, 0.0).astype(BF16)
        for hh in range(hps):
            k_s[hh] = rope(k_ref[0, :, hh * 128:(hh + 1) * 128].astype(F32), ct_ref[...],
                           st_ref[...]).astype(BF16)
            v_s[hh, :, 0:128] = v_ref[0, :, hh * 128:(hh + 1) * 128]
            v_s[hh, :, 128:] = ones_col

    qs = []
    for hh in range(hps):
        q = rope(q_ref[0, :, hh * 128:(hh + 1) * 128].astype(F32), ctq_ref[...], stq_ref[...]) * (
            DIFF_HEAD_DIM ** -0.5 * LOG2E)
        qs.append((jnp.where(lane < 64, q, 0.0).astype(BF16), jnp.where(lane >= 64, q, 0.0).astype(BF16)))
    lqk = lq_ref[...] * lk_ref[...]
    lam = (jnp.exp(jnp.sum(lqk[0:1], axis=1, keepdims=True))
           - jnp.exp(jnp.sum(lqk[1:2], axis=1, keepdims=True)) + lam_init)

    def attend(nk):
        nt = (((1,), (1,)), ((), ()))
        outs = []
        for hh in range(hps):
            kk = k_s[hh, 0:nk, :]
            vv = v_s[hh, 0:nk, :]
            os, ls = [], []
            for qm in qs[hh]:
                s = lax.dot_general(qm, kk, nt, preferred_element_type=F32)
                e = jnp.exp2((s - jnp.max(s, axis=1, keepdims=True)).astype(BF16))
                oe = jnp.dot(e, vv, preferred_element_type=F32)
                os.append(oe[:, :128])
                ls.append(oe[:, 128:129])
            o = os[0] * (1.0 / ls[0]) - os[1] * (lam / ls[1])
            y = o * lax.rsqrt(jnp.mean(o * o, axis=-1, keepdims=True) + RMS_EPS)
            outs.append(y * g_ref[...] * (1.0 - lam_init))
        o_ref[0] = jnp.concatenate(outs, axis=1).astype(o_ref.dtype)

    _attend_blocks(attend, tc, tt, skip_ctx)


def _diff(proj, lam_q, lam_k, norm_g, ct, st, lam_init, tc, skip_ctx):
    b, tt, _ = proj.shape
    tq = Q_BLOCK
    assert tc == tq
    hps = ATTN_HEADS_PER_STEP
    hw = 128 * hps
    sk = int(skip_ctx)
    return pl.pallas_call(
        functools.partial(_diff_kernel, tc=tc, tt=tt, lam_init=lam_init, skip_ctx=skip_ctx),
        grid=(b, DIFF_HEADS // hps, tt // tq - sk),
        in_specs=[
            pl.BlockSpec((1, tq, hw), lambda bi, h, i: (bi, i + sk, OFF_QD // hw + h)),
            pl.BlockSpec((1, tt, hw), lambda bi, h, i: (bi, 0, OFF_KD // hw + h)),
            pl.BlockSpec((1, tt, hw), lambda bi, h, i: (bi, 0, OFF_VD // hw + h)),
            pl.BlockSpec((2, DIFF_HEAD_DIM), lambda bi, h, i: (0, 0)),
            pl.BlockSpec((2, DIFF_HEAD_DIM), lambda bi, h, i: (0, 0)),
            pl.BlockSpec((1, 128), lambda bi, h, i: (0, 0)),
            _once((tt, 128), lambda bi, h, i: (0, 0)),
            _once((tt, 128), lambda bi, h, i: (0, 0)),
            pl.BlockSpec((tq, 128), lambda bi, h, i: (i + sk, 0)),
            pl.BlockSpec((tq, 128), lambda bi, h, i: (i + sk, 0)),
        ],
        out_specs=pl.BlockSpec((1, tq, hw), lambda bi, h, i: (bi, i + sk, h)),
        out_shape=jax.ShapeDtypeStruct((b, tt, BRANCH_W), BF16),
        scratch_shapes=[pltpu.VMEM((hps, tt, 128), BF16), pltpu.VMEM((hps, tt, 256), BF16)],
        compiler_params=_params("arbitrary", "arbitrary", "arbitrary"),
        name="diff_attn",
    )(proj, proj, proj, lam_q, lam_k, norm_g.reshape(1, -1), ct, st, ct, st)


def _s5_kernel(x_ref, t_ref, uh_ref, hy_ref, p_ref, o_ref, tok_s, xf_s, ys_s, vf_s, vfs_s, vb_s, vbs_s,
               hf_s, hb_s, *, nb, nch, nctx):
    q = S5_CHUNK
    r = nb * nch
    g8 = pl.program_id(0) % 8
    lane_blk = lax.broadcasted_iota(jnp.int32, (1, 128), 1) // S5_GROUP
    u32 = jnp.uint32

    @pl.when(g8 == 0)
    def _():
        tok_s[...] = x_ref[...].astype(F32)
        for i in range(q):
            xf_s[i] = pltpu.bitcast(tok_s[pl.ds(i, r, stride=q), :].astype(BF16), u32)

    cols = []
    for k in range(q // 8):
        acc = jnp.zeros((r // 2, 128), u32)
        for i8 in range(8):
            rot = pltpu.roll(xf_s[k * 8 + i8], lax.rem((i8 - g8) * S5_GROUP + 128, 128), 1)
            acc = jnp.where(lane_blk == i8, rot, acc)
        cols.append(acc)
    x = pltpu.bitcast(jnp.concatenate(cols, axis=1), BF16)

    yloc = jnp.dot(x, t_ref[0], preferred_element_type=F32)
    v = jnp.dot(x, uh_ref[0], preferred_element_type=F32)
    vf_s[...] = v[:, 0:128]
    vfs_s[...] = v[:, 128:256]
    vb_s[...] = v[:, 256:384]
    vbs_s[...] = v[:, 384:512]
    pm = p_ref[0]
    zero = jnp.zeros((nb, 128), F32)

    def scan(order, v_ref, vs_ref, h_ref, p1, p2):
        h, hs = zero, zero
        for c in order:
            rows = pl.ds(c, nb, stride=nch)
            h_ref[rows, :] = h
            h, hs = h * p1 + hs * p2 + v_ref[rows, :], hs * p1 - h * p2 + vs_ref[rows, :]

    scan(range(nch), vf_s, vfs_s, hf_s, pm[0:1], pm[1:2])
    scan(list(range(nctx - 1, -1, -1)) + list(range(nch - 1, nctx - 1, -1)), vb_s, vbs_s, hb_s,
         pm[2:3], pm[3:4])
    hcat = jnp.concatenate([hf_s[...], hb_s[...]], axis=1).astype(BF16)
    y = yloc + jnp.dot(hcat, hy_ref[0], preferred_element_type=F32)
    ys_s[g8] = pltpu.bitcast(y.astype(BF16), u32)

    @pl.when(g8 == 7)
    def _():
        for k in range(q // 8):
            for i8 in range(8):
                out = jnp.zeros((r // 2, 128), u32)
                for gg in range(8):
                    yk = ys_s[gg, :, k * 128:(k + 1) * 128]
                    rot = yk if gg == i8 else pltpu.roll(yk, (gg - i8) * S5_GROUP % 128, 1)
                    out = jnp.where(lane_blk == gg, rot, out)
                tok_s[pl.ds(k * 8 + i8, r, stride=q), :] = pltpu.bitcast(out, BF16).astype(F32)
        o_ref[...] = tok_s[...].astype(o_ref.dtype)


def _s5_weights(lam_re, lam_im, log_dt, b_re, b_im, c_re, c_im, d):
    q = S5_CHUNK
    g, p, s = S5_GROUPS, S5_STATE, S5_GROUP
    hi = lax.Precision.HIGHEST
    lr, li = lam_re.astype(F32), lam_im.astype(F32)
    dt = jnp.exp(log_dt.astype(F32))[..., None]
    tau = jnp.arange(q + 1, dtype=F32)[None, :, None, None]
    mag = jnp.exp((lr * dt)[:, None] * tau)
    ang = (li * dt)[:, None] * tau
    ar, ai = mag * jnp.cos(ang), mag * jnp.sin(ang)
    den = lr * lr + li * li
    fr = ((ar[:, 1] - 1.0) * lr + ai[:, 1] * li) / den
    fi = (ai[:, 1] * lr - (ar[:, 1] - 1.0) * li) / den
    br, bi = b_re.astype(F32), b_im.astype(F32)
    bbr = fr[..., None] * br - fi[..., None] * bi
    bbi = fr[..., None] * bi + fi[..., None] * br
    cr, ci = c_re.astype(F32)[:, None], c_im.astype(F32)[:, None]
    car = cr * ar[:, :, :, None, :] - ci * ai[:, :, :, None, :]
    cai = cr * ai[:, :, :, None, :] + ci * ar[:, :, :, None, :]
    car_x = jnp.repeat(jnp.swapaxes(car[:, :q], 3, 4), s, axis=-1)
    cai_x = jnp.repeat(jnp.swapaxes(cai[:, :q], 3, 4), s, axis=-1)
    bbr_x = jnp.tile(bbr, (1, 1, 1, s))[:, None]
    bbi_x = jnp.tile(bbi, (1, 1, 1, s))[:, None]
    kern = jnp.sum(car_x * bbr_x - cai_x * bbi_x, axis=3).reshape(2, q, g, s, s)
    ii = np.arange(q)
    lag = ii[None, :, None] - ii[:, None, None]
    sel_f = jnp.asarray(lag == ii[None, None, :], F32)
    sel_b = jnp.asarray(-lag == ii[None, None, :], F32)
    tmat = (jnp.einsum('jit,tgos->gjsio', sel_f, kern[0], precision=hi)
            + jnp.einsum('jit,tgos->gjsio', sel_b, kern[1], precision=hi))
    eye = jnp.eye(q, dtype=F32)[:, None, :, None] * jnp.eye(s, dtype=F32)[None, :, None, :]
    tmat = tmat + eye[None] * d.astype(F32).reshape(g, 1, s, 1, 1)
    tmat = tmat.reshape(g, q * s, q * s)

    def chunk_in(are, aim, bre, bim):
        re = jnp.einsum('jgp,gpi->gjip', are, bre) - jnp.einsum('jgp,gpi->gjip', aim, bim)
        im = jnp.einsum('jgp,gpi->gjip', are, bim) + jnp.einsum('jgp,gpi->gjip', aim, bre)
        return jnp.concatenate([re, im, im, re], axis=-1)

    uh = jnp.concatenate([chunk_in(ar[0, :q][::-1], ai[0, :q][::-1], bbr[0], bbi[0]),
                          chunk_in(ar[1, :q], ai[1, :q], bbr[1], bbi[1])], axis=-1).reshape(g, q * s, 8 * p)

    def state_out(re, im):
        return jnp.concatenate([jnp.transpose(re, (1, 3, 0, 2)), -jnp.transpose(im, (1, 3, 0, 2))], axis=1)

    hy = jnp.concatenate([state_out(car[0, 1:], cai[0, 1:]),
                          state_out(car[1, 1:][::-1], cai[1, 1:][::-1])], axis=1).reshape(g, 4 * p, q * s)
    pm = jnp.stack([jnp.concatenate([ar[0, q], ar[0, q]], -1), jnp.concatenate([-ai[0, q], ai[0, q]], -1),
                    jnp.concatenate([ar[1, q], ar[1, q]], -1), jnp.concatenate([-ai[1, q], ai[1, q]], -1)],
                   axis=1)
    pm = jnp.concatenate([pm, jnp.zeros_like(pm)], axis=1)
    return tmat.astype(BF16), uh.astype(BF16), hy.astype(BF16), pm


def _s5(proj, weights, tc):
    b, tt, _ = proj.shape
    q, g, s = S5_CHUNK, S5_GROUPS, S5_GROUP
    nch = tt // q
    tmat, uh, hy, pm = weights
    r = nch * b
    rows = b * tt
    y = pl.pallas_call(
        functools.partial(_s5_kernel, nb=b, nch=nch, nctx=tc // q),
        grid=(g,),
        in_specs=[_once((rows, 128), lambda gi: (0, OFF_U // 128 + gi // 8)),
                  pl.BlockSpec((1, q * s, q * s), lambda gi: (gi, 0, 0)),
                  pl.BlockSpec((1, q * s, 512), lambda gi: (gi, 0, 0)),
                  pl.BlockSpec((1, 256, q * s), lambda gi: (gi, 0, 0)),
                  pl.BlockSpec((1, 8, 128), lambda gi: (gi, 0, 0))],
        out_specs=pl.BlockSpec((rows, 128), lambda gi: (0, gi // 8)),
        out_shape=jax.ShapeDtypeStruct((rows, BRANCH_W), BF16),
        scratch_shapes=[pltpu.VMEM((rows, 128), F32), pltpu.VMEM((q, r // 2, 128), jnp.uint32),
                        pltpu.VMEM((8, r // 2, q * s), jnp.uint32)] + [pltpu.VMEM((r, 128), F32)] * 6,
        compiler_params=_params("arbitrary"),
        name="s5",
    )(proj.reshape(rows, N_PACK), tmat, uh, hy, pm)
    return y.reshape(b, tt, BRANCH_W)


def _ssd_kernel(*refs, backward, first, nctx_chunks, nchunks):
    if first:
        (xbc_ref, prev_ref, next_ref, z_ref, dt_ref, cw_ref, cb_ref, dtb_ref, alog_ref, dsk_ref,
         sel_ref, o_ref, st_s) = refs
        yin_ref = ng_ref = None
    else:
        (xbc_ref, prev_ref, next_ref, z_ref, dt_ref, cw_ref, cb_ref, dtb_ref, alog_ref, dsk_ref,
         sel_ref, yin_ref, ng_ref, o_ref, st_s) = refs
    qn = SSD_CHUNK
    step = pl.program_id(1)
    if backward:
        c = jnp.where(step < nctx_chunks, nctx_chunks - 1 - step, nchunks - 1 - (step - nctx_chunks))
    else:
        c = step

    @pl.when(step == 0)
    def _():
        st_s[...] = jnp.zeros_like(st_s)

    x = xbc_ref[0].astype(F32)
    has_prev = jnp.logical_and(c != 0, c != nctx_chunks)
    has_next = jnp.logical_and(c != nctx_chunks - 1, c != nchunks - 1)
    prow = jnp.where(has_prev, prev_ref[0, 7:8, :].astype(F32), 0.0)
    nrow = jnp.where(has_next, next_ref[0, 0:1, :].astype(F32), 0.0)
    rid = lax.broadcasted_iota(jnp.int32, (qn, 1), 0)
    xm = jnp.where(rid == 0, prow, pltpu.roll(x, 1, 0))
    xp = jnp.where(rid == qn - 1, nrow, pltpu.roll(x, qn - 1, 0))
    cw = cw_ref[...]
    conv = xm * cw[0:1] + x * cw[1:2] + xp * cw[2:3] + cb_ref[...]
    act = _silu(conv)
    xs = act[:, :BRANCH_W]
    gn = SSD_GROUPS * SSD_STATE
    bm = act[:, BRANCH_W:BRANCH_W + gn].astype(BF16)
    cm = act[:, BRANCH_W + gn:].astype(BF16)

    dt = _softplus(dt_ref[0].astype(F32) + dtb_ref[...])
    a = -jnp.exp(alog_ref[...])
    da = dt * a
    ri = lax.broadcasted_iota(jnp.int32, (qn, qn), 0)
    ci = lax.broadcasted_iota(jnp.int32, (qn, qn), 1)
    causal = (ci >= ri) if backward else (ci <= ri)
    ones_tri = jnp.where(causal, 1.0, 0.0).astype(BF16)
    d1 = da.astype(BF16)
    r1 = da - d1.astype(F32)
    d2 = r1.astype(BF16)
    d3 = (r1 - d2.astype(F32)).astype(BF16)
    cum = (jnp.dot(ones_tri, d1, preferred_element_type=F32)
           + jnp.dot(ones_tri, d2, preferred_element_type=F32)
           + jnp.dot(ones_tri, d3, preferred_element_type=F32))
    edge = cum[0:1] if backward else cum[qn - 1:qn]
    cum_t = cum.T
    dt_t = dt.T
    w_edge = dt * jnp.exp(edge - cum)
    e_in = jnp.exp(cum)

    def expand(v):
        v1 = v.astype(BF16)
        v2 = (v - v1.astype(F32)).astype(BF16)
        return (jnp.dot(v1, sel_ref[...], preferred_element_type=F32)
                + jnp.dot(v2, sel_ref[...], preferred_element_type=F32))

    xw = (xs * expand(w_edge)).astype(BF16)
    e_in_x = expand(e_in)
    xs_b = xs.astype(BF16)
    rpg = SSD_HEADS // SSD_GROUPS
    gw = rpg * SSD_HEAD_DIM
    ys = []
    for g in range(SSD_GROUPS):
        bg = bm[:, g * SSD_STATE:(g + 1) * SSD_STATE]
        cg = cm[:, g * SSD_STATE:(g + 1) * SSD_STATE]
        cb = lax.dot_general(cg, bg, (((1,), (1,)), ((), ())), preferred_element_type=F32)
        st_g = st_s[g * gw:(g + 1) * gw, :]
        y_off = lax.dot_general(cg, st_g.astype(BF16), (((1,), (1,)), ((), ())),
                                preferred_element_type=F32)
        yg = y_off * e_in_x[:, g * gw:(g + 1) * gw]
        parts = []
        for r in range(rpg):
            h = g * rpg + r
            dec = jnp.exp(jnp.where(causal, cum[:, h:h + 1] - cum_t[h:h + 1, :], -jnp.inf))
            wmat = (cb * dec * dt_t[h:h + 1, :]).astype(BF16)
            parts.append(jnp.dot(wmat, xs_b[:, h * SSD_HEAD_DIM:(h + 1) * SSD_HEAD_DIM],
                                 preferred_element_type=F32))
        ys.append(yg + jnp.concatenate(parts, axis=1))
        new = lax.dot_general(xw[:, g * gw:(g + 1) * gw], bg, (((0,), (0,)), ((), ())),
                              preferred_element_type=F32)
        for r in range(rpg):
            h = g * rpg + r
            rows = slice(g * gw + r * SSD_HEAD_DIM, g * gw + (r + 1) * SSD_HEAD_DIM)
            cd = jnp.exp(edge[:, h:h + 1])
            st_s[rows, :] = st_s[rows, :] * cd + new[r * SSD_HEAD_DIM:(r + 1) * SSD_HEAD_DIM, :]
    y = jnp.concatenate(ys, axis=1)
    if first:
        o_ref[0] = y + xs * dsk_ref[...]
    else:
        y = y + yin_ref[0]
        zz = z_ref[0].astype(F32)
        y = y * _silu(zz)
        y = y * lax.rsqrt(jnp.mean(y * y, axis=-1, keepdims=True) + RMS_EPS) * ng_ref[...]
        o_ref[0] = y.astype(o_ref.dtype)


def _ssd_pass(proj, conv_w, conv_b, dt_bias, a_log, d_skip, sel, yin, norm_g, tc, backward):
    b, tt, _ = proj.shape
    qn = SSD_CHUNK
    nchunks = tt // qn
    nctx = tc // qn
    first = yin is None
    last_blk8 = tt // 8 - 1

    def chunk_of(s):
        if backward:
            return jnp.where(s < nctx, nctx - 1 - s, nchunks - 1 - (s - nctx))
        return s

    xbc_blk = OFF_XBC // SSD_CONV_CH
    in_specs = [
        pl.BlockSpec((1, qn, SSD_CONV_CH), lambda bi, s: (bi, chunk_of(s), xbc_blk)),
        pl.BlockSpec((1, 8, SSD_CONV_CH),
                     lambda bi, s: (bi, jnp.maximum(chunk_of(s) * (qn // 8) - 1, 0), xbc_blk)),
        pl.BlockSpec((1, 8, SSD_CONV_CH),
                     lambda bi, s: (bi, jnp.minimum((chunk_of(s) + 1) * (qn // 8), last_blk8), xbc_blk)),
        pl.BlockSpec((1, qn, BRANCH_W), lambda bi, s: (bi, chunk_of(s), OFF_Z // BRANCH_W)),
        pl.BlockSpec((1, qn, 128), lambda bi, s: (bi, chunk_of(s), OFF_DT // 128)),
        pl.BlockSpec((8, SSD_CONV_CH), lambda bi, s: (0, 0)),
        pl.BlockSpec((1, SSD_CONV_CH), lambda bi, s: (0, 0)),
        pl.BlockSpec((1, 128), lambda bi, s: (0, 0)),
        pl.BlockSpec((1, 128), lambda bi, s: (0, 0)),
        pl.BlockSpec((1, BRANCH_W), lambda bi, s: (0, 0)),
        pl.BlockSpec((128, BRANCH_W), lambda bi, s: (0, 0)),
    ]
    args = [proj, proj, proj, proj, proj, conv_w, conv_b, dt_bias, a_log, d_skip, sel]
    if not first:
        in_specs += [pl.BlockSpec((1, qn, BRANCH_W), lambda bi, s: (bi, chunk_of(s), 0)),
                     pl.BlockSpec((1, BRANCH_W), lambda bi, s: (0, 0))]
        args += [yin, norm_g]
    return pl.pallas_call(
        functools.partial(_ssd_kernel, backward=backward, first=first, nctx_chunks=nctx, nchunks=nchunks),
        grid=(b, nchunks),
        in_specs=in_specs,
        out_specs=pl.BlockSpec((1, qn, BRANCH_W), lambda bi, s: (bi, chunk_of(s), 0)),
        out_shape=jax.ShapeDtypeStruct((b, tt, BRANCH_W), F32 if first else BF16),
        scratch_shapes=[pltpu.VMEM((SSD_HEADS * SSD_HEAD_DIM, SSD_STATE), F32)],
        compiler_params=_params("arbitrary", "arbitrary"),
        name="ssd_bwd" if backward else "ssd_fwd",
    )(*args)


def _ssd(proj, conv_w, conv_b, dt_bias, a_log, d_skip, norm_g, tc):
    def lanes(v):
        return jnp.pad(v.astype(F32), (0, 128 - SSD_HEADS)).reshape(1, 128)

    cw = jnp.pad(conv_w.astype(F32), ((0, 5), (0, 0)))
    cb = conv_b.astype(F32).reshape(1, -1)
    dsk = jnp.repeat(d_skip.astype(F32), SSD_HEAD_DIM).reshape(1, -1)
    sel = (jnp.arange(128)[:, None] == (jnp.arange(BRANCH_W)[None, :] // SSD_HEAD_DIM)).astype(BF16)
    y1 = _ssd_pass(proj, cw, cb, lanes(dt_bias[0]), lanes(a_log[0]), dsk, sel, None, None, tc, False)
    return _ssd_pass(proj, cw, cb, lanes(dt_bias[1]), lanes(a_log[1]), jnp.zeros_like(dsk), sel,
                     y1, norm_g.astype(F32).reshape(1, -1), tc, True)


def _merge_kernel(ya_ref, ga_ref, yb_ref, gb_ref, yc_ref, yd_ref, gd_ref, gm_ref, bm_ref,
                  wb_ref, wg_ref, bg_ref, o_ref):
    def f(ref):
        return ref[0].astype(F32)

    g = _gelu_tanh(f(yb_ref))
    glu = g * _sigmoid(jnp.dot(g.astype(BF16), wg_ref[...], preferred_element_type=F32) + bg_ref[...])
    branches = (f(ya_ref) * _silu(f(ga_ref)), glu * _silu(f(gb_ref)), f(yc_ref), f(yd_ref) * _silu(f(gd_ref)))
    acc = None
    for n, br in enumerate(branches):
        gate = _sigmoid(gm_ref[0, :, n * D_MODEL:(n + 1) * D_MODEL].astype(F32) + bm_ref[n:n + 1, :])
        term = gate * jnp.dot(br.astype(BF16), wb_ref[n], preferred_element_type=F32)
        acc = term if acc is None else acc + term
    o_ref[0] = acc.astype(o_ref.dtype)


def _merge(proj, ya, yb, yc, yd, b_merge, w_branch, w_glu, b_glu, skip):
    b, tt, _ = proj.shape
    tm = Q_BLOCK
    bw = BRANCH_W

    def pspec(off):
        return pl.BlockSpec((1, tm, bw), lambda bi, i: (bi, i + skip, off // bw))

    yspec = pl.BlockSpec((1, tm, bw), lambda bi, i: (bi, i + skip, 0))
    once = dict(pipeline_mode=pl.Buffered(1))
    return pl.pallas_call(
        _merge_kernel,
        grid=(b, tt // tm - skip),
        in_specs=[yspec, pspec(OFF_GA), yspec, pspec(OFF_GB), yspec, yspec, pspec(OFF_GD),
                  pl.BlockSpec((1, tm, N_BRANCH * D_MODEL), lambda bi, i: (bi, i + skip, 0)),
                  pl.BlockSpec((N_BRANCH, D_MODEL), lambda bi, i: (0, 0)),
                  pl.BlockSpec((N_BRANCH, bw, D_MODEL), lambda bi, i: (0, 0, 0), **once),
                  pl.BlockSpec((bw, bw), lambda bi, i: (0, 0), **once),
                  pl.BlockSpec((1, bw), lambda bi, i: (0, 0))],
        out_specs=pl.BlockSpec((1, tm, D_MODEL), lambda bi, i: (bi, i + skip, 0)),
        out_shape=jax.ShapeDtypeStruct((b, tt, D_MODEL), BF16),
        compiler_params=_params("arbitrary", "arbitrary"),
        name="merge",
    )(ya, proj, yb, proj, yc, yd, proj, proj, b_merge, w_branch, w_glu, b_glu.reshape(1, -1))


def _out_kernel(m_ref, x_ref, mod_ref, w_ref, lg_ref, lb_ref, o_ref, *, tc, tm, row0):
    i = pl.program_id(1)
    out = jnp.dot(m_ref[0], w_ref[...], preferred_element_type=F32)
    row = row0 + i * tm + lax.broadcasted_iota(jnp.int32, (tm, 1), 0)
    m = mod_ref[0]
    gate = jnp.where(row < tc, m[4:5], m[5:6])
    v = DEEPNORM_ALPHA * x_ref[0] + gate * out
    mu = jnp.mean(v, axis=-1, keepdims=True)
    vc = v - mu
    var = jnp.mean(vc * vc, axis=-1, keepdims=True)
    o_ref[0] = vc * lax.rsqrt(var + LN_EPS) * lg_ref[...] + lb_ref[...]


def _out(merged, xa, mod, w_out, ln_g, ln_b, tc, latent_only):
    b, tt, d = xa.shape
    tm = Q_BLOCK
    skip = tc // tm if latent_only else 0
    nblk = tt // tm - skip
    spec = pl.BlockSpec((1, tm, d), lambda bi, i: (bi, i + skip, 0))
    return pl.pallas_call(
        functools.partial(_out_kernel, tc=tc, tm=tm, row0=skip * tm),
        grid=(b, nblk),
        in_specs=[spec, spec,
                  pl.BlockSpec((1, 8, d), lambda bi, i: (bi, 0, 0)),
                  pl.BlockSpec((d, d), lambda bi, i: (0, 0)),
                  pl.BlockSpec((1, d), lambda bi, i: (0, 0)),
                  pl.BlockSpec((1, d), lambda bi, i: (0, 0))],
        out_specs=pl.BlockSpec((1, tm, d), lambda bi, i: (bi, i, 0)),
        out_shape=jax.ShapeDtypeStruct((b, nblk * tm, d), F32),
        compiler_params=_params("arbitrary", "arbitrary"),
        name="out_proj",
    )(merged, xa, mod, w_out, ln_g.reshape(1, -1), ln_b.reshape(1, -1))


def _half_split(w, heads):
    k = w.shape[0]
    return w.reshape(k, heads, ROPE_DIM // 2, 2).transpose(0, 1, 3, 2).reshape(k, heads * ROPE_DIM)


def _pack_w_in(w):
    sp = np.cumsum([0, MLA_Q_LORA, MLA_KV_LORA, MLA_ROPE, BRANCH_W, BRANCH_W, BRANCH_W, BRANCH_W,
                    SSD_CONV_CH, SSD_HEADS, BRANCH_W, BRANCH_W, BRANCH_W, BRANCH_W, N_BRANCH * D_MODEL])
    seg = [w[:, sp[k]:sp[k + 1]] for k in range(14)]
    cq, ckv, kr, ga, u, gb, z, xbc, dt, qd, kd, vd, gd, gm = seg
    kr_hs = _half_split(kr, 1)
    kr_sw = jnp.concatenate([kr_hs[:, 32:], kr_hs[:, :32]], axis=1)
    pad = jnp.zeros((w.shape[0], 128 - SSD_HEADS), w.dtype)
    packed = jnp.concatenate([gm, xbc, ga, u, gb, z, _half_split(qd, 2 * DIFF_HEADS),
                              _half_split(kd, 2 * DIFF_HEADS), vd, gd, cq, ckv, kr_hs, kr_sw, dt, pad],
                             axis=1)
    return packed.astype(BF16)


def _pack_mla(w_uq, w_ukv):
    k = w_uq.shape[0]
    wq = w_uq.reshape(k, MLA_HEADS, MLA_NOPE + MLA_ROPE)
    rope = wq[:, :, MLA_NOPE:].reshape(k, MLA_HEADS, ROPE_DIM // 2, 2)
    ev, od = rope[..., 0], rope[..., 1]
    wq = jnp.concatenate([wq[:, :, :MLA_NOPE], ev, od, od, ev], axis=-1)
    wkv = w_ukv.reshape(w_ukv.shape[0], MLA_HEADS, MLA_NOPE + MLA_V)
    return (wq.transpose(1, 0, 2).astype(BF16),
            wkv[:, :, :MLA_NOPE].transpose(1, 0, 2).astype(BF16),
            wkv[:, :, MLA_NOPE:].transpose(1, 0, 2).astype(BF16))


def _rope_tables(tl, tc):
    rows = tl // GRID_W
    row_id = jnp.repeat(jnp.arange(rows, dtype=F32), GRID_W)
    col_id = jnp.tile(jnp.arange(GRID_W, dtype=F32), rows)
    quarter = ROPE_DIM // 4
    inv_freq = ROPE_BASE ** (-jnp.arange(quarter, dtype=F32) / quarter)
    ang = jnp.concatenate([row_id[:, None] * inv_freq, col_id[:, None] * inv_freq], axis=-1)
    cos = jnp.concatenate([jnp.ones((tc, ROPE_DIM // 2), F32), jnp.cos(ang)], axis=0)
    sin = jnp.concatenate([jnp.zeros((tc, ROPE_DIM // 2), F32), jnp.sin(ang)], axis=0)
    zero = jnp.zeros_like(cos)
    mla = (jnp.concatenate([cos, cos, zero, zero], axis=1), jnp.concatenate([zero, zero, -sin, sin], axis=1))
    diff = (jnp.concatenate([cos, cos, cos, cos], axis=1), jnp.concatenate([-sin, sin, -sin, sin], axis=1))
    return mla, diff


def _layer(xa, cs, p, tables, layer_idx, tc, last):
    b = xa.shape[0]
    mla_tab, diff_tab = tables
    mod = _ada(cs, p['w_ada_stacked'], p['b_ada'], layer_idx)
    d = D_MODEL
    shift, scale, gate = mod[:, :d], mod[:, d:2 * d], mod[:, 2 * d:]
    zeros = jnp.zeros((b, d), F32)
    modb = jnp.stack([jnp.broadcast_to(scale[0], (b, d)), jnp.broadcast_to(shift[0], (b, d)),
                      scale[1:1 + b], shift[1:1 + b],
                      jnp.broadcast_to(gate[0], (b, d)), gate[1:1 + b], zeros, zeros], axis=1)
    proj = _inproj(xa, modb, _pack_w_in(p['w_in']), tc)
    wq, wk, wv = _pack_mla(p['mla_w_uq'], p['mla_w_ukv'])
    ya = _mla(proj, wq, wk, wv, p['mla_q_norm'], p['mla_kv_norm'], mla_tab[0], mla_tab[1], tc, last)
    lam_init = 0.8 - 0.6 * math.exp(-0.3 * layer_idx)
    yd = _diff(proj, p['diff_lambda_q'], p['diff_lambda_k'], p['diff_norm'], diff_tab[0], diff_tab[1],
               lam_init, tc, last)
    yb = _s5(proj, _s5_weights(p['s5_lambda_re'], p['s5_lambda_im'], p['s5_log_dt'], p['s5_b_re'],
                               p['s5_b_im'], p['s5_c_re'], p['s5_c_im'], p['s5_d']), tc)
    yc = _ssd(proj, p['ssd_conv_w'], p['ssd_conv_b'], p['ssd_dt_bias'], p['ssd_a_log'], p['ssd_d'],
              p['ssd_norm'], tc)
    merged = _merge(proj, ya, yb, yc, yd, p['b_merge'], p['w_branch'].astype(BF16),
                    p['s5_w_glu'].astype(BF16), p['s5_b_glu'], tc // Q_BLOCK if last else 0)
    return _out(merged, xa, modb, p['w_out'].astype(BF16), p['ln_g'], p['ln_b'], tc, last)


def kernel(x, c, ctx, c_ctx, w_ada, b_ada, w_in, mla_q_norm, mla_w_uq, mla_kv_norm, mla_w_ukv,
           s5_lambda_re, s5_lambda_im, s5_log_dt, s5_b_re, s5_b_im, s5_c_re, s5_c_im, s5_d,
           s5_w_glu, s5_b_glu, ssd_conv_w, ssd_conv_b, ssd_dt_bias, ssd_a_log, ssd_d, ssd_norm,
           diff_lambda_q, diff_lambda_k, diff_norm, b_merge, w_branch, w_out, ln_g, ln_b):
    b, tl, d = x.shape
    tc = ctx.shape[1]
    stacked = dict(
        w_ada=w_ada, b_ada=b_ada, w_in=w_in, mla_q_norm=mla_q_norm, mla_w_uq=mla_w_uq,
        mla_kv_norm=mla_kv_norm, mla_w_ukv=mla_w_ukv, s5_lambda_re=s5_lambda_re, s5_lambda_im=s5_lambda_im,
        s5_log_dt=s5_log_dt, s5_b_re=s5_b_re, s5_b_im=s5_b_im, s5_c_re=s5_c_re, s5_c_im=s5_c_im, s5_d=s5_d,
        s5_w_glu=s5_w_glu, s5_b_glu=s5_b_glu, ssd_conv_w=ssd_conv_w, ssd_conv_b=ssd_conv_b,
        ssd_dt_bias=ssd_dt_bias, ssd_a_log=ssd_a_log, ssd_d=ssd_d, ssd_norm=ssd_norm,
        diff_lambda_q=diff_lambda_q, diff_lambda_k=diff_lambda_k, diff_norm=diff_norm, b_merge=b_merge,
        w_branch=w_branch, w_out=w_out, ln_g=ln_g, ln_b=ln_b)
    tables = _rope_tables(tl, tc)
    cs = jnp.concatenate([c_ctx[None], c, jnp.zeros((8 - 1 - b, d), c.dtype)], axis=0)
    xa = jnp.concatenate([ctx, x], axis=1)
    depth = w_in.shape[0]
    for i in range(depth):
        p = {k: v[i] for k, v in stacked.items() if k != 'w_ada'}
        p['w_ada_stacked'] = w_ada
        xa = _layer(xa, cs, p, tables, i, tc, i == depth - 1)
    return xa
```

```python
import functools
import math

import numpy as np
import jax
import jax.numpy as jnp
from jax import lax
from jax.experimental import pallas as pl
from jax.experimental.pallas import tpu as pltpu

F32 = jnp.float32
BF16 = jnp.bfloat16

D_MODEL = 2048
DEPTH = 2
GRID_W = 64
N_BRANCH = 4
BRANCH_W = 1024
ROPE_DIM = 64
ROPE_BASE = 10000.0

MLA_HEADS = 8
MLA_NOPE = 128
MLA_ROPE = ROPE_DIM
MLA_V = 128
MLA_Q_LORA = 512
MLA_KV_LORA = 256

S5_GROUP = 16
S5_GROUPS = BRANCH_W // S5_GROUP
S5_STATE = 64
S5_CHUNK = 16

SSD_HEAD_DIM = 64
SSD_HEADS = BRANCH_W // SSD_HEAD_DIM
SSD_GROUPS = 4
SSD_STATE = 128
SSD_CHUNK = 128
SSD_CONV_CH = BRANCH_W + 2 * SSD_GROUPS * SSD_STATE

DIFF_HEAD_DIM = ROPE_DIM
DIFF_HEADS = BRANCH_W // (2 * DIFF_HEAD_DIM)

LN_EPS = 1e-5
RMS_EPS = 1e-6
DEEPNORM_ALPHA = (2 * DEPTH) ** 0.25

V7X_VMEM_LIMIT_BYTES = 56 * 1024 * 1024
Q_BLOCK = 256
ATTN_HEADS_PER_STEP = 2
MLA_HEADS_PER_STEP = 4
LOG2E = math.log2(math.e)

OFF_GM = 0
OFF_XBC = 8192
OFF_GA = 10240
OFF_U = 11264
OFF_GB = 12288
OFF_Z = 13312
OFF_QD = 14336
OFF_KD = 15360
OFF_VD = 16384
OFF_GD = 17408
OFF_CQ = 18432
OFF_CKV = 18944
OFF_KR = 19200
OFF_DT = 19328
N_PACK = 19456
IN_TILE_N = 1024


def _sigmoid(x):
    return 0.5 * (1.0 + jnp.tanh(0.5 * x))


def _silu(x):
    return x * _sigmoid(x)


def _gelu_tanh(x):
    return 0.5 * x * (1.0 + jnp.tanh(math.sqrt(2.0 / math.pi) * (x + 0.044715 * (x * x * x))))


def _softplus(x):
    return jnp.maximum(x, 0.0) + jnp.log(1.0 + jnp.exp(-jnp.abs(x)))


def _params(*sem):
    return pltpu.CompilerParams(dimension_semantics=sem, vmem_limit_bytes=V7X_VMEM_LIMIT_BYTES)


def _ada_kernel(c_ref, w_ref, b_ref, o_ref):
    s = _silu(c_ref[...])
    o_ref[...] = jnp.dot(s.astype(BF16), w_ref[0].astype(BF16),
                         preferred_element_type=F32) + b_ref[...]


def _ada(cs, w_ada, b_ada, layer):
    n = w_ada.shape[2]
    tn = 512
    return pl.pallas_call(
        _ada_kernel,
        grid=(n // tn,),
        in_specs=[pl.BlockSpec((8, D_MODEL), lambda j: (0, 0)),
                  pl.BlockSpec((1, D_MODEL, tn), lambda j: (layer, 0, j)),
                  pl.BlockSpec((1, tn), lambda j: (0, j))],
        out_specs=pl.BlockSpec((8, tn), lambda j: (0, j)),
        out_shape=jax.ShapeDtypeStruct((8, n), F32),
        compiler_params=_params("arbitrary"),
        name="ada",
    )(cs, w_ada, b_ada.reshape(1, n))


def _inproj_kernel(x_ref, mod_ref, w_ref, o_ref, h_ref, *, tc, tm):
    i = pl.program_id(1)

    @pl.when(pl.program_id(2) == 0)
    def _():
        x = x_ref[0]
        mu = jnp.mean(x, axis=-1, keepdims=True)
        xc = x - mu
        var = jnp.mean(xc * xc, axis=-1, keepdims=True)
        xn = xc * lax.rsqrt(var + LN_EPS)
        row = i * tm + lax.broadcasted_iota(jnp.int32, (tm, 1), 0)
        is_ctx = row < tc
        m = mod_ref[0]
        scale = jnp.where(is_ctx, m[0:1], m[2:3])
        shift = jnp.where(is_ctx, m[1:2], m[3:4])
        h_ref[...] = (xn * (1.0 + scale) + shift).astype(BF16)

    o_ref[0] = jnp.dot(h_ref[...], w_ref[...], preferred_element_type=F32).astype(o_ref.dtype)


def _inproj(xa, mod, w_pack, tc):
    b, tt, d = xa.shape
    tm = tt // 4
    tn = IN_TILE_N
    return pl.pallas_call(
        functools.partial(_inproj_kernel, tc=tc, tm=tm),
        grid=(b, tt // tm, N_PACK // tn),
        in_specs=[pl.BlockSpec((1, tm, d), lambda bi, i, j: (bi, i, 0)),
                  pl.BlockSpec((1, 8, d), lambda bi, i, j: (bi, 0, 0)),
                  pl.BlockSpec((d, tn), lambda bi, i, j: (0, j))],
        out_specs=pl.BlockSpec((1, tm, tn), lambda bi, i, j: (bi, i, j)),
        out_shape=jax.ShapeDtypeStruct((b, tt, N_PACK), BF16),
        scratch_shapes=[pltpu.VMEM((tm, d), BF16)],
        compiler_params=_params("arbitrary", "arbitrary", "arbitrary"),
        name="inproj",
    )(xa, mod, w_pack)


def _mla_kernel(cq_ref, ckv_ref, kr_ref, wq_ref, wk_ref, wv_ref, qn_ref, kvn_ref,
                ct_ref, st_ref, ctq_ref, stq_ref, o_ref, k_s, v_s, *, tc, tt, scale, skip_ctx):
    hps = MLA_HEADS_PER_STEP

    @pl.when(pl.program_id(2) == 0)
    def _():
        ckv = ckv_ref[0].astype(F32)
        r = lax.rsqrt(jnp.mean(ckv * ckv, axis=-1, keepdims=True) + RMS_EPS)
        ckvn = (ckv * r * kvn_ref[...]).astype(BF16)
        kr = kr_ref[0].astype(F32)
        kroped = (kr * ct_ref[...] + pltpu.roll(kr * st_ref[...], 64, 1)).astype(BF16)
        for hh in range(hps):
            k_s[hh, :, 0:MLA_NOPE] = jnp.dot(ckvn, wk_ref[hh], preferred_element_type=F32).astype(BF16)
            k_s[hh, :, MLA_NOPE:] = kroped
            v_s[hh] = jnp.dot(ckvn, wv_ref[hh], preferred_element_type=F32).astype(BF16)

    cq = cq_ref[0].astype(F32)
    r = lax.rsqrt(jnp.mean(cq * cq, axis=-1, keepdims=True) + RMS_EPS)
    cqn = (cq * r * qn_ref[...]).astype(BF16)
    qfs = []
    for hh in range(hps):
        q = jnp.dot(cqn, wq_ref[hh], preferred_element_type=F32)
        qh = q[:, MLA_NOPE:]
        qr = qh * ctq_ref[...] + pltpu.roll(qh * stq_ref[...], 64, 1)
        qfs.append((jnp.concatenate([q[:, :MLA_NOPE], qr], axis=1) * (scale * LOG2E)).astype(BF16))

    def attend(nk):
        outs = []
        for hh in range(hps):
            s = lax.dot_general(qfs[hh], k_s[hh, 0:nk, :], (((1,), (1,)), ((), ())),
                                preferred_element_type=F32)
            p = jnp.exp2(s - jnp.max(s, axis=1, keepdims=True))
            l = jnp.sum(p, axis=1, keepdims=True)
            o = jnp.dot(p.astype(BF16), v_s[hh, 0:nk, :], preferred_element_type=F32)
            outs.append(o * (1.0 / l))
        o_ref[0] = jnp.concatenate(outs, axis=1).astype(o_ref.dtype)

    _attend_blocks(attend, tc, tt, skip_ctx)


def _attend_blocks(attend, tc, tt, skip_ctx):
    if skip_ctx:
        attend(tt)
        return

    @pl.when(pl.program_id(2) == 0)
    def _():
        attend(tc)

    @pl.when(pl.program_id(2) > 0)
    def _():
        attend(tt)


def _once(shape, index_map):
    return pl.BlockSpec(shape, index_map, pipeline_mode=pl.Buffered(1))


def _mla(proj, wq, wk, wv, q_norm, kv_norm, ct, st, tc, skip_ctx):
    b, tt, _ = proj.shape
    tq = Q_BLOCK
    assert tc == tq
    scale = (MLA_NOPE + MLA_ROPE) ** -0.5
    hps = MLA_HEADS_PER_STEP
    sk = int(skip_ctx)
    return pl.pallas_call(
        functools.partial(_mla_kernel, tc=tc, tt=tt, scale=scale, skip_ctx=skip_ctx),
        grid=(b, MLA_HEADS // hps, tt // tq - sk),
        in_specs=[
            pl.BlockSpec((1, tq, MLA_Q_LORA), lambda bi, h, i: (bi, i + sk, OFF_CQ // MLA_Q_LORA)),
            _once((1, tt, MLA_KV_LORA), lambda bi, h, i: (bi, 0, OFF_CKV // MLA_KV_LORA)),
            _once((1, tt, 128), lambda bi, h, i: (bi, 0, OFF_KR // 128)),
            pl.BlockSpec((hps, MLA_Q_LORA, 256), lambda bi, h, i: (h, 0, 0)),
            pl.BlockSpec((hps, MLA_KV_LORA, MLA_NOPE), lambda bi, h, i: (h, 0, 0)),
            pl.BlockSpec((hps, MLA_KV_LORA, MLA_V), lambda bi, h, i: (h, 0, 0)),
            pl.BlockSpec((1, MLA_Q_LORA), lambda bi, h, i: (0, 0)),
            pl.BlockSpec((1, MLA_KV_LORA), lambda bi, h, i: (0, 0)),
            _once((tt, 128), lambda bi, h, i: (0, 0)),
            _once((tt, 128), lambda bi, h, i: (0, 0)),
            pl.BlockSpec((tq, 128), lambda bi, h, i: (i + sk, 0)),
            pl.BlockSpec((tq, 128), lambda bi, h, i: (i + sk, 0)),
        ],
        out_specs=pl.BlockSpec((1, tq, hps * MLA_V), lambda bi, h, i: (bi, i + sk, h)),
        out_shape=jax.ShapeDtypeStruct((b, tt, BRANCH_W), BF16),
        scratch_shapes=[pltpu.VMEM((hps, tt, 256), BF16), pltpu.VMEM((hps, tt, MLA_V), BF16)],
        compiler_params=_params("arbitrary", "arbitrary", "arbitrary"),
        name="mla_attn",
    )(proj, proj, proj, wq, wk, wv, q_norm.reshape(1, -1), kv_norm.reshape(1, -1), ct, st, ct, st)


def _diff_kernel(q_ref, k_ref, v_ref, lq_ref, lk_ref, g_ref, ct_ref, st_ref, ctq_ref, stq_ref,
                 o_ref, k_s, *, tc, tt, lam_init, skip_ctx):
    hps = ATTN_HEADS_PER_STEP
    lane = lax.broadcasted_iota(jnp.int32, (1, 128), 1)
    first_half = (lane & 32) == 0

    def rope(x, c, s):
        xs = jnp.where(first_half, pltpu.roll(x, 96, 1), pltpu.roll(x, 32, 1))
        return x * c + xs * s

    @pl.when(pl.program_id(2) == 0)
    def _():
        for hh in range(hps):
            k_s[hh] = rope(k_ref[0, :, hh * 128:(hh + 1) * 128].astype(F32), ct_ref[...],
                           st_ref[...]).astype(BF16)

    qs = []
    for hh in range(hps):
        q = rope(q_ref[0, :, hh * 128:(hh + 1) * 128].astype(F32), ctq_ref[...], stq_ref[...]) * (
            DIFF_HEAD_DIM ** -0.5 * LOG2E)
        qs.append((jnp.where(lane < 64, q, 0.0).astype(BF16), jnp.where(lane >= 64, q, 0.0).astype(BF16)))
    lqk = lq_ref[...] * lk_ref[...]
    lam = (jnp.exp(jnp.sum(lqk[0:1], axis=1, keepdims=True))
           - jnp.exp(jnp.sum(lqk[1:2], axis=1, keepdims=True)) + lam_init)

    def attend(nk):
        nt = (((1,), (1,)), ((), ()))
        outs = []
        for hh in range(hps):
            kk = k_s[hh, 0:nk, :]
            s1 = lax.dot_general(qs[hh][0], kk, nt, preferred_element_type=F32)
            s2 = lax.dot_general(qs[hh][1], kk, nt, preferred_element_type=F32)
            e1 = jnp.exp2(s1 - jnp.max(s1, axis=1, keepdims=True))
            e2 = jnp.exp2(s2 - jnp.max(s2, axis=1, keepdims=True))
            l1 = jnp.sum(e1, axis=1, keepdims=True)
            l2 = jnp.sum(e2, axis=1, keepdims=True)
            w = e1 - e2 * (lam * l1 / l2)
            o = jnp.dot(w.astype(BF16), v_ref[0, 0:nk, hh * 128:(hh + 1) * 128],
                        preferred_element_type=F32) * (1.0 / l1)
            y = o * lax.rsqrt(jnp.mean(o * o, axis=-1, keepdims=True) + RMS_EPS)
            outs.append(y * g_ref[...] * (1.0 - lam_init))
        o_ref[0] = jnp.concatenate(outs, axis=1).astype(o_ref.dtype)

    _attend_blocks(attend, tc, tt, skip_ctx)


def _diff(proj, lam_q, lam_k, norm_g, ct, st, lam_init, tc, skip_ctx):
    b, tt, _ = proj.shape
    tq = Q_BLOCK
    assert tc == tq
    hps = ATTN_HEADS_PER_STEP
    hw = 128 * hps
    sk = int(skip_ctx)
    return pl.pallas_call(
        functools.partial(_diff_kernel, tc=tc, tt=tt, lam_init=lam_init, skip_ctx=skip_ctx),
        grid=(b, DIFF_HEADS // hps, tt // tq - sk),
        in_specs=[
            pl.BlockSpec((1, tq, hw), lambda bi, h, i: (bi, i + sk, OFF_QD // hw + h)),
            pl.BlockSpec((1, tt, hw), lambda bi, h, i: (bi, 0, OFF_KD // hw + h)),
            pl.BlockSpec((1, tt, hw), lambda bi, h, i: (bi, 0, OFF_VD // hw + h)),
            pl.BlockSpec((2, DIFF_HEAD_DIM), lambda bi, h, i: (0, 0)),
            pl.BlockSpec((2, DIFF_HEAD_DIM), lambda bi, h, i: (0, 0)),
            pl.BlockSpec((1, 128), lambda bi, h, i: (0, 0)),
            _once((tt, 128), lambda bi, h, i: (0, 0)),
            _once((tt, 128), lambda bi, h, i: (0, 0)),
            pl.BlockSpec((tq, 128), lambda bi, h, i: (i + sk, 0)),
            pl.BlockSpec((tq, 128), lambda bi, h, i: (i + sk, 0)),
        ],
        out_specs=pl.BlockSpec((1, tq, hw), lambda bi, h, i: (bi, i + sk, h)),
        out_shape=jax.ShapeDtypeStruct((b, tt, BRANCH_W), BF16),
        scratch_shapes=[pltpu.VMEM((hps, tt, 128), BF16)],
        compiler_params=_params("arbitrary", "arbitrary", "arbitrary"),
        name="diff_attn",
    )(proj, proj, proj, lam_q, lam_k, norm_g.reshape(1, -1), ct, st, ct, st)


def _s5_kernel(x_ref, t_ref, uh_ref, hy_ref, p_ref, o_ref, tok_s, xf_s, ys_s, vf_s, vfs_s, vb_s, vbs_s,
               hf_s, hb_s, *, nb, nch, nctx):
    q = S5_CHUNK
    r = nb * nch
    g8 = pl.program_id(0) % 8
    lane_blk = lax.broadcasted_iota(jnp.int32, (1, 128), 1) // S5_GROUP
    u32 = jnp.uint32

    @pl.when(g8 == 0)
    def _():
        tok_s[...] = x_ref[...].astype(F32)
        for i in range(q):
            xf_s[i] = pltpu.bitcast(tok_s[pl.ds(i, r, stride=q), :].astype(BF16), u32)

    cols = []
    for k in range(q // 8):
        acc = jnp.zeros((r // 2, 128), u32)
        for i8 in range(8):
            rot = pltpu.roll(xf_s[k * 8 + i8], lax.rem((i8 - g8) * S5_GROUP + 128, 128), 1)
            acc = jnp.where(lane_blk == i8, rot, acc)
        cols.append(acc)
    x = pltpu.bitcast(jnp.concatenate(cols, axis=1), BF16)

    yloc = jnp.dot(x, t_ref[0], preferred_element_type=F32)
    v = jnp.dot(x, uh_ref[0], preferred_element_type=F32)
    vf_s[...] = v[:, 0:128]
    vfs_s[...] = v[:, 128:256]
    vb_s[...] = v[:, 256:384]
    vbs_s[...] = v[:, 384:512]
    pm = p_ref[0]
    zero = jnp.zeros((nb, 128), F32)

    def scan(order, v_ref, vs_ref, h_ref, p1, p2):
        h, hs = zero, zero
        for c in order:
            rows = pl.ds(c, nb, stride=nch)
            h_ref[rows, :] = h
            h, hs = h * p1 + hs * p2 + v_ref[rows, :], hs * p1 - h * p2 + vs_ref[rows, :]

    scan(range(nch), vf_s, vfs_s, hf_s, pm[0:1], pm[1:2])
    scan(list(range(nctx - 1, -1, -1)) + list(range(nch - 1, nctx - 1, -1)), vb_s, vbs_s, hb_s,
         pm[2:3], pm[3:4])
    hcat = jnp.concatenate([hf_s[...], hb_s[...]], axis=1).astype(BF16)
    y = yloc + jnp.dot(hcat, hy_ref[0], preferred_element_type=F32)
    ys_s[g8] = pltpu.bitcast(y.astype(BF16), u32)

    @pl.when(g8 == 7)
    def _():
        for k in range(q // 8):
            for i8 in range(8):
                out = jnp.zeros((r // 2, 128), u32)
                for gg in range(8):
                    yk = ys_s[gg, :, k * 128:(k + 1) * 128]
                    rot = yk if gg == i8 else pltpu.roll(yk, (gg - i8) * S5_GROUP % 128, 1)
                    out = jnp.where(lane_blk == gg, rot, out)
                tok_s[pl.ds(k * 8 + i8, r, stride=q), :] = pltpu.bitcast(out, BF16).astype(F32)
        o_ref[...] = tok_s[...].astype(o_ref.dtype)


def _s5_weights(lam_re, lam_im, log_dt, b_re, b_im, c_re, c_im, d):
    q = S5_CHUNK
    g, p, s = S5_GROUPS, S5_STATE, S5_GROUP
    hi = lax.Precision.HIGHEST
    lr, li = lam_re.astype(F32), lam_im.astype(F32)
    dt = jnp.exp(log_dt.astype(F32))[..., None]
    tau = jnp.arange(q + 1, dtype=F32)[None, :, None, None]
    mag = jnp.exp((lr * dt)[:, None] * tau)
    ang = (li * dt)[:, None] * tau
    ar, ai = mag * jnp.cos(ang), mag * jnp.sin(ang)
    den = lr * lr + li * li
    fr = ((ar[:, 1] - 1.0) * lr + ai[:, 1] * li) / den
    fi = (ai[:, 1] * lr - (ar[:, 1] - 1.0) * li) / den
    br, bi = b_re.astype(F32), b_im.astype(F32)
    bbr = fr[..., None] * br - fi[..., None] * bi
    bbi = fr[..., None] * bi + fi[..., None] * br
    cr, ci = c_re.astype(F32)[:, None], c_im.astype(F32)[:, None]
    car = cr * ar[:, :, :, None, :] - ci * ai[:, :, :, None, :]
    cai = cr * ai[:, :, :, None, :] + ci * ar[:, :, :, None, :]
    car_x = jnp.repeat(jnp.swapaxes(car[:, :q], 3, 4), s, axis=-1)
    cai_x = jnp.repeat(jnp.swapaxes(cai[:, :q], 3, 4), s, axis=-1)
    bbr_x = jnp.tile(bbr, (1, 1, 1, s))[:, None]
    bbi_x = jnp.tile(bbi, (1, 1, 1, s))[:, None]
    kern = jnp.sum(car_x * bbr_x - cai_x * bbi_x, axis=3).reshape(2, q, g, s, s)
    ii = np.arange(q)
    lag = ii[None, :, None] - ii[:, None, None]
    sel_f = jnp.asarray(lag == ii[None, None, :], F32)
    sel_b = jnp.asarray(-lag == ii[None, None, :], F32)
    tmat = (jnp.einsum('jit,tgos->gjsio', sel_f, kern[0], precision=hi)
            + jnp.einsum('jit,tgos->gjsio', sel_b, kern[1], precision=hi))
    eye = jnp.eye(q, dtype=F32)[:, None, :, None] * jnp.eye(s, dtype=F32)[None, :, None, :]
    tmat = tmat + eye[None] * d.astype(F32).reshape(g, 1, s, 1, 1)
    tmat = tmat.reshape(g, q * s, q * s)

    def chunk_in(are, aim, bre, bim):
        re = jnp.einsum('jgp,gpi->gjip', are, bre) - jnp.einsum('jgp,gpi->gjip', aim, bim)
        im = jnp.einsum('jgp,gpi->gjip', are, bim) + jnp.einsum('jgp,gpi->gjip', aim, bre)
        return jnp.concatenate([re, im, im, re], axis=-1)

    uh = jnp.concatenate([chunk_in(ar[0, :q][::-1], ai[0, :q][::-1], bbr[0], bbi[0]),
                          chunk_in(ar[1, :q], ai[1, :q], bbr[1], bbi[1])], axis=-1).reshape(g, q * s, 8 * p)

    def state_out(re, im):
        return jnp.concatenate([jnp.transpose(re, (1, 3, 0, 2)), -jnp.transpose(im, (1, 3, 0, 2))], axis=1)

    hy = jnp.concatenate([state_out(car[0, 1:], cai[0, 1:]),
                          state_out(car[1, 1:][::-1], cai[1, 1:][::-1])], axis=1).reshape(g, 4 * p, q * s)
    pm = jnp.stack([jnp.concatenate([ar[0, q], ar[0, q]], -1), jnp.concatenate([-ai[0, q], ai[0, q]], -1),
                    jnp.concatenate([ar[1, q], ar[1, q]], -1), jnp.concatenate([-ai[1, q], ai[1, q]], -1)],
                   axis=1)
    pm = jnp.concatenate([pm, jnp.zeros_like(pm)], axis=1)
    return tmat.astype(BF16), uh.astype(BF16), hy.astype(BF16), pm


def _s5(proj, weights, tc):
    b, tt, _ = proj.shape
    q, g, s = S5_CHUNK, S5_GROUPS, S5_GROUP
    nch = tt // q
    tmat, uh, hy, pm = weights
    r = nch * b
    rows = b * tt
    y = pl.pallas_call(
        functools.partial(_s5_kernel, nb=b, nch=nch, nctx=tc // q),
        grid=(g,),
        in_specs=[_once((rows, 128), lambda gi: (0, OFF_U // 128 + gi // 8)),
                  pl.BlockSpec((1, q * s, q * s), lambda gi: (gi, 0, 0)),
                  pl.BlockSpec((1, q * s, 512), lambda gi: (gi, 0, 0)),
                  pl.BlockSpec((1, 256, q * s), lambda gi: (gi, 0, 0)),
                  pl.BlockSpec((1, 8, 128), lambda gi: (gi, 0, 0))],
        out_specs=pl.BlockSpec((rows, 128), lambda gi: (0, gi // 8)),
        out_shape=jax.ShapeDtypeStruct((rows, BRANCH_W), BF16),
        scratch_shapes=[pltpu.VMEM((rows, 128), F32), pltpu.VMEM((q, r // 2, 128), jnp.uint32),
                        pltpu.VMEM((8, r // 2, q * s), jnp.uint32)] + [pltpu.VMEM((r, 128), F32)] * 6,
        compiler_params=_params("arbitrary"),
        name="s5",
    )(proj.reshape(rows, N_PACK), tmat, uh, hy, pm)
    return y.reshape(b, tt, BRANCH_W)


def _ssd_kernel(*refs, backward, first, nctx_chunks, nchunks):
    if first:
        (xbc_ref, prev_ref, next_ref, z_ref, dt_ref, cw_ref, cb_ref, dtb_ref, alog_ref, dsk_ref,
         sel_ref, o_ref, st_s) = refs
        yin_ref = ng_ref = None
    else:
        (xbc_ref, prev_ref, next_ref, z_ref, dt_ref, cw_ref, cb_ref, dtb_ref, alog_ref, dsk_ref,
         sel_ref, yin_ref, ng_ref, o_ref, st_s) = refs
    qn = SSD_CHUNK
    step = pl.program_id(1)
    if backward:
        c = jnp.where(step < nctx_chunks, nctx_chunks - 1 - step, nchunks - 1 - (step - nctx_chunks))
    else:
        c = step

    @pl.when(step == 0)
    def _():
        st_s[...] = jnp.zeros_like(st_s)

    x = xbc_ref[0].astype(F32)
    has_prev = jnp.logical_and(c != 0, c != nctx_chunks)
    has_next = jnp.logical_and(c != nctx_chunks - 1, c != nchunks - 1)
    prow = jnp.where(has_prev, prev_ref[0, 7:8, :].astype(F32), 0.0)
    nrow = jnp.where(has_next, next_ref[0, 0:1, :].astype(F32), 0.0)
    rid = lax.broadcasted_iota(jnp.int32, (qn, 1), 0)
    xm = jnp.where(rid == 0, prow, pltpu.roll(x, 1, 0))
    xp = jnp.where(rid == qn - 1, nrow, pltpu.roll(x, qn - 1, 0))
    cw = cw_ref[...]
    conv = xm * cw[0:1] + x * cw[1:2] + xp * cw[2:3] + cb_ref[...]
    act = _silu(conv)
    xs = act[:, :BRANCH_W]
    gn = SSD_GROUPS * SSD_STATE
    bm = act[:, BRANCH_W:BRANCH_W + gn].astype(BF16)
    cm = act[:, BRANCH_W + gn:].astype(BF16)

    dt = _softplus(dt_ref[0].astype(F32) + dtb_ref[...])
    a = -jnp.exp(alog_ref[...])
    da = dt * a
    ri = lax.broadcasted_iota(jnp.int32, (qn, qn), 0)
    ci = lax.broadcasted_iota(jnp.int32, (qn, qn), 1)
    causal = (ci >= ri) if backward else (ci <= ri)
    ones_tri = jnp.where(causal, 1.0, 0.0).astype(BF16)
    d1 = da.astype(BF16)
    r1 = da - d1.astype(F32)
    d2 = r1.astype(BF16)
    d3 = (r1 - d2.astype(F32)).astype(BF16)
    cum = (jnp.dot(ones_tri, d1, preferred_element_type=F32)
           + jnp.dot(ones_tri, d2, preferred_element_type=F32)
           + jnp.dot(ones_tri, d3, preferred_element_type=F32))
    edge = cum[0:1] if backward else cum[qn - 1:qn]
    cum_t = cum.T
    dt_t = dt.T
    w_edge = dt * jnp.exp(edge - cum)
    e_in = jnp.exp(cum)

    def expand(v):
        v1 = v.astype(BF16)
        v2 = (v - v1.astype(F32)).astype(BF16)
        return (jnp.dot(v1, sel_ref[...], preferred_element_type=F32)
                + jnp.dot(v2, sel_ref[...], preferred_element_type=F32))

    xw = (xs * expand(w_edge)).astype(BF16)
    e_in_x = expand(e_in)
    xs_b = xs.astype(BF16)
    rpg = SSD_HEADS // SSD_GROUPS
    gw = rpg * SSD_HEAD_DIM
    ys = []
    for g in range(SSD_GROUPS):
        bg = bm[:, g * SSD_STATE:(g + 1) * SSD_STATE]
        cg = cm[:, g * SSD_STATE:(g + 1) * SSD_STATE]
        cb = lax.dot_general(cg, bg, (((1,), (1,)), ((), ())), preferred_element_type=F32)
        st_g = st_s[g * gw:(g + 1) * gw, :]
        y_off = lax.dot_general(cg, st_g.astype(BF16), (((1,), (1,)), ((), ())),
                                preferred_element_type=F32)
        yg = y_off * e_in_x[:, g * gw:(g + 1) * gw]
        parts = []
        for r in range(rpg):
            h = g * rpg + r
            dec = jnp.exp(jnp.where(causal, cum[:, h:h + 1] - cum_t[h:h + 1, :], -jnp.inf))
            wmat = (cb * dec * dt_t[h:h + 1, :]).astype(BF16)
            parts.append(jnp.dot(wmat, xs_b[:, h * SSD_HEAD_DIM:(h + 1) * SSD_HEAD_DIM],
                                 preferred_element_type=F32))
        ys.append(yg + jnp.concatenate(parts, axis=1))
        new = lax.dot_general(xw[:, g * gw:(g + 1) * gw], bg, (((0,), (0,)), ((), ())),
                              preferred_element_type=F32)
        for r in range(rpg):
            h = g * rpg + r
            rows = slice(g * gw + r * SSD_HEAD_DIM, g * gw + (r + 1) * SSD_HEAD_DIM)
            cd = jnp.exp(edge[:, h:h + 1])
            st_s[rows, :] = st_s[rows, :] * cd + new[r * SSD_HEAD_DIM:(r + 1) * SSD_HEAD_DIM, :]
    y = jnp.concatenate(ys, axis=1)
    if first:
        o_ref[0] = y + xs * dsk_ref[...]
    else:
        y = y + yin_ref[0]
        zz = z_ref[0].astype(F32)
        y = y * _silu(zz)
        y = y * lax.rsqrt(jnp.mean(y * y, axis=-1, keepdims=True) + RMS_EPS) * ng_ref[...]
        o_ref[0] = y.astype(o_ref.dtype)


def _ssd_pass(proj, conv_w, conv_b, dt_bias, a_log, d_skip, sel, yin, norm_g, tc, backward):
    b, tt, _ = proj.shape
    qn = SSD_CHUNK
    nchunks = tt // qn
    nctx = tc // qn
    first = yin is None
    last_blk8 = tt // 8 - 1

    def chunk_of(s):
        if backward:
            return jnp.where(s < nctx, nctx - 1 - s, nchunks - 1 - (s - nctx))
        return s

    xbc_blk = OFF_XBC // SSD_CONV_CH
    in_specs = [
        pl.BlockSpec((1, qn, SSD_CONV_CH), lambda bi, s: (bi, chunk_of(s), xbc_blk)),
        pl.BlockSpec((1, 8, SSD_CONV_CH),
                     lambda bi, s: (bi, jnp.maximum(chunk_of(s) * (qn // 8) - 1, 0), xbc_blk)),
        pl.BlockSpec((1, 8, SSD_CONV_CH),
                     lambda bi, s: (bi, jnp.minimum((chunk_of(s) + 1) * (qn // 8), last_blk8), xbc_blk)),
        pl.BlockSpec((1, qn, BRANCH_W), lambda bi, s: (bi, chunk_of(s), OFF_Z // BRANCH_W)),
        pl.BlockSpec((1, qn, 128), lambda bi, s: (bi, chunk_of(s), OFF_DT // 128)),
        pl.BlockSpec((8, SSD_CONV_CH), lambda bi, s: (0, 0)),
        pl.BlockSpec((1, SSD_CONV_CH), lambda bi, s: (0, 0)),
        pl.BlockSpec((1, 128), lambda bi, s: (0, 0)),
        pl.BlockSpec((1, 128), lambda bi, s: (0, 0)),
        pl.BlockSpec((1, BRANCH_W), lambda bi, s: (0, 0)),
        pl.BlockSpec((128, BRANCH_W), lambda bi, s: (0, 0)),
    ]
    args = [proj, proj, proj, proj, proj, conv_w, conv_b, dt_bias, a_log, d_skip, sel]
    if not first:
        in_specs += [pl.BlockSpec((1, qn, BRANCH_W), lambda bi, s: (bi, chunk_of(s), 0)),
                     pl.BlockSpec((1, BRANCH_W), lambda bi, s: (0, 0))]
        args += [yin, norm_g]
    return pl.pallas_call(
        functools.partial(_ssd_kernel, backward=backward, first=first, nctx_chunks=nctx, nchunks=nchunks),
        grid=(b, nchunks),
        in_specs=in_specs,
        out_specs=pl.BlockSpec((1, qn, BRANCH_W), lambda bi, s: (bi, chunk_of(s), 0)),
        out_shape=jax.ShapeDtypeStruct((b, tt, BRANCH_W), F32 if first else BF16),
        scratch_shapes=[pltpu.VMEM((SSD_HEADS * SSD_HEAD_DIM, SSD_STATE), F32)],
        compiler_params=_params("arbitrary", "arbitrary"),
        name="ssd_bwd" if backward else "ssd_fwd",
    )(*args)


def _ssd(proj, conv_w, conv_b, dt_bias, a_log, d_skip, norm_g, tc):
    def lanes(v):
        return jnp.pad(v.astype(F32), (0, 128 - SSD_HEADS)).reshape(1, 128)

    cw = jnp.pad(conv_w.astype(F32), ((0, 5), (0, 0)))
    cb = conv_b.astype(F32).reshape(1, -1)
    dsk = jnp.repeat(d_skip.astype(F32), SSD_HEAD_DIM).reshape(1, -1)
    sel = (jnp.arange(128)[:, None] == (jnp.arange(BRANCH_W)[None, :] // SSD_HEAD_DIM)).astype(BF16)
    y1 = _ssd_pass(proj, cw, cb, lanes(dt_bias[0]), lanes(a_log[0]), dsk, sel, None, None, tc, False)
    return _ssd_pass(proj, cw, cb, lanes(dt_bias[1]), lanes(a_log[1]), jnp.zeros_like(dsk), sel,
                     y1, norm_g.astype(F32).reshape(1, -1), tc, True)


def _merge_kernel(ya_ref, ga_ref, yb_ref, gb_ref, yc_ref, yd_ref, gd_ref, gm_ref, bm_ref,
                  wb_ref, wg_ref, bg_ref, o_ref):
    def f(ref):
        return ref[0].astype(F32)

    g = _gelu_tanh(f(yb_ref))
    glu = g * _sigmoid(jnp.dot(g.astype(BF16), wg_ref[...], preferred_element_type=F32) + bg_ref[...])
    branches = (f(ya_ref) * _silu(f(ga_ref)), glu * _silu(f(gb_ref)), f(yc_ref), f(yd_ref) * _silu(f(gd_ref)))
    acc = None
    for n, br in enumerate(branches):
        gate = _sigmoid(gm_ref[0, :, n * D_MODEL:(n + 1) * D_MODEL].astype(F32) + bm_ref[n:n + 1, :])
        term = gate * jnp.dot(br.astype(BF16), wb_ref[n], preferred_element_type=F32)
        acc = term if acc is None else acc + term
    o_ref[0] = acc.astype(o_ref.dtype)


def _merge(proj, ya, yb, yc, yd, b_merge, w_branch, w_glu, b_glu, skip):
    b, tt, _ = proj.shape
    tm = Q_BLOCK
    bw = BRANCH_W

    def pspec(off):
        return pl.BlockSpec((1, tm, bw), lambda bi, i: (bi, i + skip, off // bw))

    yspec = pl.BlockSpec((1, tm, bw), lambda bi, i: (bi, i + skip, 0))
    once = dict(pipeline_mode=pl.Buffered(1))
    return pl.pallas_call(
        _merge_kernel,
        grid=(b, tt // tm - skip),
        in_specs=[yspec, pspec(OFF_GA), yspec, pspec(OFF_GB), yspec, yspec, pspec(OFF_GD),
                  pl.BlockSpec((1, tm, N_BRANCH * D_MODEL), lambda bi, i: (bi, i + skip, 0)),
                  pl.BlockSpec((N_BRANCH, D_MODEL), lambda bi, i: (0, 0)),
                  pl.BlockSpec((N_BRANCH, bw, D_MODEL), lambda bi, i: (0, 0, 0), **once),
                  pl.BlockSpec((bw, bw), lambda bi, i: (0, 0), **once),
                  pl.BlockSpec((1, bw), lambda bi, i: (0, 0))],
        out_specs=pl.BlockSpec((1, tm, D_MODEL), lambda bi, i: (bi, i + skip, 0)),
        out_shape=jax.ShapeDtypeStruct((b, tt, D_MODEL), BF16),
        compiler_params=_params("arbitrary", "arbitrary"),
        name="merge",
    )(ya, proj, yb, proj, yc, yd, proj, proj, b_merge, w_branch, w_glu, b_glu.reshape(1, -1))


def _out_kernel(m_ref, x_ref, mod_ref, w_ref, lg_ref, lb_ref, o_ref, *, tc, tm, row0):
    i = pl.program_id(1)
    out = jnp.dot(m_ref[0], w_ref[...], preferred_element_type=F32)
    row = row0 + i * tm + lax.broadcasted_iota(jnp.int32, (tm, 1), 0)
    m = mod_ref[0]
    gate = jnp.where(row < tc, m[4:5], m[5:6])
    v = DEEPNORM_ALPHA * x_ref[0] + gate * out
    mu = jnp.mean(v, axis=-1, keepdims=True)
    vc = v - mu
    var = jnp.mean(vc * vc, axis=-1, keepdims=True)
    o_ref[0] = vc * lax.rsqrt(var + LN_EPS) * lg_ref[...] + lb_ref[...]


def _out(merged, xa, mod, w_out, ln_g, ln_b, tc, latent_only):
    b, tt, d = xa.shape
    tm = Q_BLOCK
    skip = tc // tm if latent_only else 0
    nblk = tt // tm - skip
    spec = pl.BlockSpec((1, tm, d), lambda bi, i: (bi, i + skip, 0))
    return pl.pallas_call(
        functools.partial(_out_kernel, tc=tc, tm=tm, row0=skip * tm),
        grid=(b, nblk),
        in_specs=[spec, spec,
                  pl.BlockSpec((1, 8, d), lambda bi, i: (bi, 0, 0)),
                  pl.BlockSpec((d, d), lambda bi, i: (0, 0)),
                  pl.BlockSpec((1, d), lambda bi, i: (0, 0)),
                  pl.BlockSpec((1, d), lambda bi, i: (0, 0))],
        out_specs=pl.BlockSpec((1, tm, d), lambda bi, i: (bi, i, 0)),
        out_shape=jax.ShapeDtypeStruct((b, nblk * tm, d), F32),
        compiler_params=_params("arbitrary", "arbitrary"),
        name="out_proj",
    )(merged, xa, mod, w_out, ln_g.reshape(1, -1), ln_b.reshape(1, -1))


def _half_split(w, heads):
    k = w.shape[0]
    return w.reshape(k, heads, ROPE_DIM // 2, 2).transpose(0, 1, 3, 2).reshape(k, heads * ROPE_DIM)


def _pack_w_in(w):
    sp = np.cumsum([0, MLA_Q_LORA, MLA_KV_LORA, MLA_ROPE, BRANCH_W, BRANCH_W, BRANCH_W, BRANCH_W,
                    SSD_CONV_CH, SSD_HEADS, BRANCH_W, BRANCH_W, BRANCH_W, BRANCH_W, N_BRANCH * D_MODEL])
    seg = [w[:, sp[k]:sp[k + 1]] for k in range(14)]
    cq, ckv, kr, ga, u, gb, z, xbc, dt, qd, kd, vd, gd, gm = seg
    kr_hs = _half_split(kr, 1)
    kr_sw = jnp.concatenate([kr_hs[:, 32:], kr_hs[:, :32]], axis=1)
    pad = jnp.zeros((w.shape[0], 128 - SSD_HEADS), w.dtype)
    packed = jnp.concatenate([gm, xbc, ga, u, gb, z, _half_split(qd, 2 * DIFF_HEADS),
                              _half_split(kd, 2 * DIFF_HEADS), vd, gd, cq, ckv, kr_hs, kr_sw, dt, pad],
                             axis=1)
    return packed.astype(BF16)


def _pack_mla(w_uq, w_ukv):
    k = w_uq.shape[0]
    wq = w_uq.reshape(k, MLA_HEADS, MLA_NOPE + MLA_ROPE)
    rope = wq[:, :, MLA_NOPE:].reshape(k, MLA_HEADS, ROPE_DIM // 2, 2)
    ev, od = rope[..., 0], rope[..., 1]
    wq = jnp.concatenate([wq[:, :, :MLA_NOPE], ev, od, od, ev], axis=-1)
    wkv = w_ukv.reshape(w_ukv.shape[0], MLA_HEADS, MLA_NOPE + MLA_V)
    return (wq.transpose(1, 0, 2).astype(BF16),
            wkv[:, :, :MLA_NOPE].transpose(1, 0, 2).astype(BF16),
            wkv[:, :, MLA_NOPE:].transpose(1, 0, 2).astype(BF16))


def _rope_tables(tl, tc):
    rows = tl // GRID_W
    row_id = jnp.repeat(jnp.arange(rows, dtype=F32), GRID_W)
    col_id = jnp.tile(jnp.arange(GRID_W, dtype=F32), rows)
    quarter = ROPE_DIM // 4
    inv_freq = ROPE_BASE ** (-jnp.arange(quarter, dtype=F32) / quarter)
    ang = jnp.concatenate([row_id[:, None] * inv_freq, col_id[:, None] * inv_freq], axis=-1)
    cos = jnp.concatenate([jnp.ones((tc, ROPE_DIM // 2), F32), jnp.cos(ang)], axis=0)
    sin = jnp.concatenate([jnp.zeros((tc, ROPE_DIM // 2), F32), jnp.sin(ang)], axis=0)
    zero = jnp.zeros_like(cos)
    mla = (jnp.concatenate([cos, cos, zero, zero], axis=1), jnp.concatenate([zero, zero, -sin, sin], axis=1))
    diff = (jnp.concatenate([cos, cos, cos, cos], axis=1), jnp.concatenate([-sin, sin, -sin, sin], axis=1))
    return mla, diff


def _layer(xa, cs, p, tables, layer_idx, tc, last):
    b = xa.shape[0]
    mla_tab, diff_tab = tables
    mod = _ada(cs, p['w_ada_stacked'], p['b_ada'], layer_idx)
    d = D_MODEL
    shift, scale, gate = mod[:, :d], mod[:, d:2 * d], mod[:, 2 * d:]
    zeros = jnp.zeros((b, d), F32)
    modb = jnp.stack([jnp.broadcast_to(scale[0], (b, d)), jnp.broadcast_to(shift[0], (b, d)),
                      scale[1:1 + b], shift[1:1 + b],
                      jnp.broadcast_to(gate[0], (b, d)), gate[1:1 + b], zeros, zeros], axis=1)
    proj = _inproj(xa, modb, _pack_w_in(p['w_in']), tc)
    wq, wk, wv = _pack_mla(p['mla_w_uq'], p['mla_w_ukv'])
    ya = _mla(proj, wq, wk, wv, p['mla_q_norm'], p['mla_kv_norm'], mla_tab[0], mla_tab[1], tc, last)
    lam_init = 0.8 - 0.6 * math.exp(-0.3 * layer_idx)
    yd = _diff(proj, p['diff_lambda_q'], p['diff_lambda_k'], p['diff_norm'], diff_tab[0], diff_tab[1],
               lam_init, tc, last)
    yb = _s5(proj, _s5_weights(p['s5_lambda_re'], p['s5_lambda_im'], p['s5_log_dt'], p['s5_b_re'],
                               p['s5_b_im'], p['s5_c_re'], p['s5_c_im'], p['s5_d']), tc)
    yc = _ssd(proj, p['ssd_conv_w'], p['ssd_conv_b'], p['ssd_dt_bias'], p['ssd_a_log'], p['ssd_d'],
              p['ssd_norm'], tc)
    merged = _merge(proj, ya, yb, yc, yd, p['b_merge'], p['w_branch'].astype(BF16),
                    p['s5_w_glu'].astype(BF16), p['s5_b_glu'], tc // Q_BLOCK if last else 0)
    return _out(merged, xa, modb, p['w_out'].astype(BF16), p['ln_g'], p['ln_b'], tc, last)


def kernel(x, c, ctx, c_ctx, w_ada, b_ada, w_in, mla_q_norm, mla_w_uq, mla_kv_norm, mla_w_ukv,
           s5_lambda_re, s5_lambda_im, s5_log_dt, s5_b_re, s5_b_im, s5_c_re, s5_c_im, s5_d,
           s5_w_glu, s5_b_glu, ssd_conv_w, ssd_conv_b, ssd_dt_bias, ssd_a_log, ssd_d, ssd_norm,
           diff_lambda_q, diff_lambda_k, diff_norm, b_merge, w_branch, w_out, ln_g, ln_b):
    b, tl, d = x.shape
    tc = ctx.shape[1]
    stacked = dict(
        w_ada=w_ada, b_ada=b_ada, w_in=w_in, mla_q_norm=mla_q_norm, mla_w_uq=mla_w_uq,
        mla_kv_norm=mla_kv_norm, mla_w_ukv=mla_w_ukv, s5_lambda_re=s5_lambda_re, s5_lambda_im=s5_lambda_im,
        s5_log_dt=s5_log_dt, s5_b_re=s5_b_re, s5_b_im=s5_b_im, s5_c_re=s5_c_re, s5_c_im=s5_c_im, s5_d=s5_d,
        s5_w_glu=s5_w_glu, s5_b_glu=s5_b_glu, ssd_conv_w=ssd_conv_w, ssd_conv_b=ssd_conv_b,
        ssd_dt_bias=ssd_dt_bias, ssd_a_log=ssd_a_log, ssd_d=ssd_d, ssd_norm=ssd_norm,
        diff_lambda_q=diff_lambda_q, diff_lambda_k=diff_lambda_k, diff_norm=diff_norm, b_merge=b_merge,
        w_branch=w_branch, w_out=w_out, ln_g=ln_g, ln_b=ln_b)
    tables = _rope_tables(tl, tc)
    cs = jnp.concatenate([c_ctx[None], c, jnp.zeros((8 - 1 - b, d), c.dtype)], axis=0)
    xa = jnp.concatenate([ctx, x], axis=1)
    depth = w_in.shape[0]
    for i in range(depth):
        p = {k: v[i] for k, v in stacked.items() if k != 'w_ada'}
        p['w_ada_stacked'] = w_ada
        xa = _layer(xa, cs, p, tables, i, tc, i == depth - 1)
    return xa
```

```python
import functools
import math

import numpy as np
import jax
import jax.numpy as jnp
from jax import lax
from jax.experimental import pallas as pl
from jax.experimental.pallas import tpu as pltpu

F32 = jnp.float32
BF16 = jnp.bfloat16

D_MODEL = 2048
DEPTH = 2
GRID_W = 64
N_BRANCH = 4
BRANCH_W = 1024
ROPE_DIM = 64
ROPE_BASE = 10000.0

MLA_HEADS = 8
MLA_NOPE = 128
MLA_ROPE = ROPE_DIM
MLA_V = 128
MLA_Q_LORA = 512
MLA_KV_LORA = 256

S5_GROUP = 16
S5_GROUPS = BRANCH_W // S5_GROUP
S5_STATE = 64
S5_CHUNK = 16

SSD_HEAD_DIM = 64
SSD_HEADS = BRANCH_W // SSD_HEAD_DIM
SSD_GROUPS = 4
SSD_STATE = 128
SSD_CHUNK = 128
SSD_CONV_CH = BRANCH_W + 2 * SSD_GROUPS * SSD_STATE

DIFF_HEAD_DIM = ROPE_DIM
DIFF_HEADS = BRANCH_W // (2 * DIFF_HEAD_DIM)

LN_EPS = 1e-5
RMS_EPS = 1e-6
DEEPNORM_ALPHA = (2 * DEPTH) ** 0.25

V7X_VMEM_LIMIT_BYTES = 56 * 1024 * 1024
Q_BLOCK = 256
ATTN_HEADS_PER_STEP = 2
MLA_HEADS_PER_STEP = 8
LOG2E = math.log2(math.e)

OFF_GM = 0
OFF_XBC = 8192
OFF_GA = 10240
OFF_U = 11264
OFF_GB = 12288
OFF_Z = 13312
OFF_QD = 14336
OFF_KD = 15360
OFF_VD = 16384
OFF_GD = 17408
OFF_CQ = 18432
OFF_CKV = 18944
OFF_KR = 19200
OFF_DT = 19328
N_PACK = 19456
IN_TILE_N = 1024


def _sigmoid(x):
    return 0.5 * (1.0 + jnp.tanh(0.5 * x))


def _silu(x):
    return x * _sigmoid(x)


def _gelu_tanh(x):
    return 0.5 * x * (1.0 + jnp.tanh(math.sqrt(2.0 / math.pi) * (x + 0.044715 * (x * x * x))))


def _softplus(x):
    return jnp.maximum(x, 0.0) + jnp.log(1.0 + jnp.exp(-jnp.abs(x)))


def _params(*sem):
    return pltpu.CompilerParams(dimension_semantics=sem, vmem_limit_bytes=V7X_VMEM_LIMIT_BYTES)


def _ada_kernel(c_ref, w_ref, b_ref, o_ref):
    s = _silu(c_ref[...])
    o_ref[...] = jnp.dot(s.astype(BF16), w_ref[0].astype(BF16),
                         preferred_element_type=F32) + b_ref[...]


def _ada(cs, w_ada, b_ada, layer):
    n = w_ada.shape[2]
    tn = 512
    return pl.pallas_call(
        _ada_kernel,
        grid=(n // tn,),
        in_specs=[pl.BlockSpec((8, D_MODEL), lambda j: (0, 0)),
                  pl.BlockSpec((1, D_MODEL, tn), lambda j: (layer, 0, j)),
                  pl.BlockSpec((1, tn), lambda j: (0, j))],
        out_specs=pl.BlockSpec((8, tn), lambda j: (0, j)),
        out_shape=jax.ShapeDtypeStruct((8, n), F32),
        compiler_params=_params("arbitrary"),
        name="ada",
    )(cs, w_ada, b_ada.reshape(1, n))


def _inproj_kernel(x_ref, mod_ref, w_ref, o_ref, h_ref, *, tc, tm):
    i = pl.program_id(1)

    @pl.when(pl.program_id(2) == 0)
    def _():
        x = x_ref[0]
        mu = jnp.mean(x, axis=-1, keepdims=True)
        xc = x - mu
        var = jnp.mean(xc * xc, axis=-1, keepdims=True)
        xn = xc * lax.rsqrt(var + LN_EPS)
        row = i * tm + lax.broadcasted_iota(jnp.int32, (tm, 1), 0)
        is_ctx = row < tc
        m = mod_ref[0]
        scale = jnp.where(is_ctx, m[0:1], m[2:3])
        shift = jnp.where(is_ctx, m[1:2], m[3:4])
        h_ref[...] = (xn * (1.0 + scale) + shift).astype(BF16)

    o_ref[0] = jnp.dot(h_ref[...], w_ref[...], preferred_element_type=F32).astype(o_ref.dtype)


def _inproj(xa, mod, w_pack, tc):
    b, tt, d = xa.shape
    tm = tt // 4
    tn = IN_TILE_N
    return pl.pallas_call(
        functools.partial(_inproj_kernel, tc=tc, tm=tm),
        grid=(b, tt // tm, N_PACK // tn),
        in_specs=[pl.BlockSpec((1, tm, d), lambda bi, i, j: (bi, i, 0)),
                  pl.BlockSpec((1, 8, d), lambda bi, i, j: (bi, 0, 0)),
                  pl.BlockSpec((d, tn), lambda bi, i, j: (0, j))],
        out_specs=pl.BlockSpec((1, tm, tn), lambda bi, i, j: (bi, i, j)),
        out_shape=jax.ShapeDtypeStruct((b, tt, N_PACK), BF16),
        scratch_shapes=[pltpu.VMEM((tm, d), BF16)],
        compiler_params=_params("arbitrary", "arbitrary", "arbitrary"),
        name="inproj",
    )(xa, mod, w_pack)


def _mla_kernel(cq_ref, ckv_ref, kr_ref, wq_ref, wk_ref, wv_ref, qn_ref, kvn_ref,
                ct_ref, st_ref, ctq_ref, stq_ref, o_ref, k_s, v_s, *, tc, tt, scale, skip_ctx):
    hps = MLA_HEADS_PER_STEP

    @pl.when(pl.program_id(2) == 0)
    def _():
        ckv = ckv_ref[0].astype(F32)
        r = lax.rsqrt(jnp.mean(ckv * ckv, axis=-1, keepdims=True) + RMS_EPS)
        ckvn = (ckv * r * kvn_ref[...]).astype(BF16)
        kr = kr_ref[0].astype(F32)
        kroped = (kr * ct_ref[...] + pltpu.roll(kr * st_ref[...], 64, 1)).astype(BF16)
        for hh in range(hps):
            k_s[hh, :, 0:MLA_NOPE] = jnp.dot(ckvn, wk_ref[hh], preferred_element_type=F32).astype(BF16)
            k_s[hh, :, MLA_NOPE:] = kroped
            v_s[hh] = jnp.dot(ckvn, wv_ref[hh], preferred_element_type=F32).astype(BF16)

    cq = cq_ref[0].astype(F32)
    r = lax.rsqrt(jnp.mean(cq * cq, axis=-1, keepdims=True) + RMS_EPS)
    cqn = (cq * r * qn_ref[...]).astype(BF16)
    qfs = []
    for hh in range(hps):
        q = jnp.dot(cqn, wq_ref[hh], preferred_element_type=F32)
        qh = q[:, MLA_NOPE:]
        qr = qh * ctq_ref[...] + pltpu.roll(qh * stq_ref[...], 64, 1)
        qfs.append((jnp.concatenate([q[:, :MLA_NOPE], qr], axis=1) * (scale * LOG2E)).astype(BF16))

    def attend(nk):
        outs = []
        for hh in range(hps):
            s = lax.dot_general(qfs[hh], k_s[hh, 0:nk, :], (((1,), (1,)), ((), ())),
                                preferred_element_type=F32)
            p = jnp.exp2(s - jnp.max(s, axis=1, keepdims=True))
            l = jnp.sum(p, axis=1, keepdims=True)
            o = jnp.dot(p.astype(BF16), v_s[hh, 0:nk, :], preferred_element_type=F32)
            outs.append(o * (1.0 / l))
        o_ref[0] = jnp.concatenate(outs, axis=1).astype(o_ref.dtype)

    _attend_blocks(attend, tc, tt, skip_ctx)


def _attend_blocks(attend, tc, tt, skip_ctx):
    if skip_ctx:
        attend(tt)
        return

    @pl.when(pl.program_id(2) == 0)
    def _():
        attend(tc)

    @pl.when(pl.program_id(2) > 0)
    def _():
        attend(tt)


def _once(shape, index_map):
    return pl.BlockSpec(shape, index_map, pipeline_mode=pl.Buffered(1))


def _mla(proj, wq, wk, wv, q_norm, kv_norm, ct, st, tc, skip_ctx):
    b, tt, _ = proj.shape
    tq = Q_BLOCK
    assert tc == tq
    scale = (MLA_NOPE + MLA_ROPE) ** -0.5
    hps = MLA_HEADS_PER_STEP
    sk = int(skip_ctx)
    return pl.pallas_call(
        functools.partial(_mla_kernel, tc=tc, tt=tt, scale=scale, skip_ctx=skip_ctx),
        grid=(b, MLA_HEADS // hps, tt // tq - sk),
        in_specs=[
            pl.BlockSpec((1, tq, MLA_Q_LORA), lambda bi, h, i: (bi, i + sk, OFF_CQ // MLA_Q_LORA)),
            _once((1, tt, MLA_KV_LORA), lambda bi, h, i: (bi, 0, OFF_CKV // MLA_KV_LORA)),
            _once((1, tt, 128), lambda bi, h, i: (bi, 0, OFF_KR // 128)),
            pl.BlockSpec((hps, MLA_Q_LORA, 256), lambda bi, h, i: (h, 0, 0)),
            pl.BlockSpec((hps, MLA_KV_LORA, MLA_NOPE), lambda bi, h, i: (h, 0, 0)),
            pl.BlockSpec((hps, MLA_KV_LORA, MLA_V), lambda bi, h, i: (h, 0, 0)),
            pl.BlockSpec((1, MLA_Q_LORA), lambda bi, h, i: (0, 0)),
            pl.BlockSpec((1, MLA_KV_LORA), lambda bi, h, i: (0, 0)),
            _once((tt, 128), lambda bi, h, i: (0, 0)),
            _once((tt, 128), lambda bi, h, i: (0, 0)),
            pl.BlockSpec((tq, 128), lambda bi, h, i: (i + sk, 0)),
            pl.BlockSpec((tq, 128), lambda bi, h, i: (i + sk, 0)),
        ],
        out_specs=pl.BlockSpec((1, tq, hps * MLA_V), lambda bi, h, i: (bi, i + sk, h)),
        out_shape=jax.ShapeDtypeStruct((b, tt, BRANCH_W), BF16),
        scratch_shapes=[pltpu.VMEM((hps, tt, 256), BF16), pltpu.VMEM((hps, tt, MLA_V), BF16)],
        compiler_params=_params("arbitrary", "arbitrary", "arbitrary"),
        name="mla_attn",
    )(proj, proj, proj, wq, wk, wv, q_norm.reshape(1, -1), kv_norm.reshape(1, -1), ct, st, ct, st)


def _diff_kernel(q_ref, k_ref, v_ref, lq_ref, lk_ref, g_ref, ct_ref, st_ref, ctq_ref, stq_ref,
                 o_ref, k_s, *, tc, tt, lam_init, skip_ctx):
    hps = ATTN_HEADS_PER_STEP
    lane = lax.broadcasted_iota(jnp.int32, (1, 128), 1)
    first_half = (lane & 32) == 0

    def rope(x, c, s):
        xs = jnp.where(first_half, pltpu.roll(x, 96, 1), pltpu.roll(x, 32, 1))
        return x * c + xs * s

    @pl.when(pl.program_id(2) == 0)
    def _():
        for hh in range(hps):
            k_s[hh] = rope(k_ref[0, :, hh * 128:(hh + 1) * 128].astype(F32), ct_ref[...],
                           st_ref[...]).astype(BF16)

    qs = []
    for hh in range(hps):
        q = rope(q_ref[0, :, hh * 128:(hh + 1) * 128].astype(F32), ctq_ref[...], stq_ref[...]) * (
            DIFF_HEAD_DIM ** -0.5 * LOG2E)
        qs.append((jnp.where(lane < 64, q, 0.0).astype(BF16), jnp.where(lane >= 64, q, 0.0).astype(BF16)))
    lqk = lq_ref[...] * lk_ref[...]
    lam = (jnp.exp(jnp.sum(lqk[0:1], axis=1, keepdims=True))
           - jnp.exp(jnp.sum(lqk[1:2], axis=1, keepdims=True)) + lam_init)

    def attend(nk):
        nt = (((1,), (1,)), ((), ()))
        outs = []
        for hh in range(hps):
            kk = k_s[hh, 0:nk, :]
            s1 = lax.dot_general(qs[hh][0], kk, nt, preferred_element_type=F32)
            s2 = lax.dot_general(qs[hh][1], kk, nt, preferred_element_type=F32)
            e1 = jnp.exp2(s1 - jnp.max(s1, axis=1, keepdims=True))
            e2 = jnp.exp2(s2 - jnp.max(s2, axis=1, keepdims=True))
            l1 = jnp.sum(e1, axis=1, keepdims=True)
            l2 = jnp.sum(e2, axis=1, keepdims=True)
            w = e1 - e2 * (lam * l1 / l2)
            o = jnp.dot(w.astype(BF16), v_ref[0, 0:nk, hh * 128:(hh + 1) * 128],
                        preferred_element_type=F32) * (1.0 / l1)
            y = o * lax.rsqrt(jnp.mean(o * o, axis=-1, keepdims=True) + RMS_EPS)
            outs.append(y * g_ref[...] * (1.0 - lam_init))
        o_ref[0] = jnp.concatenate(outs, axis=1).astype(o_ref.dtype)

    _attend_blocks(attend, tc, tt, skip_ctx)


def _diff(proj, lam_q, lam_k, norm_g, ct, st, lam_init, tc, skip_ctx):
    b, tt, _ = proj.shape
    tq = Q_BLOCK
    assert tc == tq
    hps = ATTN_HEADS_PER_STEP
    hw = 128 * hps
    sk = int(skip_ctx)
    return pl.pallas_call(
        functools.partial(_diff_kernel, tc=tc, tt=tt, lam_init=lam_init, skip_ctx=skip_ctx),
        grid=(b, DIFF_HEADS // hps, tt // tq - sk),
        in_specs=[
            pl.BlockSpec((1, tq, hw), lambda bi, h, i: (bi, i + sk, OFF_QD // hw + h)),
            pl.BlockSpec((1, tt, hw), lambda bi, h, i: (bi, 0, OFF_KD // hw + h)),
            pl.BlockSpec((1, tt, hw), lambda bi, h, i: (bi, 0, OFF_VD // hw + h)),
            pl.BlockSpec((2, DIFF_HEAD_DIM), lambda bi, h, i: (0, 0)),
            pl.BlockSpec((2, DIFF_HEAD_DIM), lambda bi, h, i: (0, 0)),
            pl.BlockSpec((1, 128), lambda bi, h, i: (0, 0)),
            _once((tt, 128), lambda bi, h, i: (0, 0)),
            _once((tt, 128), lambda bi, h, i: (0, 0)),
            pl.BlockSpec((tq, 128), lambda bi, h, i: (i + sk, 0)),
            pl.BlockSpec((tq, 128), lambda bi, h, i: (i + sk, 0)),
        ],
        out_specs=pl.BlockSpec((1, tq, hw), lambda bi, h, i: (bi, i + sk, h)),
        out_shape=jax.ShapeDtypeStruct((b, tt, BRANCH_W), BF16),
        scratch_shapes=[pltpu.VMEM((hps, tt, 128), BF16)],
        compiler_params=_params("arbitrary", "arbitrary", "arbitrary"),
        name="diff_attn",
    )(proj, proj, proj, lam_q, lam_k, norm_g.reshape(1, -1), ct, st, ct, st)


def _s5_kernel(x_ref, t_ref, uh_ref, hy_ref, p_ref, o_ref, tok_s, xf_s, ys_s, vf_s, vfs_s, vb_s, vbs_s,
               hf_s, hb_s, *, nb, nch, nctx):
    q = S5_CHUNK
    r = nb * nch
    g8 = pl.program_id(0) % 8
    lane_blk = lax.broadcasted_iota(jnp.int32, (1, 128), 1) // S5_GROUP
    u32 = jnp.uint32

    @pl.when(g8 == 0)
    def _():
        tok_s[...] = x_ref[...].astype(F32)
        for i in range(q):
            xf_s[i] = pltpu.bitcast(tok_s[pl.ds(i, r, stride=q), :].astype(BF16), u32)

    cols = []
    for k in range(q // 8):
        acc = jnp.zeros((r // 2, 128), u32)
        for i8 in range(8):
            rot = pltpu.roll(xf_s[k * 8 + i8], lax.rem((i8 - g8) * S5_GROUP + 128, 128), 1)
            acc = jnp.where(lane_blk == i8, rot, acc)
        cols.append(acc)
    x = pltpu.bitcast(jnp.concatenate(cols, axis=1), BF16)

    yloc = jnp.dot(x, t_ref[0], preferred_element_type=F32)
    v = jnp.dot(x, uh_ref[0], preferred_element_type=F32)
    vf_s[...] = v[:, 0:128]
    vfs_s[...] = v[:, 128:256]
    vb_s[...] = v[:, 256:384]
    vbs_s[...] = v[:, 384:512]
    pm = p_ref[0]
    zero = jnp.zeros((nb, 128), F32)

    def scan(order, v_ref, vs_ref, h_ref, p1, p2):
        h, hs = zero, zero
        for c in order:
            rows = pl.ds(c, nb, stride=nch)
            h_ref[rows, :] = h
            h, hs = h * p1 + hs * p2 + v_ref[rows, :], hs * p1 - h * p2 + vs_ref[rows, :]

    scan(range(nch), vf_s, vfs_s, hf_s, pm[0:1], pm[1:2])
    scan(list(range(nctx - 1, -1, -1)) + list(range(nch - 1, nctx - 1, -1)), vb_s, vbs_s, hb_s,
         pm[2:3], pm[3:4])
    hcat = jnp.concatenate([hf_s[...], hb_s[...]], axis=1).astype(BF16)
    y = yloc + jnp.dot(hcat, hy_ref[0], preferred_element_type=F32)
    ys_s[g8] = pltpu.bitcast(y.astype(BF16), u32)

    @pl.when(g8 == 7)
    def _():
        for k in range(q // 8):
            for i8 in range(8):
                out = jnp.zeros((r // 2, 128), u32)
                for gg in range(8):
                    yk = ys_s[gg, :, k * 128:(k + 1) * 128]
                    rot = yk if gg == i8 else pltpu.roll(yk, (gg - i8) * S5_GROUP % 128, 1)
                    out = jnp.where(lane_blk == gg, rot, out)
                tok_s[pl.ds(k * 8 + i8, r, stride=q), :] = pltpu.bitcast(out, BF16).astype(F32)
        o_ref[...] = tok_s[...].astype(o_ref.dtype)


def _s5_weights(lam_re, lam_im, log_dt, b_re, b_im, c_re, c_im, d):
    q = S5_CHUNK
    g, p, s = S5_GROUPS, S5_STATE, S5_GROUP
    hi = lax.Precision.HIGHEST
    lr, li = lam_re.astype(F32), lam_im.astype(F32)
    dt = jnp.exp(log_dt.astype(F32))[..., None]
    tau = jnp.arange(q + 1, dtype=F32)[None, :, None, None]
    mag = jnp.exp((lr * dt)[:, None] * tau)
    ang = (li * dt)[:, None] * tau
    ar, ai = mag * jnp.cos(ang), mag * jnp.sin(ang)
    den = lr * lr + li * li
    fr = ((ar[:, 1] - 1.0) * lr + ai[:, 1] * li) / den
    fi = (ai[:, 1] * lr - (ar[:, 1] - 1.0) * li) / den
    br, bi = b_re.astype(F32), b_im.astype(F32)
    bbr = fr[..., None] * br - fi[..., None] * bi
    bbi = fr[..., None] * bi + fi[..., None] * br
    cr, ci = c_re.astype(F32)[:, None], c_im.astype(F32)[:, None]
    car = cr * ar[:, :, :, None, :] - ci * ai[:, :, :, None, :]
    cai = cr * ai[:, :, :, None, :] + ci * ar[:, :, :, None, :]
    car_x = jnp.repeat(jnp.swapaxes(car[:, :q], 3, 4), s, axis=-1)
    cai_x = jnp.repeat(jnp.swapaxes(cai[:, :q], 3, 4), s, axis=-1)
    bbr_x = jnp.tile(bbr, (1, 1, 1, s))[:, None]
    bbi_x = jnp.tile(bbi, (1, 1, 1, s))[:, None]
    kern = jnp.sum(car_x * bbr_x - cai_x * bbi_x, axis=3).reshape(2, q, g, s, s)
    ii = np.arange(q)
    lag = ii[None, :, None] - ii[:, None, None]
    sel_f = jnp.asarray(lag == ii[None, None, :], F32)
    sel_b = jnp.asarray(-lag == ii[None, None, :], F32)
    tmat = (jnp.einsum('jit,tgos->gjsio', sel_f, kern[0], precision=hi)
            + jnp.einsum('jit,tgos->gjsio', sel_b, kern[1], precision=hi))
    eye = jnp.eye(q, dtype=F32)[:, None, :, None] * jnp.eye(s, dtype=F32)[None, :, None, :]
    tmat = tmat + eye[None] * d.astype(F32).reshape(g, 1, s, 1, 1)
    tmat = tmat.reshape(g, q * s, q * s)

    def chunk_in(are, aim, bre, bim):
        re = jnp.einsum('jgp,gpi->gjip', are, bre) - jnp.einsum('jgp,gpi->gjip', aim, bim)
        im = jnp.einsum('jgp,gpi->gjip', are, bim) + jnp.einsum('jgp,gpi->gjip', aim, bre)
        return jnp.concatenate([re, im, im, re], axis=-1)

    uh = jnp.concatenate([chunk_in(ar[0, :q][::-1], ai[0, :q][::-1], bbr[0], bbi[0]),
                          chunk_in(ar[1, :q], ai[1, :q], bbr[1], bbi[1])], axis=-1).reshape(g, q * s, 8 * p)

    def state_out(re, im):
        return jnp.concatenate([jnp.transpose(re, (1, 3, 0, 2)), -jnp.transpose(im, (1, 3, 0, 2))], axis=1)

    hy = jnp.concatenate([state_out(car[0, 1:], cai[0, 1:]),
                          state_out(car[1, 1:][::-1], cai[1, 1:][::-1])], axis=1).reshape(g, 4 * p, q * s)
    pm = jnp.stack([jnp.concatenate([ar[0, q], ar[0, q]], -1), jnp.concatenate([-ai[0, q], ai[0, q]], -1),
                    jnp.concatenate([ar[1, q], ar[1, q]], -1), jnp.concatenate([-ai[1, q], ai[1, q]], -1)],
                   axis=1)
    pm = jnp.concatenate([pm, jnp.zeros_like(pm)], axis=1)
    return tmat.astype(BF16), uh.astype(BF16), hy.astype(BF16), pm


def _s5(proj, weights, tc):
    b, tt, _ = proj.shape
    q, g, s = S5_CHUNK, S5_GROUPS, S5_GROUP
    nch = tt // q
    tmat, uh, hy, pm = weights
    r = nch * b
    rows = b * tt
    y = pl.pallas_call(
        functools.partial(_s5_kernel, nb=b, nch=nch, nctx=tc // q),
        grid=(g,),
        in_specs=[_once((rows, 128), lambda gi: (0, OFF_U // 128 + gi // 8)),
                  pl.BlockSpec((1, q * s, q * s), lambda gi: (gi, 0, 0)),
                  pl.BlockSpec((1, q * s, 512), lambda gi: (gi, 0, 0)),
                  pl.BlockSpec((1, 256, q * s), lambda gi: (gi, 0, 0)),
                  pl.BlockSpec((1, 8, 128), lambda gi: (gi, 0, 0))],
        out_specs=pl.BlockSpec((rows, 128), lambda gi: (0, gi // 8)),
        out_shape=jax.ShapeDtypeStruct((rows, BRANCH_W), BF16),
        scratch_shapes=[pltpu.VMEM((rows, 128), F32), pltpu.VMEM((q, r // 2, 128), jnp.uint32),
                        pltpu.VMEM((8, r // 2, q * s), jnp.uint32)] + [pltpu.VMEM((r, 128), F32)] * 6,
        compiler_params=_params("arbitrary"),
        name="s5",
    )(proj.reshape(rows, N_PACK), tmat, uh, hy, pm)
    return y.reshape(b, tt, BRANCH_W)


def _ssd_kernel(*refs, backward, first, nctx_chunks, nchunks):
    if first:
        (xbc_ref, prev_ref, next_ref, z_ref, dt_ref, cw_ref, cb_ref, dtb_ref, alog_ref, dsk_ref,
         sel_ref, o_ref, st_s) = refs
        yin_ref = ng_ref = None
    else:
        (xbc_ref, prev_ref, next_ref, z_ref, dt_ref, cw_ref, cb_ref, dtb_ref, alog_ref, dsk_ref,
         sel_ref, yin_ref, ng_ref, o_ref, st_s) = refs
    qn = SSD_CHUNK
    step = pl.program_id(1)
    if backward:
        c = jnp.where(step < nctx_chunks, nctx_chunks - 1 - step, nchunks - 1 - (step - nctx_chunks))
    else:
        c = step

    @pl.when(step == 0)
    def _():
        st_s[...] = jnp.zeros_like(st_s)

    x = xbc_ref[0].astype(F32)
    has_prev = jnp.logical_and(c != 0, c != nctx_chunks)
    has_next = jnp.logical_and(c != nctx_chunks - 1, c != nchunks - 1)
    prow = jnp.where(has_prev, prev_ref[0, 7:8, :].astype(F32), 0.0)
    nrow = jnp.where(has_next, next_ref[0, 0:1, :].astype(F32), 0.0)
    rid = lax.broadcasted_iota(jnp.int32, (qn, 1), 0)
    xm = jnp.where(rid == 0, prow, pltpu.roll(x, 1, 0))
    xp = jnp.where(rid == qn - 1, nrow, pltpu.roll(x, qn - 1, 0))
    cw = cw_ref[...]
    conv = xm * cw[0:1] + x * cw[1:2] + xp * cw[2:3] + cb_ref[...]
    act = _silu(conv)
    xs = act[:, :BRANCH_W]
    gn = SSD_GROUPS * SSD_STATE
    bm = act[:, BRANCH_W:BRANCH_W + gn].astype(BF16)
    cm = act[:, BRANCH_W + gn:].astype(BF16)

    dt = _softplus(dt_ref[0].astype(F32) + dtb_ref[...])
    a = -jnp.exp(alog_ref[...])
    da = dt * a
    ri = lax.broadcasted_iota(jnp.int32, (qn, qn), 0)
    ci = lax.broadcasted_iota(jnp.int32, (qn, qn), 1)
    causal = (ci >= ri) if backward else (ci <= ri)
    ones_tri = jnp.where(causal, 1.0, 0.0).astype(BF16)
    d1 = da.astype(BF16)
    r1 = da - d1.astype(F32)
    d2 = r1.astype(BF16)
    d3 = (r1 - d2.astype(F32)).astype(BF16)
    cum = (jnp.dot(ones_tri, d1, preferred_element_type=F32)
           + jnp.dot(ones_tri, d2, preferred_element_type=F32)
           + jnp.dot(ones_tri, d3, preferred_element_type=F32))
    edge = cum[0:1] if backward else cum[qn - 1:qn]
    cum_t = cum.T
    dt_t = dt.T
    w_edge = dt * jnp.exp(edge - cum)
    e_in = jnp.exp(cum)

    def expand(v):
        v1 = v.astype(BF16)
        v2 = (v - v1.astype(F32)).astype(BF16)
        return (jnp.dot(v1, sel_ref[...], preferred_element_type=F32)
                + jnp.dot(v2, sel_ref[...], preferred_element_type=F32))

    xw = (xs * expand(w_edge)).astype(BF16)
    e_in_x = expand(e_in)
    xs_b = xs.astype(BF16)
    rpg = SSD_HEADS // SSD_GROUPS
    gw = rpg * SSD_HEAD_DIM
    ys = []
    for g in range(SSD_GROUPS):
        bg = bm[:, g * SSD_STATE:(g + 1) * SSD_STATE]
        cg = cm[:, g * SSD_STATE:(g + 1) * SSD_STATE]
        cb = lax.dot_general(cg, bg, (((1,), (1,)), ((), ())), preferred_element_type=F32)
        st_g = st_s[g * gw:(g + 1) * gw, :]
        y_off = lax.dot_general(cg, st_g.astype(BF16), (((1,), (1,)), ((), ())),
                                preferred_element_type=F32)
        yg = y_off * e_in_x[:, g * gw:(g + 1) * gw]
        parts = []
        for r in range(rpg):
            h = g * rpg + r
            dec = jnp.exp(jnp.where(causal, cum[:, h:h + 1] - cum_t[h:h + 1, :], -jnp.inf))
            wmat = (cb * dec * dt_t[h:h + 1, :]).astype(BF16)
            parts.append(jnp.dot(wmat, xs_b[:, h * SSD_HEAD_DIM:(h + 1) * SSD_HEAD_DIM],
                                 preferred_element_type=F32))
        ys.append(yg + jnp.concatenate(parts, axis=1))
        new = lax.dot_general(xw[:, g * gw:(g + 1) * gw], bg, (((0,), (0,)), ((), ())),
                              preferred_element_type=F32)
        for r in range(rpg):
            h = g * rpg + r
            rows = slice(g * gw + r * SSD_HEAD_DIM, g * gw + (r + 1) * SSD_HEAD_DIM)
            cd = jnp.exp(edge[:, h:h + 1])
            st_s[rows, :] = st_s[rows, :] * cd + new[r * SSD_HEAD_DIM:(r + 1) * SSD_HEAD_DIM, :]
    y = jnp.concatenate(ys, axis=1)
    if first:
        o_ref[0] = y + xs * dsk_ref[...]
    else:
        y = y + yin_ref[0]
        zz = z_ref[0].astype(F32)
        y = y * _silu(zz)
        y = y * lax.rsqrt(jnp.mean(y * y, axis=-1, keepdims=True) + RMS_EPS) * ng_ref[...]
        o_ref[0] = y.astype(o_ref.dtype)


def _ssd_pass(proj, conv_w, conv_b, dt_bias, a_log, d_skip, sel, yin, norm_g, tc, backward):
    b, tt, _ = proj.shape
    qn = SSD_CHUNK
    nchunks = tt // qn
    nctx = tc // qn
    first = yin is None
    last_blk8 = tt // 8 - 1

    def chunk_of(s):
        if backward:
            return jnp.where(s < nctx, nctx - 1 - s, nchunks - 1 - (s - nctx))
        return s

    xbc_blk = OFF_XBC // SSD_CONV_CH
    in_specs = [
        pl.BlockSpec((1, qn, SSD_CONV_CH), lambda bi, s: (bi, chunk_of(s), xbc_blk)),
        pl.BlockSpec((1, 8, SSD_CONV_CH),
                     lambda bi, s: (bi, jnp.maximum(chunk_of(s) * (qn // 8) - 1, 0), xbc_blk)),
        pl.BlockSpec((1, 8, SSD_CONV_CH),
                     lambda bi, s: (bi, jnp.minimum((chunk_of(s) + 1) * (qn // 8), last_blk8), xbc_blk)),
        pl.BlockSpec((1, qn, BRANCH_W), lambda bi, s: (bi, chunk_of(s), OFF_Z // BRANCH_W)),
        pl.BlockSpec((1, qn, 128), lambda bi, s: (bi, chunk_of(s), OFF_DT // 128)),
        pl.BlockSpec((8, SSD_CONV_CH), lambda bi, s: (0, 0)),
        pl.BlockSpec((1, SSD_CONV_CH), lambda bi, s: (0, 0)),
        pl.BlockSpec((1, 128), lambda bi, s: (0, 0)),
        pl.BlockSpec((1, 128), lambda bi, s: (0, 0)),
        pl.BlockSpec((1, BRANCH_W), lambda bi, s: (0, 0)),
        pl.BlockSpec((128, BRANCH_W), lambda bi, s: (0, 0)),
    ]
    args = [proj, proj, proj, proj, proj, conv_w, conv_b, dt_bias, a_log, d_skip, sel]
    if not first:
        in_specs += [pl.BlockSpec((1, qn, BRANCH_W), lambda bi, s: (bi, chunk_of(s), 0)),
                     pl.BlockSpec((1, BRANCH_W), lambda bi, s: (0, 0))]
        args += [yin, norm_g]
    return pl.pallas_call(
        functools.partial(_ssd_kernel, backward=backward, first=first, nctx_chunks=nctx, nchunks=nchunks),
        grid=(b, nchunks),
        in_specs=in_specs,
        out_specs=pl.BlockSpec((1, qn, BRANCH_W), lambda bi, s: (bi, chunk_of(s), 0)),
        out_shape=jax.ShapeDtypeStruct((b, tt, BRANCH_W), F32 if first else BF16),
        scratch_shapes=[pltpu.VMEM((SSD_HEADS * SSD_HEAD_DIM, SSD_STATE), F32)],
        compiler_params=_params("arbitrary", "arbitrary"),
        name="ssd_bwd" if backward else "ssd_fwd",
    )(*args)


def _ssd(proj, conv_w, conv_b, dt_bias, a_log, d_skip, norm_g, tc):
    def lanes(v):
        return jnp.pad(v.astype(F32), (0, 128 - SSD_HEADS)).reshape(1, 128)

    cw = jnp.pad(conv_w.astype(F32), ((0, 5), (0, 0)))
    cb = conv_b.astype(F32).reshape(1, -1)
    dsk = jnp.repeat(d_skip.astype(F32), SSD_HEAD_DIM).reshape(1, -1)
    sel = (jnp.arange(128)[:, None] == (jnp.arange(BRANCH_W)[None, :] // SSD_HEAD_DIM)).astype(BF16)
    y1 = _ssd_pass(proj, cw, cb, lanes(dt_bias[0]), lanes(a_log[0]), dsk, sel, None, None, tc, False)
    return _ssd_pass(proj, cw, cb, lanes(dt_bias[1]), lanes(a_log[1]), jnp.zeros_like(dsk), sel,
                     y1, norm_g.astype(F32).reshape(1, -1), tc, True)


def _merge_kernel(ya_ref, ga_ref, yb_ref, gb_ref, yc_ref, yd_ref, gd_ref, gm_ref, bm_ref,
                  wb_ref, wg_ref, bg_ref, o_ref):
    def f(ref):
        return ref[0].astype(F32)

    g = _gelu_tanh(f(yb_ref))
    glu = g * _sigmoid(jnp.dot(g.astype(BF16), wg_ref[...], preferred_element_type=F32) + bg_ref[...])
    branches = (f(ya_ref) * _silu(f(ga_ref)), glu * _silu(f(gb_ref)), f(yc_ref), f(yd_ref) * _silu(f(gd_ref)))
    acc = None
    for n, br in enumerate(branches):
        gate = _sigmoid(gm_ref[0, :, n * D_MODEL:(n + 1) * D_MODEL].astype(F32) + bm_ref[n:n + 1, :])
        term = gate * jnp.dot(br.astype(BF16), wb_ref[n], preferred_element_type=F32)
        acc = term if acc is None else acc + term
    o_ref[0] = acc.astype(o_ref.dtype)


def _merge(proj, ya, yb, yc, yd, b_merge, w_branch, w_glu, b_glu, skip):
    b, tt, _ = proj.shape
    tm = Q_BLOCK
    bw = BRANCH_W

    def pspec(off):
        return pl.BlockSpec((1, tm, bw), lambda bi, i: (bi, i + skip, off // bw))

    yspec = pl.BlockSpec((1, tm, bw), lambda bi, i: (bi, i + skip, 0))
    once = dict(pipeline_mode=pl.Buffered(1))
    return pl.pallas_call(
        _merge_kernel,
        grid=(b, tt // tm - skip),
        in_specs=[yspec, pspec(OFF_GA), yspec, pspec(OFF_GB), yspec, yspec, pspec(OFF_GD),
                  pl.BlockSpec((1, tm, N_BRANCH * D_MODEL), lambda bi, i: (bi, i + skip, 0)),
                  pl.BlockSpec((N_BRANCH, D_MODEL), lambda bi, i: (0, 0)),
                  pl.BlockSpec((N_BRANCH, bw, D_MODEL), lambda bi, i: (0, 0, 0), **once),
                  pl.BlockSpec((bw, bw), lambda bi, i: (0, 0), **once),
                  pl.BlockSpec((1, bw), lambda bi, i: (0, 0))],
        out_specs=pl.BlockSpec((1, tm, D_MODEL), lambda bi, i: (bi, i + skip, 0)),
        out_shape=jax.ShapeDtypeStruct((b, tt, D_MODEL), BF16),
        compiler_params=_params("arbitrary", "arbitrary"),
        name="merge",
    )(ya, proj, yb, proj, yc, yd, proj, proj, b_merge, w_branch, w_glu, b_glu.reshape(1, -1))


def _out_kernel(m_ref, x_ref, mod_ref, w_ref, lg_ref, lb_ref, o_ref, *, tc, tm, row0):
    i = pl.program_id(1)
    out = jnp.dot(m_ref[0], w_ref[...], preferred_element_type=F32)
    row = row0 + i * tm + lax.broadcasted_iota(jnp.int32, (tm, 1), 0)
    m = mod_ref[0]
    gate = jnp.where(row < tc, m[4:5], m[5:6])
    v = DEEPNORM_ALPHA * x_ref[0] + gate * out
    mu = jnp.mean(v, axis=-1, keepdims=True)
    vc = v - mu
    var = jnp.mean(vc * vc, axis=-1, keepdims=True)
    o_ref[0] = vc * lax.rsqrt(var + LN_EPS) * lg_ref[...] + lb_ref[...]


def _out(merged, xa, mod, w_out, ln_g, ln_b, tc, latent_only):
    b, tt, d = xa.shape
    tm = Q_BLOCK
    skip = tc // tm if latent_only else 0
    nblk = tt // tm - skip
    spec = pl.BlockSpec((1, tm, d), lambda bi, i: (bi, i + skip, 0))
    return pl.pallas_call(
        functools.partial(_out_kernel, tc=tc, tm=tm, row0=skip * tm),
        grid=(b, nblk),
        in_specs=[spec, spec,
                  pl.BlockSpec((1, 8, d), lambda bi, i: (bi, 0, 0)),
                  pl.BlockSpec((d, d), lambda bi, i: (0, 0)),
                  pl.BlockSpec((1, d), lambda bi, i: (0, 0)),
                  pl.BlockSpec((1, d), lambda bi, i: (0, 0))],
        out_specs=pl.BlockSpec((1, tm, d), lambda bi, i: (bi, i, 0)),
        out_shape=jax.ShapeDtypeStruct((b, nblk * tm, d), F32),
        compiler_params=_params("arbitrary", "arbitrary"),
        name="out_proj",
    )(merged, xa, mod, w_out, ln_g.reshape(1, -1), ln_b.reshape(1, -1))


def _half_split(w, heads):
    k = w.shape[0]
    return w.reshape(k, heads, ROPE_DIM // 2, 2).transpose(0, 1, 3, 2).reshape(k, heads * ROPE_DIM)


def _pack_w_in(w):
    sp = np.cumsum([0, MLA_Q_LORA, MLA_KV_LORA, MLA_ROPE, BRANCH_W, BRANCH_W, BRANCH_W, BRANCH_W,
                    SSD_CONV_CH, SSD_HEADS, BRANCH_W, BRANCH_W, BRANCH_W, BRANCH_W, N_BRANCH * D_MODEL])
    seg = [w[:, sp[k]:sp[k + 1]] for k in range(14)]
    cq, ckv, kr, ga, u, gb, z, xbc, dt, qd, kd, vd, gd, gm = seg
    kr_hs = _half_split(kr, 1)
    kr_sw = jnp.concatenate([kr_hs[:, 32:], kr_hs[:, :32]], axis=1)
    pad = jnp.zeros((w.shape[0], 128 - SSD_HEADS), w.dtype)
    packed = jnp.concatenate([gm, xbc, ga, u, gb, z, _half_split(qd, 2 * DIFF_HEADS),
                              _half_split(kd, 2 * DIFF_HEADS), vd, gd, cq, ckv, kr_hs, kr_sw, dt, pad],
                             axis=1)
    return packed.astype(BF16)


def _pack_mla(w_uq, w_ukv):
    k = w_uq.shape[0]
    wq = w_uq.reshape(k, MLA_HEADS, MLA_NOPE + MLA_ROPE)
    rope = wq[:, :, MLA_NOPE:].reshape(k, MLA_HEADS, ROPE_DIM // 2, 2)
    ev, od = rope[..., 0], rope[..., 1]
    wq = jnp.concatenate([wq[:, :, :MLA_NOPE], ev, od, od, ev], axis=-1)
    wkv = w_ukv.reshape(w_ukv.shape[0], MLA_HEADS, MLA_NOPE + MLA_V)
    return (wq.transpose(1, 0, 2).astype(BF16),
            wkv[:, :, :MLA_NOPE].transpose(1, 0, 2).astype(BF16),
            wkv[:, :, MLA_NOPE:].transpose(1, 0, 2).astype(BF16))


def _rope_tables(tl, tc):
    rows = tl // GRID_W
    row_id = jnp.repeat(jnp.arange(rows, dtype=F32), GRID_W)
    col_id = jnp.tile(jnp.arange(GRID_W, dtype=F32), rows)
    quarter = ROPE_DIM // 4
    inv_freq = ROPE_BASE ** (-jnp.arange(quarter, dtype=F32) / quarter)
    ang = jnp.concatenate([row_id[:, None] * inv_freq, col_id[:, None] * inv_freq], axis=-1)
    cos = jnp.concatenate([jnp.ones((tc, ROPE_DIM // 2), F32), jnp.cos(ang)], axis=0)
    sin = jnp.concatenate([jnp.zeros((tc, ROPE_DIM // 2), F32), jnp.sin(ang)], axis=0)
    zero = jnp.zeros_like(cos)
    mla = (jnp.concatenate([cos, cos, zero, zero], axis=1), jnp.concatenate([zero, zero, -sin, sin], axis=1))
    diff = (jnp.concatenate([cos, cos, cos, cos], axis=1), jnp.concatenate([-sin, sin, -sin, sin], axis=1))
    return mla, diff


def _layer(xa, cs, p, tables, layer_idx, tc, last):
    b = xa.shape[0]
    mla_tab, diff_tab = tables
    mod = _ada(cs, p['w_ada_stacked'], p['b_ada'], layer_idx)
    d = D_MODEL
    shift, scale, gate = mod[:, :d], mod[:, d:2 * d], mod[:, 2 * d:]
    zeros = jnp.zeros((b, d), F32)
    modb = jnp.stack([jnp.broadcast_to(scale[0], (b, d)), jnp.broadcast_to(shift[0], (b, d)),
                      scale[1:1 + b], shift[1:1 + b],
                      jnp.broadcast_to(gate[0], (b, d)), gate[1:1 + b], zeros, zeros], axis=1)
    proj = _inproj(xa, modb, _pack_w_in(p['w_in']), tc)
    wq, wk, wv = _pack_mla(p['mla_w_uq'], p['mla_w_ukv'])
    ya = _mla(proj, wq, wk, wv, p['mla_q_norm'], p['mla_kv_norm'], mla_tab[0], mla_tab[1], tc, last)
    lam_init = 0.8 - 0.6 * math.exp(-0.3 * layer_idx)
    yd = _diff(proj, p['diff_lambda_q'], p['diff_lambda_k'], p['diff_norm'], diff_tab[0], diff_tab[1],
               lam_init, tc, last)
    yb = _s5(proj, _s5_weights(p['s5_lambda_re'], p['s5_lambda_im'], p['s5_log_dt'], p['s5_b_re'],
                               p['s5_b_im'], p['s5_c_re'], p['s5_c_im'], p['s5_d']), tc)
    yc = _ssd(proj, p['ssd_conv_w'], p['ssd_conv_b'], p['ssd_dt_bias'], p['ssd_a_log'], p['ssd_d'],
              p['ssd_norm'], tc)
    merged = _merge(proj, ya, yb, yc, yd, p['b_merge'], p['w_branch'].astype(BF16),
                    p['s5_w_glu'].astype(BF16), p['s5_b_glu'], tc // Q_BLOCK if last else 0)
    return _out(merged, xa, modb, p['w_out'].astype(BF16), p['ln_g'], p['ln_b'], tc, last)


def kernel(x, c, ctx, c_ctx, w_ada, b_ada, w_in, mla_q_norm, mla_w_uq, mla_kv_norm, mla_w_ukv,
           s5_lambda_re, s5_lambda_im, s5_log_dt, s5_b_re, s5_b_im, s5_c_re, s5_c_im, s5_d,
           s5_w_glu, s5_b_glu, ssd_conv_w, ssd_conv_b, ssd_dt_bias, ssd_a_log, ssd_d, ssd_norm,
           diff_lambda_q, diff_lambda_k, diff_norm, b_merge, w_branch, w_out, ln_g, ln_b):
    b, tl, d = x.shape
    tc = ctx.shape[1]
    stacked = dict(
        w_ada=w_ada, b_ada=b_ada, w_in=w_in, mla_q_norm=mla_q_norm, mla_w_uq=mla_w_uq,
        mla_kv_norm=mla_kv_norm, mla_w_ukv=mla_w_ukv, s5_lambda_re=s5_lambda_re, s5_lambda_im=s5_lambda_im,
        s5_log_dt=s5_log_dt, s5_b_re=s5_b_re, s5_b_im=s5_b_im, s5_c_re=s5_c_re, s5_c_im=s5_c_im, s5_d=s5_d,
        s5_w_glu=s5_w_glu, s5_b_glu=s5_b_glu, ssd_conv_w=ssd_conv_w, ssd_conv_b=ssd_conv_b,
        ssd_dt_bias=ssd_dt_bias, ssd_a_log=ssd_a_log, ssd_d=ssd_d, ssd_norm=ssd_norm,
        diff_lambda_q=diff_lambda_q, diff_lambda_k=diff_lambda_k, diff_norm=diff_norm, b_merge=b_merge,
        w_branch=w_branch, w_out=w_out, ln_g=ln_g, ln_b=ln_b)
    tables = _rope_tables(tl, tc)
    cs = jnp.concatenate([c_ctx[None], c, jnp.zeros((8 - 1 - b, d), c.dtype)], axis=0)
    xa = jnp.concatenate([ctx, x], axis=1)
    depth = w_in.shape[0]
    for i in range(depth):
        p = {k: v[i] for k, v in stacked.items() if k != 'w_ada'}
        p['w_ada_stacked'] = w_ada
        xa = _layer(xa, cs, p, tables, i, tc, i == depth - 1)
    return xa
```

```python
import functools
import math

import numpy as np
import jax
import jax.numpy as jnp
from jax import lax
from jax.experimental import pallas as pl
from jax.experimental.pallas import tpu as pltpu

F32 = jnp.float32
BF16 = jnp.bfloat16

D_MODEL = 2048
DEPTH = 2
GRID_W = 64
N_BRANCH = 4
BRANCH_W = 1024
ROPE_DIM = 64
ROPE_BASE = 10000.0

MLA_HEADS = 8
MLA_NOPE = 128
MLA_ROPE = ROPE_DIM
MLA_V = 128
MLA_Q_LORA = 512
MLA_KV_LORA = 256

S5_GROUP = 16
S5_GROUPS = BRANCH_W // S5_GROUP
S5_STATE = 64
S5_CHUNK = 16

SSD_HEAD_DIM = 64
SSD_HEADS = BRANCH_W // SSD_HEAD_DIM
SSD_GROUPS = 4
SSD_STATE = 128
SSD_CHUNK = 128
SSD_CONV_CH = BRANCH_W + 2 * SSD_GROUPS * SSD_STATE

DIFF_HEAD_DIM = ROPE_DIM
DIFF_HEADS = BRANCH_W // (2 * DIFF_HEAD_DIM)

LN_EPS = 1e-5
RMS_EPS = 1e-6
DEEPNORM_ALPHA = (2 * DEPTH) ** 0.25

V7X_VMEM_LIMIT_BYTES = 56 * 1024 * 1024
Q_BLOCK = 256
ATTN_HEADS_PER_STEP = 4
MLA_HEADS_PER_STEP = 8
LOG2E = math.log2(math.e)

OFF_GM = 0
OFF_XBC = 8192
OFF_GA = 10240
OFF_U = 11264
OFF_GB = 12288
OFF_Z = 13312
OFF_QD = 14336
OFF_KD = 15360
OFF_VD = 16384
OFF_GD = 17408
OFF_CQ = 18432
OFF_CKV = 18944
OFF_KR = 19200
OFF_DT = 19328
N_PACK = 19456
IN_TILE_N = 1024


def _sigmoid(x):
    return 0.5 * (1.0 + jnp.tanh(0.5 * x))


def _silu(x):
    return x * _sigmoid(x)


def _gelu_tanh(x):
    return 0.5 * x * (1.0 + jnp.tanh(math.sqrt(2.0 / math.pi) * (x + 0.044715 * (x * x * x))))


def _softplus(x):
    return jnp.maximum(x, 0.0) + jnp.log(1.0 + jnp.exp(-jnp.abs(x)))


def _params(*sem):
    return pltpu.CompilerParams(dimension_semantics=sem, vmem_limit_bytes=V7X_VMEM_LIMIT_BYTES)


def _ada_kernel(c_ref, w_ref, b_ref, o_ref):
    s = _silu(c_ref[...])
    o_ref[...] = jnp.dot(s.astype(BF16), w_ref[0].astype(BF16),
                         preferred_element_type=F32) + b_ref[...]


def _ada(cs, w_ada, b_ada, layer):
    n = w_ada.shape[2]
    tn = 512
    return pl.pallas_call(
        _ada_kernel,
        grid=(n // tn,),
        in_specs=[pl.BlockSpec((8, D_MODEL), lambda j: (0, 0)),
                  pl.BlockSpec((1, D_MODEL, tn), lambda j: (layer, 0, j)),
                  pl.BlockSpec((1, tn), lambda j: (0, j))],
        out_specs=pl.BlockSpec((8, tn), lambda j: (0, j)),
        out_shape=jax.ShapeDtypeStruct((8, n), F32),
        compiler_params=_params("arbitrary"),
        name="ada",
    )(cs, w_ada, b_ada.reshape(1, n))


def _inproj_kernel(x_ref, mod_ref, w_ref, o_ref, h_ref, *, tc, tm):
    i = pl.program_id(1)

    @pl.when(pl.program_id(2) == 0)
    def _():
        x = x_ref[0]
        mu = jnp.mean(x, axis=-1, keepdims=True)
        xc = x - mu
        var = jnp.mean(xc * xc, axis=-1, keepdims=True)
        xn = xc * lax.rsqrt(var + LN_EPS)
        row = i * tm + lax.broadcasted_iota(jnp.int32, (tm, 1), 0)
        is_ctx = row < tc
        m = mod_ref[0]
        scale = jnp.where(is_ctx, m[0:1], m[2:3])
        shift = jnp.where(is_ctx, m[1:2], m[3:4])
        h_ref[...] = (xn * (1.0 + scale) + shift).astype(BF16)

    o_ref[0] = jnp.dot(h_ref[...], w_ref[...], preferred_element_type=F32).astype(o_ref.dtype)


def _inproj(xa, mod, w_pack, tc):
    b, tt, d = xa.shape
    tm = tt // 4
    tn = IN_TILE_N
    return pl.pallas_call(
        functools.partial(_inproj_kernel, tc=tc, tm=tm),
        grid=(b, tt // tm, N_PACK // tn),
        in_specs=[pl.BlockSpec((1, tm, d), lambda bi, i, j: (bi, i, 0)),
                  pl.BlockSpec((1, 8, d), lambda bi, i, j: (bi, 0, 0)),
                  pl.BlockSpec((d, tn), lambda bi, i, j: (0, j))],
        out_specs=pl.BlockSpec((1, tm, tn), lambda bi, i, j: (bi, i, j)),
        out_shape=jax.ShapeDtypeStruct((b, tt, N_PACK), BF16),
        scratch_shapes=[pltpu.VMEM((tm, d), BF16)],
        compiler_params=_params("arbitrary", "arbitrary", "arbitrary"),
        name="inproj",
    )(xa, mod, w_pack)


def _mla_kernel(cq_ref, ckv_ref, kr_ref, wq_ref, wk_ref, wv_ref, qn_ref, kvn_ref,
                ct_ref, st_ref, ctq_ref, stq_ref, o_ref, k_s, v_s, *, tc, tt, scale, skip_ctx):
    hps = MLA_HEADS_PER_STEP

    @pl.when(pl.program_id(2) == 0)
    def _():
        ckv = ckv_ref[0].astype(F32)
        r = lax.rsqrt(jnp.mean(ckv * ckv, axis=-1, keepdims=True) + RMS_EPS)
        ckvn = (ckv * r * kvn_ref[...]).astype(BF16)
        kr = kr_ref[0].astype(F32)
        kroped = (kr * ct_ref[...] + pltpu.roll(kr * st_ref[...], 64, 1)).astype(BF16)
        for hh in range(hps):
            k_s[hh, :, 0:MLA_NOPE] = jnp.dot(ckvn, wk_ref[hh], preferred_element_type=F32).astype(BF16)
            k_s[hh, :, MLA_NOPE:] = kroped
            v_s[hh] = jnp.dot(ckvn, wv_ref[hh], preferred_element_type=F32).astype(BF16)

    cq = cq_ref[0].astype(F32)
    r = lax.rsqrt(jnp.mean(cq * cq, axis=-1, keepdims=True) + RMS_EPS)
    cqn = (cq * r * qn_ref[...]).astype(BF16)
    qfs = []
    for hh in range(hps):
        q = jnp.dot(cqn, wq_ref[hh], preferred_element_type=F32)
        qh = q[:, MLA_NOPE:]
        qr = qh * ctq_ref[...] + pltpu.roll(qh * stq_ref[...], 64, 1)
        qfs.append((jnp.concatenate([q[:, :MLA_NOPE], qr], axis=1) * (scale * LOG2E)).astype(BF16))

    def attend(nk):
        outs = []
        for hh in range(hps):
            s = lax.dot_general(qfs[hh], k_s[hh, 0:nk, :], (((1,), (1,)), ((), ())),
                                preferred_element_type=F32)
            p = jnp.exp2(s - jnp.max(s, axis=1, keepdims=True))
            l = jnp.sum(p, axis=1, keepdims=True)
            o = jnp.dot(p.astype(BF16), v_s[hh, 0:nk, :], preferred_element_type=F32)
            outs.append(o * (1.0 / l))
        o_ref[0] = jnp.concatenate(outs, axis=1).astype(o_ref.dtype)

    _attend_blocks(attend, tc, tt, skip_ctx)


def _attend_blocks(attend, tc, tt, skip_ctx):
    if skip_ctx:
        attend(tt)
        return

    @pl.when(pl.program_id(2) == 0)
    def _():
        attend(tc)

    @pl.when(pl.program_id(2) > 0)
    def _():
        attend(tt)


def _once(shape, index_map):
    return pl.BlockSpec(shape, index_map, pipeline_mode=pl.Buffered(1))


def _mla(proj, wq, wk, wv, q_norm, kv_norm, ct, st, tc, skip_ctx):
    b, tt, _ = proj.shape
    tq = Q_BLOCK
    assert tc == tq
    scale = (MLA_NOPE + MLA_ROPE) ** -0.5
    hps = MLA_HEADS_PER_STEP
    sk = int(skip_ctx)
    return pl.pallas_call(
        functools.partial(_mla_kernel, tc=tc, tt=tt, scale=scale, skip_ctx=skip_ctx),
        grid=(b, MLA_HEADS // hps, tt // tq - sk),
        in_specs=[
            pl.BlockSpec((1, tq, MLA_Q_LORA), lambda bi, h, i: (bi, i + sk, OFF_CQ // MLA_Q_LORA)),
            _once((1, tt, MLA_KV_LORA), lambda bi, h, i: (bi, 0, OFF_CKV // MLA_KV_LORA)),
            _once((1, tt, 128), lambda bi, h, i: (bi, 0, OFF_KR // 128)),
            pl.BlockSpec((hps, MLA_Q_LORA, 256), lambda bi, h, i: (h, 0, 0)),
            pl.BlockSpec((hps, MLA_KV_LORA, MLA_NOPE), lambda bi, h, i: (h, 0, 0)),
            pl.BlockSpec((hps, MLA_KV_LORA, MLA_V), lambda bi, h, i: (h, 0, 0)),
            pl.BlockSpec((1, MLA_Q_LORA), lambda bi, h, i: (0, 0)),
            pl.BlockSpec((1, MLA_KV_LORA), lambda bi, h, i: (0, 0)),
            _once((tt, 128), lambda bi, h, i: (0, 0)),
            _once((tt, 128), lambda bi, h, i: (0, 0)),
            pl.BlockSpec((tq, 128), lambda bi, h, i: (i + sk, 0)),
            pl.BlockSpec((tq, 128), lambda bi, h, i: (i + sk, 0)),
        ],
        out_specs=pl.BlockSpec((1, tq, hps * MLA_V), lambda bi, h, i: (bi, i + sk, h)),
        out_shape=jax.ShapeDtypeStruct((b, tt, BRANCH_W), BF16),
        scratch_shapes=[pltpu.VMEM((hps, tt, 256), BF16), pltpu.VMEM((hps, tt, MLA_V), BF16)],
        compiler_params=_params("arbitrary", "arbitrary", "arbitrary"),
        name="mla_attn",
    )(proj, proj, proj, wq, wk, wv, q_norm.reshape(1, -1), kv_norm.reshape(1, -1), ct, st, ct, st)


def _diff_kernel(q_ref, k_ref, v_ref, lq_ref, lk_ref, g_ref, ct_ref, st_ref, ctq_ref, stq_ref,
                 o_ref, k_s, *, tc, tt, lam_init, skip_ctx):
    hps = ATTN_HEADS_PER_STEP
    lane = lax.broadcasted_iota(jnp.int32, (1, 128), 1)
    first_half = (lane & 32) == 0

    def rope(x, c, s):
        xs = jnp.where(first_half, pltpu.roll(x, 96, 1), pltpu.roll(x, 32, 1))
        return x * c + xs * s

    @pl.when(pl.program_id(2) == 0)
    def _():
        for hh in range(hps):
            k_s[hh] = rope(k_ref[0, :, hh * 128:(hh + 1) * 128].astype(F32), ct_ref[...],
                           st_ref[...]).astype(BF16)

    qs = []
    for hh in range(hps):
        q = rope(q_ref[0, :, hh * 128:(hh + 1) * 128].astype(F32), ctq_ref[...], stq_ref[...]) * (
            DIFF_HEAD_DIM ** -0.5 * LOG2E)
        qs.append((jnp.where(lane < 64, q, 0.0).astype(BF16), jnp.where(lane >= 64, q, 0.0).astype(BF16)))
    lqk = lq_ref[...] * lk_ref[...]
    lam = (jnp.exp(jnp.sum(lqk[0:1], axis=1, keepdims=True))
           - jnp.exp(jnp.sum(lqk[1:2], axis=1, keepdims=True)) + lam_init)

    def attend(nk):
        nt = (((1,), (1,)), ((), ()))
        outs = []
        for hh in range(hps):
            kk = k_s[hh, 0:nk, :]
            s1 = lax.dot_general(qs[hh][0], kk, nt, preferred_element_type=F32)
            s2 = lax.dot_general(qs[hh][1], kk, nt, preferred_element_type=F32)
            e1 = jnp.exp2(s1 - jnp.max(s1, axis=1, keepdims=True))
            e2 = jnp.exp2(s2 - jnp.max(s2, axis=1, keepdims=True))
            l1 = jnp.sum(e1, axis=1, keepdims=True)
            l2 = jnp.sum(e2, axis=1, keepdims=True)
            w = e1 - e2 * (lam * l1 / l2)
            o = jnp.dot(w.astype(BF16), v_ref[0, 0:nk, hh * 128:(hh + 1) * 128],
                        preferred_element_type=F32) * (1.0 / l1)
            y = o * lax.rsqrt(jnp.mean(o * o, axis=-1, keepdims=True) + RMS_EPS)
            outs.append(y * g_ref[...] * (1.0 - lam_init))
        o_ref[0] = jnp.concatenate(outs, axis=1).astype(o_ref.dtype)

    _attend_blocks(attend, tc, tt, skip_ctx)


def _diff(proj, lam_q, lam_k, norm_g, ct, st, lam_init, tc, skip_ctx):
    b, tt, _ = proj.shape
    tq = Q_BLOCK
    assert tc == tq
    hps = ATTN_HEADS_PER_STEP
    hw = 128 * hps
    sk = int(skip_ctx)
    return pl.pallas_call(
        functools.partial(_diff_kernel, tc=tc, tt=tt, lam_init=lam_init, skip_ctx=skip_ctx),
        grid=(b, DIFF_HEADS // hps, tt // tq - sk),
        in_specs=[
            pl.BlockSpec((1, tq, hw), lambda bi, h, i: (bi, i + sk, OFF_QD // hw + h)),
            pl.BlockSpec((1, tt, hw), lambda bi, h, i: (bi, 0, OFF_KD // hw + h)),
            pl.BlockSpec((1, tt, hw), lambda bi, h, i: (bi, 0, OFF_VD // hw + h)),
            pl.BlockSpec((2, DIFF_HEAD_DIM), lambda bi, h, i: (0, 0)),
            pl.BlockSpec((2, DIFF_HEAD_DIM), lambda bi, h, i: (0, 0)),
            pl.BlockSpec((1, 128), lambda bi, h, i: (0, 0)),
            _once((tt, 128), lambda bi, h, i: (0, 0)),
            _once((tt, 128), lambda bi, h, i: (0, 0)),
            pl.BlockSpec((tq, 128), lambda bi, h, i: (i + sk, 0)),
            pl.BlockSpec((tq, 128), lambda bi, h, i: (i + sk, 0)),
        ],
        out_specs=pl.BlockSpec((1, tq, hw), lambda bi, h, i: (bi, i + sk, h)),
        out_shape=jax.ShapeDtypeStruct((b, tt, BRANCH_W), BF16),
        scratch_shapes=[pltpu.VMEM((hps, tt, 128), BF16)],
        compiler_params=_params("arbitrary", "arbitrary", "arbitrary"),
        name="diff_attn",
    )(proj, proj, proj, lam_q, lam_k, norm_g.reshape(1, -1), ct, st, ct, st)


def _s5_kernel(x_ref, t_ref, uh_ref, hy_ref, p_ref, o_ref, tok_s, xf_s, ys_s, vf_s, vfs_s, vb_s, vbs_s,
               hf_s, hb_s, *, nb, nch, nctx):
    q = S5_CHUNK
    r = nb * nch
    g8 = pl.program_id(0) % 8
    lane_blk = lax.broadcasted_iota(jnp.int32, (1, 128), 1) // S5_GROUP
    u32 = jnp.uint32

    @pl.when(g8 == 0)
    def _():
        tok_s[...] = x_ref[...].astype(F32)
        for i in range(q):
            xf_s[i] = pltpu.bitcast(tok_s[pl.ds(i, r, stride=q), :].astype(BF16), u32)

    cols = []
    for k in range(q // 8):
        acc = jnp.zeros((r // 2, 128), u32)
        for i8 in range(8):
            rot = pltpu.roll(xf_s[k * 8 + i8], lax.rem((i8 - g8) * S5_GROUP + 128, 128), 1)
            acc = jnp.where(lane_blk == i8, rot, acc)
        cols.append(acc)
    x = pltpu.bitcast(jnp.concatenate(cols, axis=1), BF16)

    yloc = jnp.dot(x, t_ref[0], preferred_element_type=F32)
    v = jnp.dot(x, uh_ref[0], preferred_element_type=F32)
    vf_s[...] = v[:, 0:128]
    vfs_s[...] = v[:, 128:256]
    vb_s[...] = v[:, 256:384]
    vbs_s[...] = v[:, 384:512]
    pm = p_ref[0]
    zero = jnp.zeros((nb, 128), F32)

    def scan(order, v_ref, vs_ref, h_ref, p1, p2):
        h, hs = zero, zero
        for c in order:
            rows = pl.ds(c, nb, stride=nch)
            h_ref[rows, :] = h
            h, hs = h * p1 + hs * p2 + v_ref[rows, :], hs * p1 - h * p2 + vs_ref[rows, :]

    scan(range(nch), vf_s, vfs_s, hf_s, pm[0:1], pm[1:2])
    scan(list(range(nctx - 1, -1, -1)) + list(range(nch - 1, nctx - 1, -1)), vb_s, vbs_s, hb_s,
         pm[2:3], pm[3:4])
    hcat = jnp.concatenate([hf_s[...], hb_s[...]], axis=1).astype(BF16)
    y = yloc + jnp.dot(hcat, hy_ref[0], preferred_element_type=F32)
    ys_s[g8] = pltpu.bitcast(y.astype(BF16), u32)

    @pl.when(g8 == 7)
    def _():
        for k in range(q // 8):
            for i8 in range(8):
                out = jnp.zeros((r // 2, 128), u32)
                for gg in range(8):
                    yk = ys_s[gg, :, k * 128:(k + 1) * 128]
                    rot = yk if gg == i8 else pltpu.roll(yk, (gg - i8) * S5_GROUP % 128, 1)
                    out = jnp.where(lane_blk == gg, rot, out)
                tok_s[pl.ds(k * 8 + i8, r, stride=q), :] = pltpu.bitcast(out, BF16).astype(F32)
        o_ref[...] = tok_s[...].astype(o_ref.dtype)


def _s5_weights(lam_re, lam_im, log_dt, b_re, b_im, c_re, c_im, d):
    q = S5_CHUNK
    g, p, s = S5_GROUPS, S5_STATE, S5_GROUP
    hi = lax.Precision.HIGHEST
    lr, li = lam_re.astype(F32), lam_im.astype(F32)
    dt = jnp.exp(log_dt.astype(F32))[..., None]
    tau = jnp.arange(q + 1, dtype=F32)[None, :, None, None]
    mag = jnp.exp((lr * dt)[:, None] * tau)
    ang = (li * dt)[:, None] * tau
    ar, ai = mag * jnp.cos(ang), mag * jnp.sin(ang)
    den = lr * lr + li * li
    fr = ((ar[:, 1] - 1.0) * lr + ai[:, 1] * li) / den
    fi = (ai[:, 1] * lr - (ar[:, 1] - 1.0) * li) / den
    br, bi = b_re.astype(F32), b_im.astype(F32)
    bbr = fr[..., None] * br - fi[..., None] * bi
    bbi = fr[..., None] * bi + fi[..., None] * br
    cr, ci = c_re.astype(F32)[:, None], c_im.astype(F32)[:, None]
    car = cr * ar[:, :, :, None, :] - ci * ai[:, :, :, None, :]
    cai = cr * ai[:, :, :, None, :] + ci * ar[:, :, :, None, :]
    car_x = jnp.repeat(jnp.swapaxes(car[:, :q], 3, 4), s, axis=-1)
    cai_x = jnp.repeat(jnp.swapaxes(cai[:, :q], 3, 4), s, axis=-1)
    bbr_x = jnp.tile(bbr, (1, 1, 1, s))[:, None]
    bbi_x = jnp.tile(bbi, (1, 1, 1, s))[:, None]
    kern = jnp.sum(car_x * bbr_x - cai_x * bbi_x, axis=3).reshape(2, q, g, s, s)
    ii = np.arange(q)
    lag = ii[None, :, None] - ii[:, None, None]
    sel_f = jnp.asarray(lag == ii[None, None, :], F32)
    sel_b = jnp.asarray(-lag == ii[None, None, :], F32)
    tmat = (jnp.einsum('jit,tgos->gjsio', sel_f, kern[0], precision=hi)
            + jnp.einsum('jit,tgos->gjsio', sel_b, kern[1], precision=hi))
    eye = jnp.eye(q, dtype=F32)[:, None, :, None] * jnp.eye(s, dtype=F32)[None, :, None, :]
    tmat = tmat + eye[None] * d.astype(F32).reshape(g, 1, s, 1, 1)
    tmat = tmat.reshape(g, q * s, q * s)

    def chunk_in(are, aim, bre, bim):
        re = jnp.einsum('jgp,gpi->gjip', are, bre) - jnp.einsum('jgp,gpi->gjip', aim, bim)
        im = jnp.einsum('jgp,gpi->gjip', are, bim) + jnp.einsum('jgp,gpi->gjip', aim, bre)
        return jnp.concatenate([re, im, im, re], axis=-1)

    uh = jnp.concatenate([chunk_in(ar[0, :q][::-1], ai[0, :q][::-1], bbr[0], bbi[0]),
                          chunk_in(ar[1, :q], ai[1, :q], bbr[1], bbi[1])], axis=-1).reshape(g, q * s, 8 * p)

    def state_out(re, im):
        return jnp.concatenate([jnp.transpose(re, (1, 3, 0, 2)), -jnp.transpose(im, (1, 3, 0, 2))], axis=1)

    hy = jnp.concatenate([state_out(car[0, 1:], cai[0, 1:]),
                          state_out(car[1, 1:][::-1], cai[1, 1:][::-1])], axis=1).reshape(g, 4 * p, q * s)
    pm = jnp.stack([jnp.concatenate([ar[0, q], ar[0, q]], -1), jnp.concatenate([-ai[0, q], ai[0, q]], -1),
                    jnp.concatenate([ar[1, q], ar[1, q]], -1), jnp.concatenate([-ai[1, q], ai[1, q]], -1)],
                   axis=1)
    pm = jnp.concatenate([pm, jnp.zeros_like(pm)], axis=1)
    return tmat.astype(BF16), uh.astype(BF16), hy.astype(BF16), pm


def _s5(proj, weights, tc):
    b, tt, _ = proj.shape
    q, g, s = S5_CHUNK, S5_GROUPS, S5_GROUP
    nch = tt // q
    tmat, uh, hy, pm = weights
    r = nch * b
    rows = b * tt
    y = pl.pallas_call(
        functools.partial(_s5_kernel, nb=b, nch=nch, nctx=tc // q),
        grid=(g,),
        in_specs=[_once((rows, 128), lambda gi: (0, OFF_U // 128 + gi // 8)),
                  pl.BlockSpec((1, q * s, q * s), lambda gi: (gi, 0, 0)),
                  pl.BlockSpec((1, q * s, 512), lambda gi: (gi, 0, 0)),
                  pl.BlockSpec((1, 256, q * s), lambda gi: (gi, 0, 0)),
                  pl.BlockSpec((1, 8, 128), lambda gi: (gi, 0, 0))],
        out_specs=pl.BlockSpec((rows, 128), lambda gi: (0, gi // 8)),
        out_shape=jax.ShapeDtypeStruct((rows, BRANCH_W), BF16),
        scratch_shapes=[pltpu.VMEM((rows, 128), F32), pltpu.VMEM((q, r // 2, 128), jnp.uint32),
                        pltpu.VMEM((8, r // 2, q * s), jnp.uint32)] + [pltpu.VMEM((r, 128), F32)] * 6,
        compiler_params=_params("arbitrary"),
        name="s5",
    )(proj.reshape(rows, N_PACK), tmat, uh, hy, pm)
    return y.reshape(b, tt, BRANCH_W)


def _ssd_kernel(*refs, backward, first, nctx_chunks, nchunks):
    if first:
        (xbc_ref, prev_ref, next_ref, z_ref, dt_ref, cw_ref, cb_ref, dtb_ref, alog_ref, dsk_ref,
         sel_ref, o_ref, st_s) = refs
        yin_ref = ng_ref = None
    else:
        (xbc_ref, prev_ref, next_ref, z_ref, dt_ref, cw_ref, cb_ref, dtb_ref, alog_ref, dsk_ref,
         sel_ref, yin_ref, ng_ref, o_ref, st_s) = refs
    qn = SSD_CHUNK
    step = pl.program_id(1)
    if backward:
        c = jnp.where(step < nctx_chunks, nctx_chunks - 1 - step, nchunks - 1 - (step - nctx_chunks))
    else:
        c = step

    @pl.when(step == 0)
    def _():
        st_s[...] = jnp.zeros_like(st_s)

    x = xbc_ref[0].astype(F32)
    has_prev = jnp.logical_and(c != 0, c != nctx_chunks)
    has_next = jnp.logical_and(c != nctx_chunks - 1, c != nchunks - 1)
    prow = jnp.where(has_prev, prev_ref[0, 7:8, :].astype(F32), 0.0)
    nrow = jnp.where(has_next, next_ref[0, 0:1, :].astype(F32), 0.0)
    rid = lax.broadcasted_iota(jnp.int32, (qn, 1), 0)
    xm = jnp.where(rid == 0, prow, pltpu.roll(x, 1, 0))
    xp = jnp.where(rid == qn - 1, nrow, pltpu.roll(x, qn - 1, 0))
    cw = cw_ref[...]
    conv = xm * cw[0:1] + x * cw[1:2] + xp * cw[2:3] + cb_ref[...]
    act = _silu(conv)
    xs = act[:, :BRANCH_W]
    gn = SSD_GROUPS * SSD_STATE
    bm = act[:, BRANCH_W:BRANCH_W + gn].astype(BF16)
    cm = act[:, BRANCH_W + gn:].astype(BF16)

    dt = _softplus(dt_ref[0].astype(F32) + dtb_ref[...])
    a = -jnp.exp(alog_ref[...])
    da = dt * a
    ri = lax.broadcasted_iota(jnp.int32, (qn, qn), 0)
    ci = lax.broadcasted_iota(jnp.int32, (qn, qn), 1)
    causal = (ci >= ri) if backward else (ci <= ri)
    ones_tri = jnp.where(causal, 1.0, 0.0).astype(BF16)
    d1 = da.astype(BF16)
    r1 = da - d1.astype(F32)
    d2 = r1.astype(BF16)
    d3 = (r1 - d2.astype(F32)).astype(BF16)
    cum = (jnp.dot(ones_tri, d1, preferred_element_type=F32)
           + jnp.dot(ones_tri, d2, preferred_element_type=F32)
           + jnp.dot(ones_tri, d3, preferred_element_type=F32))
    edge = cum[0:1] if backward else cum[qn - 1:qn]
    cum_t = cum.T
    dt_t = dt.T
    w_edge = dt * jnp.exp(edge - cum)
    e_in = jnp.exp(cum)

    def expand(v):
        v1 = v.astype(BF16)
        v2 = (v - v1.astype(F32)).astype(BF16)
        return (jnp.dot(v1, sel_ref[...], preferred_element_type=F32)
                + jnp.dot(v2, sel_ref[...], preferred_element_type=F32))

    xw = (xs * expand(w_edge)).astype(BF16)
    e_in_x = expand(e_in)
    xs_b = xs.astype(BF16)
    rpg = SSD_HEADS // SSD_GROUPS
    gw = rpg * SSD_HEAD_DIM
    ys = []
    for g in range(SSD_GROUPS):
        bg = bm[:, g * SSD_STATE:(g + 1) * SSD_STATE]
        cg = cm[:, g * SSD_STATE:(g + 1) * SSD_STATE]
        cb = lax.dot_general(cg, bg, (((1,), (1,)), ((), ())), preferred_element_type=F32)
        st_g = st_s[g * gw:(g + 1) * gw, :]
        y_off = lax.dot_general(cg, st_g.astype(BF16), (((1,), (1,)), ((), ())),
                                preferred_element_type=F32)
        yg = y_off * e_in_x[:, g * gw:(g + 1) * gw]
        parts = []
        for r in range(rpg):
            h = g * rpg + r
            dec = jnp.exp(jnp.where(causal, cum[:, h:h + 1] - cum_t[h:h + 1, :], -jnp.inf))
            wmat = (cb * dec * dt_t[h:h + 1, :]).astype(BF16)
            parts.append(jnp.dot(wmat, xs_b[:, h * SSD_HEAD_DIM:(h + 1) * SSD_HEAD_DIM],
                                 preferred_element_type=F32))
        ys.append(yg + jnp.concatenate(parts, axis=1))
        new = lax.dot_general(xw[:, g * gw:(g + 1) * gw], bg, (((0,), (0,)), ((), ())),
                              preferred_element_type=F32)
        for r in range(rpg):
            h = g * rpg + r
            rows = slice(g * gw + r * SSD_HEAD_DIM, g * gw + (r + 1) * SSD_HEAD_DIM)
            cd = jnp.exp(edge[:, h:h + 1])
            st_s[rows, :] = st_s[rows, :] * cd + new[r * SSD_HEAD_DIM:(r + 1) * SSD_HEAD_DIM, :]
    y = jnp.concatenate(ys, axis=1)
    if first:
        o_ref[0] = y + xs * dsk_ref[...]
    else:
        y = y + yin_ref[0]
        zz = z_ref[0].astype(F32)
        y = y * _silu(zz)
        y = y * lax.rsqrt(jnp.mean(y * y, axis=-1, keepdims=True) + RMS_EPS) * ng_ref[...]
        o_ref[0] = y.astype(o_ref.dtype)


def _ssd_pass(proj, conv_w, conv_b, dt_bias, a_log, d_skip, sel, yin, norm_g, tc, backward):
    b, tt, _ = proj.shape
    qn = SSD_CHUNK
    nchunks = tt // qn
    nctx = tc // qn
    first = yin is None
    last_blk8 = tt // 8 - 1

    def chunk_of(s):
        if backward:
            return jnp.where(s < nctx, nctx - 1 - s, nchunks - 1 - (s - nctx))
        return s

    xbc_blk = OFF_XBC // SSD_CONV_CH
    in_specs = [
        pl.BlockSpec((1, qn, SSD_CONV_CH), lambda bi, s: (bi, chunk_of(s), xbc_blk)),
        pl.BlockSpec((1, 8, SSD_CONV_CH),
                     lambda bi, s: (bi, jnp.maximum(chunk_of(s) * (qn // 8) - 1, 0), xbc_blk)),
        pl.BlockSpec((1, 8, SSD_CONV_CH),
                     lambda bi, s: (bi, jnp.minimum((chunk_of(s) + 1) * (qn // 8), last_blk8), xbc_blk)),
        pl.BlockSpec((1, qn, BRANCH_W), lambda bi, s: (bi, chunk_of(s), OFF_Z // BRANCH_W)),
        pl.BlockSpec((1, qn, 128), lambda bi, s: (bi, chunk_of(s), OFF_DT // 128)),
        pl.BlockSpec((8, SSD_CONV_CH), lambda bi, s: (0, 0)),
        pl.BlockSpec((1, SSD_CONV_CH), lambda bi, s: (0, 0)),
        pl.BlockSpec((1, 128), lambda bi, s: (0, 0)),
        pl.BlockSpec((1, 128), lambda bi, s: (0, 0)),
        pl.BlockSpec((1, BRANCH_W), lambda bi, s: (0, 0)),
        pl.BlockSpec((128, BRANCH_W), lambda bi, s: (0, 0)),
    ]
    args = [proj, proj, proj, proj, proj, conv_w, conv_b, dt_bias, a_log, d_skip, sel]
    if not first:
        in_specs += [pl.BlockSpec((1, qn, BRANCH_W), lambda bi, s: (bi, chunk_of(s), 0)),
                     pl.BlockSpec((1, BRANCH_W), lambda bi, s: (0, 0))]
        args += [yin, norm_g]
    return pl.pallas_call(
        functools.partial(_ssd_kernel, backward=backward, first=first, nctx_chunks=nctx, nchunks=nchunks),
        grid=(b, nchunks),
        in_specs=in_specs,
        out_specs=pl.BlockSpec((1, qn, BRANCH_W), lambda bi, s: (bi, chunk_of(s), 0)),
        out_shape=jax.ShapeDtypeStruct((b, tt, BRANCH_W), F32 if first else BF16),
        scratch_shapes=[pltpu.VMEM((SSD_HEADS * SSD_HEAD_DIM, SSD_STATE), F32)],
        compiler_params=_params("arbitrary", "arbitrary"),
        name="ssd_bwd" if backward else "ssd_fwd",
    )(*args)


def _ssd(proj, conv_w, conv_b, dt_bias, a_log, d_skip, norm_g, tc):
    def lanes(v):
        return jnp.pad(v.astype(F32), (0, 128 - SSD_HEADS)).reshape(1, 128)

    cw = jnp.pad(conv_w.astype(F32), ((0, 5), (0, 0)))
    cb = conv_b.astype(F32).reshape(1, -1)
    dsk = jnp.repeat(d_skip.astype(F32), SSD_HEAD_DIM).reshape(1, -1)
    sel = (jnp.arange(128)[:, None] == (jnp.arange(BRANCH_W)[None, :] // SSD_HEAD_DIM)).astype(BF16)
    y1 = _ssd_pass(proj, cw, cb, lanes(dt_bias[0]), lanes(a_log[0]), dsk, sel, None, None, tc, False)
    return _ssd_pass(proj, cw, cb, lanes(dt_bias[1]), lanes(a_log[1]), jnp.zeros_like(dsk), sel,
                     y1, norm_g.astype(F32).reshape(1, -1), tc, True)


def _merge_kernel(ya_ref, ga_ref, yb_ref, gb_ref, yc_ref, yd_ref, gd_ref, gm_ref, bm_ref,
                  wb_ref, wg_ref, bg_ref, o_ref):
    def f(ref):
        return ref[0].astype(F32)

    g = _gelu_tanh(f(yb_ref))
    glu = g * _sigmoid(jnp.dot(g.astype(BF16), wg_ref[...], preferred_element_type=F32) + bg_ref[...])
    branches = (f(ya_ref) * _silu(f(ga_ref)), glu * _silu(f(gb_ref)), f(yc_ref), f(yd_ref) * _silu(f(gd_ref)))
    acc = None
    for n, br in enumerate(branches):
        gate = _sigmoid(gm_ref[0, :, n * D_MODEL:(n + 1) * D_MODEL].astype(F32) + bm_ref[n:n + 1, :])
        term = gate * jnp.dot(br.astype(BF16), wb_ref[n], preferred_element_type=F32)
        acc = term if acc is None else acc + term
    o_ref[0] = acc.astype(o_ref.dtype)


def _merge(proj, ya, yb, yc, yd, b_merge, w_branch, w_glu, b_glu, skip):
    b, tt, _ = proj.shape
    tm = Q_BLOCK
    bw = BRANCH_W

    def pspec(off):
        return pl.BlockSpec((1, tm, bw), lambda bi, i: (bi, i + skip, off // bw))

    yspec = pl.BlockSpec((1, tm, bw), lambda bi, i: (bi, i + skip, 0))
    once = dict(pipeline_mode=pl.Buffered(1))
    return pl.pallas_call(
        _merge_kernel,
        grid=(b, tt // tm - skip),
        in_specs=[yspec, pspec(OFF_GA), yspec, pspec(OFF_GB), yspec, yspec, pspec(OFF_GD),
                  pl.BlockSpec((1, tm, N_BRANCH * D_MODEL), lambda bi, i: (bi, i + skip, 0)),
                  pl.BlockSpec((N_BRANCH, D_MODEL), lambda bi, i: (0, 0)),
                  pl.BlockSpec((N_BRANCH, bw, D_MODEL), lambda bi, i: (0, 0, 0), **once),
                  pl.BlockSpec((bw, bw), lambda bi, i: (0, 0), **once),
                  pl.BlockSpec((1, bw), lambda bi, i: (0, 0))],
        out_specs=pl.BlockSpec((1, tm, D_MODEL), lambda bi, i: (bi, i + skip, 0)),
        out_shape=jax.ShapeDtypeStruct((b, tt, D_MODEL), BF16),
        compiler_params=_params("arbitrary", "arbitrary"),
        name="merge",
    )(ya, proj, yb, proj, yc, yd, proj, proj, b_merge, w_branch, w_glu, b_glu.reshape(1, -1))


def _out_kernel(m_ref, x_ref, mod_ref, w_ref, lg_ref, lb_ref, o_ref, *, tc, tm, row0):
    i = pl.program_id(1)
    out = jnp.dot(m_ref[0], w_ref[...], preferred_element_type=F32)
    row = row0 + i * tm + lax.broadcasted_iota(jnp.int32, (tm, 1), 0)
    m = mod_ref[0]
    gate = jnp.where(row < tc, m[4:5], m[5:6])
    v = DEEPNORM_ALPHA * x_ref[0] + gate * out
    mu = jnp.mean(v, axis=-1, keepdims=True)
    vc = v - mu
    var = jnp.mean(vc * vc, axis=-1, keepdims=True)
    o_ref[0] = vc * lax.rsqrt(var + LN_EPS) * lg_ref[...] + lb_ref[...]


def _out(merged, xa, mod, w_out, ln_g, ln_b, tc, latent_only):
    b, tt, d = xa.shape
    tm = Q_BLOCK
    skip = tc // tm if latent_only else 0
    nblk = tt // tm - skip
    spec = pl.BlockSpec((1, tm, d), lambda bi, i: (bi, i + skip, 0))
    return pl.pallas_call(
        functools.partial(_out_kernel, tc=tc, tm=tm, row0=skip * tm),
        grid=(b, nblk),
        in_specs=[spec, spec,
                  pl.BlockSpec((1, 8, d), lambda bi, i: (bi, 0, 0)),
                  pl.BlockSpec((d, d), lambda bi, i: (0, 0)),
                  pl.BlockSpec((1, d), lambda bi, i: (0, 0)),
                  pl.BlockSpec((1, d), lambda bi, i: (0, 0))],
        out_specs=pl.BlockSpec((1, tm, d), lambda bi, i: (bi, i, 0)),
        out_shape=jax.ShapeDtypeStruct((b, nblk * tm, d), F32),
        compiler_params=_params("arbitrary", "arbitrary"),
        name="out_proj",
    )(merged, xa, mod, w_out, ln_g.reshape(1, -1), ln_b.reshape(1, -1))


def _half_split(w, heads):
    k = w.shape[0]
    return w.reshape(k, heads, ROPE_DIM // 2, 2).transpose(0, 1, 3, 2).reshape(k, heads * ROPE_DIM)


def _pack_w_in(w):
    sp = np.cumsum([0, MLA_Q_LORA, MLA_KV_LORA, MLA_ROPE, BRANCH_W, BRANCH_W, BRANCH_W, BRANCH_W,
                    SSD_CONV_CH, SSD_HEADS, BRANCH_W, BRANCH_W, BRANCH_W, BRANCH_W, N_BRANCH * D_MODEL])
    seg = [w[:, sp[k]:sp[k + 1]] for k in range(14)]
    cq, ckv, kr, ga, u, gb, z, xbc, dt, qd, kd, vd, gd, gm = seg
    kr_hs = _half_split(kr, 1)
    kr_sw = jnp.concatenate([kr_hs[:, 32:], kr_hs[:, :32]], axis=1)
    pad = jnp.zeros((w.shape[0], 128 - SSD_HEADS), w.dtype)
    packed = jnp.concatenate([gm, xbc, ga, u, gb, z, _half_split(qd, 2 * DIFF_HEADS),
                              _half_split(kd, 2 * DIFF_HEADS), vd, gd, cq, ckv, kr_hs, kr_sw, dt, pad],
                             axis=1)
    return packed.astype(BF16)


def _pack_mla(w_uq, w_ukv):
    k = w_uq.shape[0]
    wq = w_uq.reshape(k, MLA_HEADS, MLA_NOPE + MLA_ROPE)
    rope = wq[:, :, MLA_NOPE:].reshape(k, MLA_HEADS, ROPE_DIM // 2, 2)
    ev, od = rope[..., 0], rope[..., 1]
    wq = jnp.concatenate([wq[:, :, :MLA_NOPE], ev, od, od, ev], axis=-1)
    wkv = w_ukv.reshape(w_ukv.shape[0], MLA_HEADS, MLA_NOPE + MLA_V)
    return (wq.transpose(1, 0, 2).astype(BF16),
            wkv[:, :, :MLA_NOPE].transpose(1, 0, 2).astype(BF16),
            wkv[:, :, MLA_NOPE:].transpose(1, 0, 2).astype(BF16))


def _rope_tables(tl, tc):
    rows = tl // GRID_W
    row_id = jnp.repeat(jnp.arange(rows, dtype=F32), GRID_W)
    col_id = jnp.tile(jnp.arange(GRID_W, dtype=F32), rows)
    quarter = ROPE_DIM // 4
    inv_freq = ROPE_BASE ** (-jnp.arange(quarter, dtype=F32) / quarter)
    ang = jnp.concatenate([row_id[:, None] * inv_freq, col_id[:, None] * inv_freq], axis=-1)
    cos = jnp.concatenate([jnp.ones((tc, ROPE_DIM // 2), F32), jnp.cos(ang)], axis=0)
    sin = jnp.concatenate([jnp.zeros((tc, ROPE_DIM // 2), F32), jnp.sin(ang)], axis=0)
    zero = jnp.zeros_like(cos)
    mla = (jnp.concatenate([cos, cos, zero, zero], axis=1), jnp.concatenate([zero, zero, -sin, sin], axis=1))
    diff = (jnp.concatenate([cos, cos, cos, cos], axis=1), jnp.concatenate([-sin, sin, -sin, sin], axis=1))
    return mla, diff


def _layer(xa, cs, p, tables, layer_idx, tc, last):
    b = xa.shape[0]
    mla_tab, diff_tab = tables
    mod = _ada(cs, p['w_ada_stacked'], p['b_ada'], layer_idx)
    d = D_MODEL
    shift, scale, gate = mod[:, :d], mod[:, d:2 * d], mod[:, 2 * d:]
    zeros = jnp.zeros((b, d), F32)
    modb = jnp.stack([jnp.broadcast_to(scale[0], (b, d)), jnp.broadcast_to(shift[0], (b, d)),
                      scale[1:1 + b], shift[1:1 + b],
                      jnp.broadcast_to(gate[0], (b, d)), gate[1:1 + b], zeros, zeros], axis=1)
    proj = _inproj(xa, modb, _pack_w_in(p['w_in']), tc)
    wq, wk, wv = _pack_mla(p['mla_w_uq'], p['mla_w_ukv'])
    ya = _mla(proj, wq, wk, wv, p['mla_q_norm'], p['mla_kv_norm'], mla_tab[0], mla_tab[1], tc, last)
    lam_init = 0.8 - 0.6 * math.exp(-0.3 * layer_idx)
    yd = _diff(proj, p['diff_lambda_q'], p['diff_lambda_k'], p['diff_norm'], diff_tab[0], diff_tab[1],
               lam_init, tc, last)
    yb = _s5(proj, _s5_weights(p['s5_lambda_re'], p['s5_lambda_im'], p['s5_log_dt'], p['s5_b_re'],
                               p['s5_b_im'], p['s5_c_re'], p['s5_c_im'], p['s5_d']), tc)
    yc = _ssd(proj, p['ssd_conv_w'], p['ssd_conv_b'], p['ssd_dt_bias'], p['ssd_a_log'], p['ssd_d'],
              p['ssd_norm'], tc)
    merged = _merge(proj, ya, yb, yc, yd, p['b_merge'], p['w_branch'].astype(BF16),
                    p['s5_w_glu'].astype(BF16), p['s5_b_glu'], tc // Q_BLOCK if last else 0)
    return _out(merged, xa, modb, p['w_out'].astype(BF16), p['ln_g'], p['ln_b'], tc, last)


def kernel(x, c, ctx, c_ctx, w_ada, b_ada, w_in, mla_q_norm, mla_w_uq, mla_kv_norm, mla_w_ukv,
           s5_lambda_re, s5_lambda_im, s5_log_dt, s5_b_re, s5_b_im, s5_c_re, s5_c_im, s5_d,
           s5_w_glu, s5_b_glu, ssd_conv_w, ssd_conv_b, ssd_dt_bias, ssd_a_log, ssd_d, ssd_norm,
           diff_lambda_q, diff_lambda_k, diff_norm, b_merge, w_branch, w_out, ln_g, ln_b):
    b, tl, d = x.shape
    tc = ctx.shape[1]
    stacked = dict(
        w_ada=w_ada, b_ada=b_ada, w_in=w_in, mla_q_norm=mla_q_norm, mla_w_uq=mla_w_uq,
        mla_kv_norm=mla_kv_norm, mla_w_ukv=mla_w_ukv, s5_lambda_re=s5_lambda_re, s5_lambda_im=s5_lambda_im,
        s5_log_dt=s5_log_dt, s5_b_re=s5_b_re, s5_b_im=s5_b_im, s5_c_re=s5_c_re, s5_c_im=s5_c_im, s5_d=s5_d,
        s5_w_glu=s5_w_glu, s5_b_glu=s5_b_glu, ssd_conv_w=ssd_conv_w, ssd_conv_b=ssd_conv_b,
        ssd_dt_bias=ssd_dt_bias, ssd_a_log=ssd_a_log, ssd_d=ssd_d, ssd_norm=ssd_norm,
        diff_lambda_q=diff_lambda_q, diff_lambda_k=diff_lambda_k, diff_norm=diff_norm, b_merge=b_merge,
        w_branch=w_branch, w_out=w_out, ln_g=ln_g, ln_b=ln_b)
    tables = _rope_tables(tl, tc)
    cs = jnp.concatenate([c_ctx[None], c, jnp.zeros((8 - 1 - b, d), c.dtype)], axis=0)
    xa = jnp.concatenate([ctx, x], axis=1)
    depth = w_in.shape[0]
    for i in range(depth):
        p = {k: v[i] for k, v in stacked.items() if k != 'w_ada'}
        p['w_ada_stacked'] = w_ada
        xa = _layer(xa, cs, p, tables, i, tc, i == depth - 1)
    return xa
```

```python
import functools
import math

import numpy as np
import jax
import jax.numpy as jnp
from jax import lax
from jax.experimental import pallas as pl
from jax.experimental.pallas import tpu as pltpu

F32 = jnp.float32
BF16 = jnp.bfloat16

D_MODEL = 2048
DEPTH = 2
GRID_W = 64
N_BRANCH = 4
BRANCH_W = 1024
ROPE_DIM = 64
ROPE_BASE = 10000.0

MLA_HEADS = 8
MLA_NOPE = 128
MLA_ROPE = ROPE_DIM
MLA_V = 128
MLA_Q_LORA = 512
MLA_KV_LORA = 256

S5_GROUP = 16
S5_GROUPS = BRANCH_W // S5_GROUP
S5_STATE = 64
S5_CHUNK = 16

SSD_HEAD_DIM = 64
SSD_HEADS = BRANCH_W // SSD_HEAD_DIM
SSD_GROUPS = 4
SSD_STATE = 128
SSD_CHUNK = 128
SSD_CONV_CH = BRANCH_W + 2 * SSD_GROUPS * SSD_STATE

DIFF_HEAD_DIM = ROPE_DIM
DIFF_HEADS = BRANCH_W // (2 * DIFF_HEAD_DIM)

LN_EPS = 1e-5
RMS_EPS = 1e-6
DEEPNORM_ALPHA = (2 * DEPTH) ** 0.25

V7X_VMEM_LIMIT_BYTES = 56 * 1024 * 1024
Q_BLOCK = 256
ATTN_HEADS_PER_STEP = 4
MLA_HEADS_PER_STEP = 8
LOG2E = math.log2(math.e)

OFF_GM = 0
OFF_XBC = 8192
OFF_GA = 10240
OFF_U = 11264
OFF_GB = 12288
OFF_Z = 13312
OFF_QD = 14336
OFF_KD = 15360
OFF_VD = 16384
OFF_GD = 17408
OFF_CQ = 18432
OFF_CKV = 18944
OFF_KR = 19200
OFF_DT = 19328
N_PACK = 19456
IN_TILE_N = 1024


def _sigmoid(x):
    return 0.5 * (1.0 + jnp.tanh(0.5 * x))


def _silu(x):
    return x * _sigmoid(x)


def _gelu_tanh(x):
    return 0.5 * x * (1.0 + jnp.tanh(math.sqrt(2.0 / math.pi) * (x + 0.044715 * (x * x * x))))


def _softplus(x):
    return jnp.maximum(x, 0.0) + jnp.log(1.0 + jnp.exp(-jnp.abs(x)))


def _params(*sem):
    return pltpu.CompilerParams(dimension_semantics=sem, vmem_limit_bytes=V7X_VMEM_LIMIT_BYTES)


def _ada_kernel(c_ref, w_ref, b_ref, o_ref):
    s = _silu(c_ref[...])
    o_ref[...] = jnp.dot(s.astype(BF16), w_ref[0].astype(BF16),
                         preferred_element_type=F32) + b_ref[...]


def _ada(cs, w_ada, b_ada, layer):
    n = w_ada.shape[2]
    tn = 512
    return pl.pallas_call(
        _ada_kernel,
        grid=(n // tn,),
        in_specs=[pl.BlockSpec((8, D_MODEL), lambda j: (0, 0)),
                  pl.BlockSpec((1, D_MODEL, tn), lambda j: (layer, 0, j)),
                  pl.BlockSpec((1, tn), lambda j: (0, j))],
        out_specs=pl.BlockSpec((8, tn), lambda j: (0, j)),
        out_shape=jax.ShapeDtypeStruct((8, n), F32),
        compiler_params=_params("arbitrary"),
        name="ada",
    )(cs, w_ada, b_ada.reshape(1, n))


def _inproj_kernel(x_ref, mod_ref, w_ref, o_ref, h_ref, *, tc, tm):
    i = pl.program_id(1)

    @pl.when(pl.program_id(2) == 0)
    def _():
        x = x_ref[0]
        mu = jnp.mean(x, axis=-1, keepdims=True)
        xc = x - mu
        var = jnp.mean(xc * xc, axis=-1, keepdims=True)
        xn = xc * lax.rsqrt(var + LN_EPS)
        row = i * tm + lax.broadcasted_iota(jnp.int32, (tm, 1), 0)
        is_ctx = row < tc
        m = mod_ref[0]
        scale = jnp.where(is_ctx, m[0:1], m[2:3])
        shift = jnp.where(is_ctx, m[1:2], m[3:4])
        h_ref[...] = (xn * (1.0 + scale) + shift).astype(BF16)

    o_ref[0] = jnp.dot(h_ref[...], w_ref[...], preferred_element_type=F32).astype(o_ref.dtype)


def _inproj(xa, mod, w_pack, tc):
    b, tt, d = xa.shape
    tm = tt // 4
    tn = IN_TILE_N
    return pl.pallas_call(
        functools.partial(_inproj_kernel, tc=tc, tm=tm),
        grid=(b, tt // tm, N_PACK // tn),
        in_specs=[pl.BlockSpec((1, tm, d), lambda bi, i, j: (bi, i, 0)),
                  pl.BlockSpec((1, 8, d), lambda bi, i, j: (bi, 0, 0)),
                  pl.BlockSpec((d, tn), lambda bi, i, j: (0, j))],
        out_specs=pl.BlockSpec((1, tm, tn), lambda bi, i, j: (bi, i, j)),
        out_shape=jax.ShapeDtypeStruct((b, tt, N_PACK), BF16),
        scratch_shapes=[pltpu.VMEM((tm, d), BF16)],
        compiler_params=_params("arbitrary", "arbitrary", "arbitrary"),
        name="inproj",
    )(xa, mod, w_pack)


def _mla_kernel(cq_ref, ckv_ref, kr_ref, wq_ref, wk_ref, wv_ref, qn_ref, kvn_ref,
                ct_ref, st_ref, ctq_ref, stq_ref, o_ref, k_s, v_s, *, tc, tt, scale, skip_ctx):
    hps = MLA_HEADS_PER_STEP

    @pl.when(pl.program_id(2) == 0)
    def _():
        ckv = ckv_ref[0].astype(F32)
        r = lax.rsqrt(jnp.mean(ckv * ckv, axis=-1, keepdims=True) + RMS_EPS)
        ckvn = (ckv * r * kvn_ref[...]).astype(BF16)
        kr = kr_ref[0].astype(F32)
        kroped = (kr * ct_ref[...] + pltpu.roll(kr * st_ref[...], 64, 1)).astype(BF16)
        for hh in range(hps):
            k_s[hh, :, 0:MLA_NOPE] = jnp.dot(ckvn, wk_ref[hh], preferred_element_type=F32).astype(BF16)
            k_s[hh, :, MLA_NOPE:] = kroped
            v_s[hh] = jnp.dot(ckvn, wv_ref[hh], preferred_element_type=F32).astype(BF16)

    cq = cq_ref[0].astype(F32)
    r = lax.rsqrt(jnp.mean(cq * cq, axis=-1, keepdims=True) + RMS_EPS)
    cqn = (cq * r * qn_ref[...]).astype(BF16)
    qfs = []
    for hh in range(hps):
        q = jnp.dot(cqn, wq_ref[hh], preferred_element_type=F32)
        qh = q[:, MLA_NOPE:]
        qr = qh * ctq_ref[...] + pltpu.roll(qh * stq_ref[...], 64, 1)
        qfs.append((jnp.concatenate([q[:, :MLA_NOPE], qr], axis=1) * (scale * LOG2E)).astype(BF16))

    def attend(nk):
        outs = []
        for hh in range(hps):
            s = lax.dot_general(qfs[hh], k_s[hh, 0:nk, :], (((1,), (1,)), ((), ())),
                                preferred_element_type=F32)
            p = jnp.exp2(s - jnp.max(s, axis=1, keepdims=True))
            l = jnp.sum(p, axis=1, keepdims=True)
            o = jnp.dot(p.astype(BF16), v_s[hh, 0:nk, :], preferred_element_type=F32)
            outs.append(o * (1.0 / l))
        o_ref[0] = jnp.concatenate(outs, axis=1).astype(o_ref.dtype)

    _attend_blocks(attend, tc, tt, skip_ctx)


def _attend_blocks(attend, tc, tt, skip_ctx):
    if skip_ctx:
        attend(tt)
        return

    @pl.when(pl.program_id(2) == 0)
    def _():
        attend(tc)

    @pl.when(pl.program_id(2) > 0)
    def _():
        attend(tt)


def _once(shape, index_map):
    return pl.BlockSpec(shape, index_map, pipeline_mode=pl.Buffered(1))


def _mla(proj, wq, wk, wv, q_norm, kv_norm, ct, st, tc, skip_ctx):
    b, tt, _ = proj.shape
    tq = Q_BLOCK
    assert tc == tq
    scale = (MLA_NOPE + MLA_ROPE) ** -0.5
    hps = MLA_HEADS_PER_STEP
    sk = int(skip_ctx)
    return pl.pallas_call(
        functools.partial(_mla_kernel, tc=tc, tt=tt, scale=scale, skip_ctx=skip_ctx),
        grid=(b, MLA_HEADS // hps, tt // tq - sk),
        in_specs=[
            pl.BlockSpec((1, tq, MLA_Q_LORA), lambda bi, h, i: (bi, i + sk, OFF_CQ // MLA_Q_LORA)),
            _once((1, tt, MLA_KV_LORA), lambda bi, h, i: (bi, 0, OFF_CKV // MLA_KV_LORA)),
            _once((1, tt, 128), lambda bi, h, i: (bi, 0, OFF_KR // 128)),
            pl.BlockSpec((hps, MLA_Q_LORA, 256), lambda bi, h, i: (h, 0, 0)),
            pl.BlockSpec((hps, MLA_KV_LORA, MLA_NOPE), lambda bi, h, i: (h, 0, 0)),
            pl.BlockSpec((hps, MLA_KV_LORA, MLA_V), lambda bi, h, i: (h, 0, 0)),
            pl.BlockSpec((1, MLA_Q_LORA), lambda bi, h, i: (0, 0)),
            pl.BlockSpec((1, MLA_KV_LORA), lambda bi, h, i: (0, 0)),
            _once((tt, 128), lambda bi, h, i: (0, 0)),
            _once((tt, 128), lambda bi, h, i: (0, 0)),
            pl.BlockSpec((tq, 128), lambda bi, h, i: (i + sk, 0)),
            pl.BlockSpec((tq, 128), lambda bi, h, i: (i + sk, 0)),
        ],
        out_specs=pl.BlockSpec((1, tq, hps * MLA_V), lambda bi, h, i: (bi, i + sk, h)),
        out_shape=jax.ShapeDtypeStruct((b, tt, BRANCH_W), BF16),
        scratch_shapes=[pltpu.VMEM((hps, tt, 256), BF16), pltpu.VMEM((hps, tt, MLA_V), BF16)],
        compiler_params=_params("arbitrary", "arbitrary", "arbitrary"),
        name="mla_attn",
    )(proj, proj, proj, wq, wk, wv, q_norm.reshape(1, -1), kv_norm.reshape(1, -1), ct, st, ct, st)


def _diff_kernel(q_ref, k_ref, v_ref, lq_ref, lk_ref, g_ref, ct_ref, st_ref, ctq_ref, stq_ref,
                 o_ref, k_s, *, tc, tt, lam_init, skip_ctx):
    hps = ATTN_HEADS_PER_STEP
    lane = lax.broadcasted_iota(jnp.int32, (1, 128), 1)
    first_half = (lane & 32) == 0

    def rope(x, c, s):
        xs = jnp.where(first_half, pltpu.roll(x, 96, 1), pltpu.roll(x, 32, 1))
        return x * c + xs * s

    @pl.when(pl.program_id(2) == 0)
    def _():
        for hh in range(hps):
            k_s[hh] = rope(k_ref[0, :, hh * 128:(hh + 1) * 128].astype(F32), ct_ref[...],
                           st_ref[...]).astype(BF16)

    qs = []
    for hh in range(hps):
        q = rope(q_ref[0, :, hh * 128:(hh + 1) * 128].astype(F32), ctq_ref[...], stq_ref[...]) * (
            DIFF_HEAD_DIM ** -0.5 * LOG2E)
        qs.append((jnp.where(lane < 64, q, 0.0).astype(BF16), jnp.where(lane >= 64, q, 0.0).astype(BF16)))
    lqk = lq_ref[...] * lk_ref[...]
    lam = (jnp.exp(jnp.sum(lqk[0:1], axis=1, keepdims=True))
           - jnp.exp(jnp.sum(lqk[1:2], axis=1, keepdims=True)) + lam_init)

    def attend(nk):
        nt = (((1,), (1,)), ((), ()))
        outs = []
        for hh in range(hps):
            kk = k_s[hh, 0:nk, :]
            s1 = lax.dot_general(qs[hh][0], kk, nt, preferred_element_type=F32)
            s2 = lax.dot_general(qs[hh][1], kk, nt, preferred_element_type=F32)
            e1 = jnp.exp2(s1 - jnp.max(s1, axis=1, keepdims=True))
            e2 = jnp.exp2(s2 - jnp.max(s2, axis=1, keepdims=True))
            l1 = jnp.sum(e1, axis=1, keepdims=True)
            l2 = jnp.sum(e2, axis=1, keepdims=True)
            w = e1 - e2 * (lam * l1 / l2)
            o = jnp.dot(w.astype(BF16), v_ref[0, 0:nk, hh * 128:(hh + 1) * 128],
                        preferred_element_type=F32) * (1.0 / l1)
            y = o * lax.rsqrt(jnp.mean(o * o, axis=-1, keepdims=True) + RMS_EPS)
            outs.append(y * g_ref[...] * (1.0 - lam_init))
        o_ref[0] = jnp.concatenate(outs, axis=1).astype(o_ref.dtype)

    _attend_blocks(attend, tc, tt, skip_ctx)


def _diff(proj, lam_q, lam_k, norm_g, ct, st, lam_init, tc, skip_ctx):
    b, tt, _ = proj.shape
    tq = Q_BLOCK
    assert tc == tq
    hps = ATTN_HEADS_PER_STEP
    hw = 128 * hps
    sk = int(skip_ctx)
    return pl.pallas_call(
        functools.partial(_diff_kernel, tc=tc, tt=tt, lam_init=lam_init, skip_ctx=skip_ctx),
        grid=(b, DIFF_HEADS // hps, tt // tq - sk),
        in_specs=[
            pl.BlockSpec((1, tq, hw), lambda bi, h, i: (bi, i + sk, OFF_QD // hw + h)),
            pl.BlockSpec((1, tt, hw), lambda bi, h, i: (bi, 0, OFF_KD // hw + h)),
            pl.BlockSpec((1, tt, hw), lambda bi, h, i: (bi, 0, OFF_VD // hw + h)),
            pl.BlockSpec((2, DIFF_HEAD_DIM), lambda bi, h, i: (0, 0)),
            pl.BlockSpec((2, DIFF_HEAD_DIM), lambda bi, h, i: (0, 0)),
            pl.BlockSpec((1, 128), lambda bi, h, i: (0, 0)),
            _once((tt, 128), lambda bi, h, i: (0, 0)),
            _once((tt, 128), lambda bi, h, i: (0, 0)),
            pl.BlockSpec((tq, 128), lambda bi, h, i: (i + sk, 0)),
            pl.BlockSpec((tq, 128), lambda bi, h, i: (i + sk, 0)),
        ],
        out_specs=pl.BlockSpec((1, tq, hw), lambda bi, h, i: (bi, i + sk, h)),
        out_shape=jax.ShapeDtypeStruct((b, tt, BRANCH_W), BF16),
        scratch_shapes=[pltpu.VMEM((hps, tt, 128), BF16)],
        compiler_params=_params("arbitrary", "arbitrary", "arbitrary"),
        name="diff_attn",
    )(proj, proj, proj, lam_q, lam_k, norm_g.reshape(1, -1), ct, st, ct, st)


def _s5_kernel(x_ref, t_ref, uh_ref, hy_ref, p_ref, o_ref, tok_s, xf_s, ys_s, vf_s, vfs_s, vb_s, vbs_s,
               hf_s, hb_s, *, nb, nch, nctx):
    q = S5_CHUNK
    r = nb * nch
    g8 = pl.program_id(0) % 8
    lane_blk = lax.broadcasted_iota(jnp.int32, (1, 128), 1) // S5_GROUP
    u32 = jnp.uint32

    @pl.when(g8 == 0)
    def _():
        tok_s[...] = x_ref[...].astype(F32)
        for i in range(q):
            xf_s[i] = pltpu.bitcast(tok_s[pl.ds(i, r, stride=q), :].astype(BF16), u32)

    cols = []
    for k in range(q // 8):
        acc = jnp.zeros((r // 2, 128), u32)
        for i8 in range(8):
            rot = pltpu.roll(xf_s[k * 8 + i8], lax.rem((i8 - g8) * S5_GROUP + 128, 128), 1)
            acc = jnp.where(lane_blk == i8, rot, acc)
        cols.append(acc)
    x = pltpu.bitcast(jnp.concatenate(cols, axis=1), BF16)

    yloc = jnp.dot(x, t_ref[0], preferred_element_type=F32)
    v = jnp.dot(x, uh_ref[0], preferred_element_type=F32)
    vf_s[...] = v[:, 0:128]
    vfs_s[...] = v[:, 128:256]
    vb_s[...] = v[:, 256:384]
    vbs_s[...] = v[:, 384:512]
    pm = p_ref[0]
    zero = jnp.zeros((nb, 128), F32)

    def scan(order, v_ref, vs_ref, h_ref, p1, p2):
        h, hs = zero, zero
        for c in order:
            rows = pl.ds(c, nb, stride=nch)
            h_ref[rows, :] = h
            h, hs = h * p1 + hs * p2 + v_ref[rows, :], hs * p1 - h * p2 + vs_ref[rows, :]

    scan(range(nch), vf_s, vfs_s, hf_s, pm[0:1], pm[1:2])
    scan(list(range(nctx - 1, -1, -1)) + list(range(nch - 1, nctx - 1, -1)), vb_s, vbs_s, hb_s,
         pm[2:3], pm[3:4])
    hcat = jnp.concatenate([hf_s[...], hb_s[...]], axis=1).astype(BF16)
    y = yloc + jnp.dot(hcat, hy_ref[0], preferred_element_type=F32)
    ys_s[g8] = pltpu.bitcast(y.astype(BF16), u32)

    @pl.when(g8 == 7)
    def _():
        for k in range(q // 8):
            for i8 in range(8):
                out = jnp.zeros((r // 2, 128), u32)
                for gg in range(8):
                    yk = ys_s[gg, :, k * 128:(k + 1) * 128]
                    rot = yk if gg == i8 else pltpu.roll(yk, (gg - i8) * S5_GROUP % 128, 1)
                    out = jnp.where(lane_blk == gg, rot, out)
                tok_s[pl.ds(k * 8 + i8, r, stride=q), :] = pltpu.bitcast(out, BF16).astype(F32)
        o_ref[...] = tok_s[...].astype(o_ref.dtype)


def _s5_weights(lam_re, lam_im, log_dt, b_re, b_im, c_re, c_im, d):
    q = S5_CHUNK
    g, p, s = S5_GROUPS, S5_STATE, S5_GROUP
    hi = lax.Precision.HIGHEST
    lr, li = lam_re.astype(F32), lam_im.astype(F32)
    dt = jnp.exp(log_dt.astype(F32))[..., None]
    tau = jnp.arange(q + 1, dtype=F32)[None, :, None, None]
    mag = jnp.exp((lr * dt)[:, None] * tau)
    ang = (li * dt)[:, None] * tau
    ar, ai = mag * jnp.cos(ang), mag * jnp.sin(ang)
    den = lr * lr + li * li
    fr = ((ar[:, 1] - 1.0) * lr + ai[:, 1] * li) / den
    fi = (ai[:, 1] * lr - (ar[:, 1] - 1.0) * li) / den
    br, bi = b_re.astype(F32), b_im.astype(F32)
    bbr = fr[..., None] * br - fi[..., None] * bi
    bbi = fr[..., None] * bi + fi[..., None] * br
    cr, ci = c_re.astype(F32)[:, None], c_im.astype(F32)[:, None]
    car = cr * ar[:, :, :, None, :] - ci * ai[:, :, :, None, :]
    cai = cr * ai[:, :, :, None, :] + ci * ar[:, :, :, None, :]
    car_x = jnp.repeat(jnp.swapaxes(car[:, :q], 3, 4), s, axis=-1)
    cai_x = jnp.repeat(jnp.swapaxes(cai[:, :q], 3, 4), s, axis=-1)
    bbr_x = jnp.tile(bbr, (1, 1, 1, s))[:, None]
    bbi_x = jnp.tile(bbi, (1, 1, 1, s))[:, None]
    kern = jnp.sum(car_x * bbr_x - cai_x * bbi_x, axis=3).reshape(2, q, g, s, s)
    ii = np.arange(q)
    lag = ii[None, :, None] - ii[:, None, None]
    sel_f = jnp.asarray(lag == ii[None, None, :], F32)
    sel_b = jnp.asarray(-lag == ii[None, None, :], F32)
    tmat = (jnp.einsum('jit,tgos->gjsio', sel_f, kern[0], precision=hi)
            + jnp.einsum('jit,tgos->gjsio', sel_b, kern[1], precision=hi))
    eye = jnp.eye(q, dtype=F32)[:, None, :, None] * jnp.eye(s, dtype=F32)[None, :, None, :]
    tmat = tmat + eye[None] * d.astype(F32).reshape(g, 1, s, 1, 1)
    tmat = tmat.reshape(g, q * s, q * s)

    def chunk_in(are, aim, bre, bim):
        re = jnp.einsum('jgp,gpi->gjip', are, bre) - jnp.einsum('jgp,gpi->gjip', aim, bim)
        im = jnp.einsum('jgp,gpi->gjip', are, bim) + jnp.einsum('jgp,gpi->gjip', aim, bre)
        return jnp.concatenate([re, im, im, re], axis=-1)

    uh = jnp.concatenate([chunk_in(ar[0, :q][::-1], ai[0, :q][::-1], bbr[0], bbi[0]),
                          chunk_in(ar[1, :q], ai[1, :q], bbr[1], bbi[1])], axis=-1).reshape(g, q * s, 8 * p)

    def state_out(re, im):
        return jnp.concatenate([jnp.transpose(re, (1, 3, 0, 2)), -jnp.transpose(im, (1, 3, 0, 2))], axis=1)

    hy = jnp.concatenate([state_out(car[0, 1:], cai[0, 1:]),
                          state_out(car[1, 1:][::-1], cai[1, 1:][::-1])], axis=1).reshape(g, 4 * p, q * s)
    pm = jnp.stack([jnp.concatenate([ar[0, q], ar[0, q]], -1), jnp.concatenate([-ai[0, q], ai[0, q]], -1),
                    jnp.concatenate([ar[1, q], ar[1, q]], -1), jnp.concatenate([-ai[1, q], ai[1, q]], -1)],
                   axis=1)
    pm = jnp.concatenate([pm, jnp.zeros_like(pm)], axis=1)
    return tmat.astype(BF16), uh.astype(BF16), hy.astype(BF16), pm


def _s5(proj, weights, tc):
    b, tt, _ = proj.shape
    q, g, s = S5_CHUNK, S5_GROUPS, S5_GROUP
    nch = tt // q
    tmat, uh, hy, pm = weights
    r = nch * b
    rows = b * tt
    y = pl.pallas_call(
        functools.partial(_s5_kernel, nb=b, nch=nch, nctx=tc // q),
        grid=(g,),
        in_specs=[_once((rows, 128), lambda gi: (0, OFF_U // 128 + gi // 8)),
                  pl.BlockSpec((1, q * s, q * s), lambda gi: (gi, 0, 0)),
                  pl.BlockSpec((1, q * s, 512), lambda gi: (gi, 0, 0)),
                  pl.BlockSpec((1, 256, q * s), lambda gi: (gi, 0, 0)),
                  pl.BlockSpec((1, 8, 128), lambda gi: (gi, 0, 0))],
        out_specs=pl.BlockSpec((rows, 128), lambda gi: (0, gi // 8)),
        out_shape=jax.ShapeDtypeStruct((rows, BRANCH_W), BF16),
        scratch_shapes=[pltpu.VMEM((rows, 128), F32), pltpu.VMEM((q, r // 2, 128), jnp.uint32),
                        pltpu.VMEM((8, r // 2, q * s), jnp.uint32)] + [pltpu.VMEM((r, 128), F32)] * 6,
        compiler_params=_params("arbitrary"),
        name="s5",
    )(proj.reshape(rows, N_PACK), tmat, uh, hy, pm)
    return y.reshape(b, tt, BRANCH_W)


def _ssd_kernel(*refs, backward, first, nctx_chunks, nchunks):
    if first:
        (xbc_ref, prev_ref, next_ref, z_ref, dt_ref, cw_ref, cb_ref, dtb_ref, alog_ref, dsk_ref,
         sel_ref, o_ref, st_s) = refs
        yin_ref = ng_ref = None
    else:
        (xbc_ref, prev_ref, next_ref, z_ref, dt_ref, cw_ref, cb_ref, dtb_ref, alog_ref, dsk_ref,
         sel_ref, yin_ref, ng_ref, o_ref, st_s) = refs
    qn = SSD_CHUNK
    step = pl.program_id(1)
    if backward:
        c = jnp.where(step < nctx_chunks, nctx_chunks - 1 - step, nchunks - 1 - (step - nctx_chunks))
    else:
        c = step

    @pl.when(step == 0)
    def _():
        st_s[...] = jnp.zeros_like(st_s)

    x = xbc_ref[0].astype(F32)
    has_prev = jnp.logical_and(c != 0, c != nctx_chunks)
    has_next = jnp.logical_and(c != nctx_chunks - 1, c != nchunks - 1)
    prow = jnp.where(has_prev, prev_ref[0, 7:8, :].astype(F32), 0.0)
    nrow = jnp.where(has_next, next_ref[0, 0:1, :].astype(F32), 0.0)
    rid = lax.broadcasted_iota(jnp.int32, (qn, 1), 0)
    xm = jnp.where(rid == 0, prow, pltpu.roll(x, 1, 0))
    xp = jnp.where(rid == qn - 1, nrow, pltpu.roll(x, qn - 1, 0))
    cw = cw_ref[...]
    conv = xm * cw[0:1] + x * cw[1:2] + xp * cw[2:3] + cb_ref[...]
    act = _silu(conv)
    xs = act[:, :BRANCH_W]
    gn = SSD_GROUPS * SSD_STATE
    bm = act[:, BRANCH_W:BRANCH_W + gn].astype(BF16)
    cm = act[:, BRANCH_W + gn:].astype(BF16)

    dt = _softplus(dt_ref[0].astype(F32) + dtb_ref[...])
    a = -jnp.exp(alog_ref[...])
    da = dt * a
    ri = lax.broadcasted_iota(jnp.int32, (qn, qn), 0)
    ci = lax.broadcasted_iota(jnp.int32, (qn, qn), 1)
    causal = (ci >= ri) if backward else (ci <= ri)
    ones_tri = jnp.where(causal, 1.0, 0.0).astype(BF16)
    d1 = da.astype(BF16)
    r1 = da - d1.astype(F32)
    d2 = r1.astype(BF16)
    d3 = (r1 - d2.astype(F32)).astype(BF16)
    cum = (jnp.dot(ones_tri, d1, preferred_element_type=F32)
           + jnp.dot(ones_tri, d2, preferred_element_type=F32)
           + jnp.dot(ones_tri, d3, preferred_element_type=F32))
    edge = cum[0:1] if backward else cum[qn - 1:qn]
    cum_t = cum.T
    dt_t = dt.T
    w_edge = dt * jnp.exp(edge - cum)
    e_in = jnp.exp(cum)

    def expand(v):
        v1 = v.astype(BF16)
        v2 = (v - v1.astype(F32)).astype(BF16)
        return (jnp.dot(v1, sel_ref[...], preferred_element_type=F32)
                + jnp.dot(v2, sel_ref[...], preferred_element_type=F32))

    xw = (xs * expand(w_edge)).astype(BF16)
    e_in_x = expand(e_in)
    xs_b = xs.astype(BF16)
    rpg = SSD_HEADS // SSD_GROUPS
    gw = rpg * SSD_HEAD_DIM
    ys = []
    for g in range(SSD_GROUPS):
        bg = bm[:, g * SSD_STATE:(g + 1) * SSD_STATE]
        cg = cm[:, g * SSD_STATE:(g + 1) * SSD_STATE]
        cb = lax.dot_general(cg, bg, (((1,), (1,)), ((), ())), preferred_element_type=F32)
        st_g = st_s[g * gw:(g + 1) * gw, :]
        y_off = lax.dot_general(cg, st_g.astype(BF16), (((1,), (1,)), ((), ())),
                                preferred_element_type=F32)
        yg = y_off * e_in_x[:, g * gw:(g + 1) * gw]
        parts = []
        for r in range(rpg):
            h = g * rpg + r
            dec = jnp.exp(jnp.where(causal, cum[:, h:h + 1] - cum_t[h:h + 1, :], -jnp.inf))
            wmat = (cb * dec * dt_t[h:h + 1, :]).astype(BF16)
            parts.append(jnp.dot(wmat, xs_b[:, h * SSD_HEAD_DIM:(h + 1) * SSD_HEAD_DIM],
                                 preferred_element_type=F32))
        ys.append(yg + jnp.concatenate(parts, axis=1))
        new = lax.dot_general(xw[:, g * gw:(g + 1) * gw], bg, (((0,), (0,)), ((), ())),
                              preferred_element_type=F32)
        for r in range(rpg):
            h = g * rpg + r
            rows = slice(g * gw + r * SSD_HEAD_DIM, g * gw + (r + 1) * SSD_HEAD_DIM)
            cd = jnp.exp(edge[:, h:h + 1])
            st_s[rows, :] = st_s[rows, :] * cd + new[r * SSD_HEAD_DIM:(r + 1) * SSD_HEAD_DIM, :]
    y = jnp.concatenate(ys, axis=1)
    if first:
        o_ref[0] = y + xs * dsk_ref[...]
    else:
        y = y + yin_ref[0]
        zz = z_ref[0].astype(F32)
        y = y * _silu(zz)
        y = y * lax.rsqrt(jnp.mean(y * y, axis=-1, keepdims=True) + RMS_EPS) * ng_ref[...]
        o_ref[0] = y.astype(o_ref.dtype)


def _ssd_pass(proj, conv_w, conv_b, dt_bias, a_log, d_skip, sel, yin, norm_g, tc, backward):
    b, tt, _ = proj.shape
    qn = SSD_CHUNK
    nchunks = tt // qn
    nctx = tc // qn
    first = yin is None
    last_blk8 = tt // 8 - 1

    def chunk_of(s):
        if backward:
            return jnp.where(s < nctx, nctx - 1 - s, nchunks - 1 - (s - nctx))
        return s

    xbc_blk = OFF_XBC // SSD_CONV_CH
    in_specs = [
        pl.BlockSpec((1, qn, SSD_CONV_CH), lambda bi, s: (bi, chunk_of(s), xbc_blk)),
        pl.BlockSpec((1, 8, SSD_CONV_CH),
                     lambda bi, s: (bi, jnp.maximum(chunk_of(s) * (qn // 8) - 1, 0), xbc_blk)),
        pl.BlockSpec((1, 8, SSD_CONV_CH),
                     lambda bi, s: (bi, jnp.minimum((chunk_of(s) + 1) * (qn // 8), last_blk8), xbc_blk)),
        pl.BlockSpec((1, qn, BRANCH_W), lambda bi, s: (bi, chunk_of(s), OFF_Z // BRANCH_W)),
        pl.BlockSpec((1, qn, 128), lambda bi, s: (bi, chunk_of(s), OFF_DT // 128)),
        pl.BlockSpec((8, SSD_CONV_CH), lambda bi, s: (0, 0)),
        pl.BlockSpec((1, SSD_CONV_CH), lambda bi, s: (0, 0)),
        pl.BlockSpec((1, 128), lambda bi, s: (0, 0)),
        pl.BlockSpec((1, 128), lambda bi, s: (0, 0)),
        pl.BlockSpec((1, BRANCH_W), lambda bi, s: (0, 0)),
        pl.BlockSpec((128, BRANCH_W), lambda bi, s: (0, 0)),
    ]
    args = [proj, proj, proj, proj, proj, conv_w, conv_b, dt_bias, a_log, d_skip, sel]
    if not first:
        in_specs += [pl.BlockSpec((1, qn, BRANCH_W), lambda bi, s: (bi, chunk_of(s), 0)),
                     pl.BlockSpec((1, BRANCH_W), lambda bi, s: (0, 0))]
        args += [yin, norm_g]
    return pl.pallas_call(
        functools.partial(_ssd_kernel, backward=backward, first=first, nctx_chunks=nctx, nchunks=nchunks),
        grid=(b, nchunks),
        in_specs=in_specs,
        out_specs=pl.BlockSpec((1, qn, BRANCH_W), lambda bi, s: (bi, chunk_of(s), 0)),
        out_shape=jax.ShapeDtypeStruct((b, tt, BRANCH_W), F32 if first else BF16),
        scratch_shapes=[pltpu.VMEM((SSD_HEADS * SSD_HEAD_DIM, SSD_STATE), F32)],
        compiler_params=_params("arbitrary", "arbitrary"),
        name="ssd_bwd" if backward else "ssd_fwd",
    )(*args)


def _ssd(proj, conv_w, conv_b, dt_bias, a_log, d_skip, norm_g, tc):
    def lanes(v):
        return jnp.pad(v.astype(F32), (0, 128 - SSD_HEADS)).reshape(1, 128)

    cw = jnp.pad(conv_w.astype(F32), ((0, 5), (0, 0)))
    cb = conv_b.astype(F32).reshape(1, -1)
    dsk = jnp.repeat(d_skip.astype(F32), SSD_HEAD_DIM).reshape(1, -1)
    sel = (jnp.arange(128)[:, None] == (jnp.arange(BRANCH_W)[None, :] // SSD_HEAD_DIM)).astype(BF16)
    y1 = _ssd_pass(proj, cw, cb, lanes(dt_bias[0]), lanes(a_log[0]), dsk, sel, None, None, tc, False)
    return _ssd_pass(proj, cw, cb, lanes(dt_bias[1]), lanes(a_log[1]), jnp.zeros_like(dsk), sel,
                     y1, norm_g.astype(F32).reshape(1, -1), tc, True)


def _merge_kernel(ya_ref, ga_ref, yb_ref, gb_ref, yc_ref, yd_ref, gd_ref, gm_ref, bm_ref,
                  wb_ref, wg_ref, bg_ref, o_ref):
    def f(ref):
        return ref[0].astype(F32)

    g = _gelu_tanh(f(yb_ref))
    glu = g * _sigmoid(jnp.dot(g.astype(BF16), wg_ref[...], preferred_element_type=F32) + bg_ref[...])
    branches = (f(ya_ref) * _silu(f(ga_ref)), glu * _silu(f(gb_ref)), f(yc_ref), f(yd_ref) * _silu(f(gd_ref)))
    acc = None
    for n, br in enumerate(branches):
        gate = _sigmoid(gm_ref[0, :, n * D_MODEL:(n + 1) * D_MODEL].astype(F32) + bm_ref[n:n + 1, :])
        term = gate * jnp.dot(br.astype(BF16), wb_ref[n], preferred_element_type=F32)
        acc = term if acc is None else acc + term
    o_ref[0] = acc.astype(o_ref.dtype)


def _merge(proj, ya, yb, yc, yd, b_merge, w_branch, w_glu, b_glu, skip):
    b, tt, _ = proj.shape
    tm = Q_BLOCK
    bw = BRANCH_W

    def pspec(off):
        return pl.BlockSpec((1, tm, bw), lambda bi, i: (bi, i + skip, off // bw))

    yspec = pl.BlockSpec((1, tm, bw), lambda bi, i: (bi, i + skip, 0))
    once = dict(pipeline_mode=pl.Buffered(1))
    return pl.pallas_call(
        _merge_kernel,
        grid=(b, tt // tm - skip),
        in_specs=[yspec, pspec(OFF_GA), yspec, pspec(OFF_GB), yspec, yspec, pspec(OFF_GD),
                  pl.BlockSpec((1, tm, N_BRANCH * D_MODEL), lambda bi, i: (bi, i + skip, 0)),
                  pl.BlockSpec((N_BRANCH, D_MODEL), lambda bi, i: (0, 0)),
                  pl.BlockSpec((N_BRANCH, bw, D_MODEL), lambda bi, i: (0, 0, 0), **once),
                  pl.BlockSpec((bw, bw), lambda bi, i: (0, 0), **once),
                  pl.BlockSpec((1, bw), lambda bi, i: (0, 0))],
        out_specs=pl.BlockSpec((1, tm, D_MODEL), lambda bi, i: (bi, i + skip, 0)),
        out_shape=jax.ShapeDtypeStruct((b, tt, D_MODEL), BF16),
        compiler_params=_params("arbitrary", "arbitrary"),
        name="merge",
    )(ya, proj, yb, proj, yc, yd, proj, proj, b_merge, w_branch, w_glu, b_glu.reshape(1, -1))


def _out_kernel(m_ref, x_ref, mod_ref, w_ref, lg_ref, lb_ref, o_ref, *, tc, tm, row0):
    i = pl.program_id(1)
    m = mod_ref[0]
    hb = tm // 2
    for half in range(2):
        rows = slice(half * hb, (half + 1) * hb)
        out = jnp.dot(m_ref[0, rows, :], w_ref[...], preferred_element_type=F32)
        row = row0 + i * tm + half * hb + lax.broadcasted_iota(jnp.int32, (hb, 1), 0)
        gate = jnp.where(row < tc, m[4:5], m[5:6])
        v = DEEPNORM_ALPHA * x_ref[0, rows, :] + gate * out
        mu = jnp.mean(v, axis=-1, keepdims=True)
        vc = v - mu
        var = jnp.mean(vc * vc, axis=-1, keepdims=True)
        o_ref[0, rows, :] = vc * lax.rsqrt(var + LN_EPS) * lg_ref[...] + lb_ref[...]


def _out(merged, xa, mod, w_out, ln_g, ln_b, tc, latent_only):
    b, tt, d = xa.shape
    tm = Q_BLOCK
    skip = tc // tm if latent_only else 0
    nblk = tt // tm - skip
    spec = pl.BlockSpec((1, tm, d), lambda bi, i: (bi, i + skip, 0))
    return pl.pallas_call(
        functools.partial(_out_kernel, tc=tc, tm=tm, row0=skip * tm),
        grid=(b, nblk),
        in_specs=[spec, spec,
                  pl.BlockSpec((1, 8, d), lambda bi, i: (bi, 0, 0)),
                  pl.BlockSpec((d, d), lambda bi, i: (0, 0)),
                  pl.BlockSpec((1, d), lambda bi, i: (0, 0)),
                  pl.BlockSpec((1, d), lambda bi, i: (0, 0))],
        out_specs=pl.BlockSpec((1, tm, d), lambda bi, i: (bi, i, 0)),
        out_shape=jax.ShapeDtypeStruct((b, nblk * tm, d), F32),
        compiler_params=_params("arbitrary", "arbitrary"),
        name="out_proj",
    )(merged, xa, mod, w_out, ln_g.reshape(1, -1), ln_b.reshape(1, -1))


def _half_split(w, heads):
    k = w.shape[0]
    return w.reshape(k, heads, ROPE_DIM // 2, 2).transpose(0, 1, 3, 2).reshape(k, heads * ROPE_DIM)


def _pack_w_in(w):
    sp = np.cumsum([0, MLA_Q_LORA, MLA_KV_LORA, MLA_ROPE, BRANCH_W, BRANCH_W, BRANCH_W, BRANCH_W,
                    SSD_CONV_CH, SSD_HEADS, BRANCH_W, BRANCH_W, BRANCH_W, BRANCH_W, N_BRANCH * D_MODEL])
    seg = [w[:, sp[k]:sp[k + 1]] for k in range(14)]
    cq, ckv, kr, ga, u, gb, z, xbc, dt, qd, kd, vd, gd, gm = seg
    kr_hs = _half_split(kr, 1)
    kr_sw = jnp.concatenate([kr_hs[:, 32:], kr_hs[:, :32]], axis=1)
    pad = jnp.zeros((w.shape[0], 128 - SSD_HEADS), w.dtype)
    packed = jnp.concatenate([gm, xbc, ga, u, gb, z, _half_split(qd, 2 * DIFF_HEADS),
                              _half_split(kd, 2 * DIFF_HEADS), vd, gd, cq, ckv, kr_hs, kr_sw, dt, pad],
                             axis=1)
    return packed.astype(BF16)


def _pack_mla(w_uq, w_ukv):
    k = w_uq.shape[0]
    wq = w_uq.reshape(k, MLA_HEADS, MLA_NOPE + MLA_ROPE)
    rope = wq[:, :, MLA_NOPE:].reshape(k, MLA_HEADS, ROPE_DIM // 2, 2)
    ev, od = rope[..., 0], rope[..., 1]
    wq = jnp.concatenate([wq[:, :, :MLA_NOPE], ev, od, od, ev], axis=-1)
    wkv = w_ukv.reshape(w_ukv.shape[0], MLA_HEADS, MLA_NOPE + MLA_V)
    return (wq.transpose(1, 0, 2).astype(BF16),
            wkv[:, :, :MLA_NOPE].transpose(1, 0, 2).astype(BF16),
            wkv[:, :, MLA_NOPE:].transpose(1, 0, 2).astype(BF16))


def _rope_tables(tl, tc):
    rows = tl // GRID_W
    row_id = jnp.repeat(jnp.arange(rows, dtype=F32), GRID_W)
    col_id = jnp.tile(jnp.arange(GRID_W, dtype=F32), rows)
    quarter = ROPE_DIM // 4
    inv_freq = ROPE_BASE ** (-jnp.arange(quarter, dtype=F32) / quarter)
    ang = jnp.concatenate([row_id[:, None] * inv_freq, col_id[:, None] * inv_freq], axis=-1)
    cos = jnp.concatenate([jnp.ones((tc, ROPE_DIM // 2), F32), jnp.cos(ang)], axis=0)
    sin = jnp.concatenate([jnp.zeros((tc, ROPE_DIM // 2), F32), jnp.sin(ang)], axis=0)
    zero = jnp.zeros_like(cos)
    mla = (jnp.concatenate([cos, cos, zero, zero], axis=1), jnp.concatenate([zero, zero, -sin, sin], axis=1))
    diff = (jnp.concatenate([cos, cos, cos, cos], axis=1), jnp.concatenate([-sin, sin, -sin, sin], axis=1))
    return mla, diff


def _layer(xa, cs, p, tables, layer_idx, tc, last):
    b = xa.shape[0]
    mla_tab, diff_tab = tables
    mod = _ada(cs, p['w_ada_stacked'], p['b_ada'], layer_idx)
    d = D_MODEL
    shift, scale, gate = mod[:, :d], mod[:, d:2 * d], mod[:, 2 * d:]
    zeros = jnp.zeros((b, d), F32)
    modb = jnp.stack([jnp.broadcast_to(scale[0], (b, d)), jnp.broadcast_to(shift[0], (b, d)),
                      scale[1:1 + b], shift[1:1 + b],
                      jnp.broadcast_to(gate[0], (b, d)), gate[1:1 + b], zeros, zeros], axis=1)
    proj = _inproj(xa, modb, _pack_w_in(p['w_in']), tc)
    wq, wk, wv = _pack_mla(p['mla_w_uq'], p['mla_w_ukv'])
    ya = _mla(proj, wq, wk, wv, p['mla_q_norm'], p['mla_kv_norm'], mla_tab[0], mla_tab[1], tc, last)
    lam_init = 0.8 - 0.6 * math.exp(-0.3 * layer_idx)
    yd = _diff(proj, p['diff_lambda_q'], p['diff_lambda_k'], p['diff_norm'], diff_tab[0], diff_tab[1],
               lam_init, tc, last)
    yb = _s5(proj, _s5_weights(p['s5_lambda_re'], p['s5_lambda_im'], p['s5_log_dt'], p['s5_b_re'],
                               p['s5_b_im'], p['s5_c_re'], p['s5_c_im'], p['s5_d']), tc)
    yc = _ssd(proj, p['ssd_conv_w'], p['ssd_conv_b'], p['ssd_dt_bias'], p['ssd_a_log'], p['ssd_d'],
              p['ssd_norm'], tc)
    merged = _merge(proj, ya, yb, yc, yd, p['b_merge'], p['w_branch'].astype(BF16),
                    p['s5_w_glu'].astype(BF16), p['s5_b_glu'], tc // Q_BLOCK if last else 0)
    return _out(merged, xa, modb, p['w_out'].astype(BF16), p['ln_g'], p['ln_b'], tc, last)


def kernel(x, c, ctx, c_ctx, w_ada, b_ada, w_in, mla_q_norm, mla_w_uq, mla_kv_norm, mla_w_ukv,
           s5_lambda_re, s5_lambda_im, s5_log_dt, s5_b_re, s5_b_im, s5_c_re, s5_c_im, s5_d,
           s5_w_glu, s5_b_glu, ssd_conv_w, ssd_conv_b, ssd_dt_bias, ssd_a_log, ssd_d, ssd_norm,
           diff_lambda_q, diff_lambda_k, diff_norm, b_merge, w_branch, w_out, ln_g, ln_b):
    b, tl, d = x.shape
    tc = ctx.shape[1]
    stacked = dict(
        w_ada=w_ada, b_ada=b_ada, w_in=w_in, mla_q_norm=mla_q_norm, mla_w_uq=mla_w_uq,
        mla_kv_norm=mla_kv_norm, mla_w_ukv=mla_w_ukv, s5_lambda_re=s5_lambda_re, s5_lambda_im=s5_lambda_im,
        s5_log_dt=s5_log_dt, s5_b_re=s5_b_re, s5_b_im=s5_b_im, s5_c_re=s5_c_re, s5_c_im=s5_c_im, s5_d=s5_d,
        s5_w_glu=s5_w_glu, s5_b_glu=s5_b_glu, ssd_conv_w=ssd_conv_w, ssd_conv_b=ssd_conv_b,
        ssd_dt_bias=ssd_dt_bias, ssd_a_log=ssd_a_log, ssd_d=ssd_d, ssd_norm=ssd_norm,
        diff_lambda_q=diff_lambda_q, diff_lambda_k=diff_lambda_k, diff_norm=diff_norm, b_merge=b_merge,
        w_branch=w_branch, w_out=w_out, ln_g=ln_g, ln_b=ln_b)
    tables = _rope_tables(tl, tc)
    cs = jnp.concatenate([c_ctx[None], c, jnp.zeros((8 - 1 - b, d), c.dtype)], axis=0)
    xa = jnp.concatenate([ctx, x], axis=1)
    depth = w_in.shape[0]
    for i in range(depth):
        p = {k: v[i] for k, v in stacked.items() if k != 'w_ada'}
        p['w_ada_stacked'] = w_ada
        xa = _layer(xa, cs, p, tables, i, tc, i == depth - 1)
    return xa
```
